```python
import math
import jax, jax.numpy as jnp
from jax import lax
import numpy as np

D_MODEL = 2048
BATCH = 1
SEQ = 8192
DEPTH = 1

CHUNK = 64
Q_BLOCK = 128
HEAD_DIM = 128
N_HEADS_DIFF = D_MODEL // 256
DIFF_QK_DIM = HEAD_DIM // 2
N_HEADS_FOX = D_MODEL // 256
WIDTH_DIFF = N_HEADS_DIFF * HEAD_DIM
WIDTH_FOX = N_HEADS_FOX * HEAD_DIM
D_IN_PROJ = 3 * WIDTH_DIFF + 3 * WIDTH_FOX + N_HEADS_FOX
D_FF = 4 * D_MODEL
EPS = 1e-6

OFF_QA = WIDTH_DIFF
OFF_KA = 2 * WIDTH_DIFF
OFF_VA = 3 * WIDTH_DIFF
OFF_QB = 3 * WIDTH_DIFF + WIDTH_FOX
OFF_KB = 3 * WIDTH_DIFF + 2 * WIDTH_FOX
OFF_VB = 3 * WIDTH_DIFF + 3 * WIDTH_FOX

kernel_name = "hybrid_diff_fox_gated_block"


def rms_norm(x, g):
    xf = x.astype(jnp.float32)
    r = lax.rsqrt(jnp.mean(xf * xf, axis=-1, keepdims=True) + EPS)
    return (xf * r).astype(x.dtype) * g


def alibi_slopes(n_heads):
    return 2.0 ** (-8.0 * jnp.arange(1, n_heads + 1, dtype=jnp.float32) / n_heads)


def diff_attention(q1, q2, k1, k2, v, lam, slopes):
    S = q1.shape[2]
    scale = DIFF_QK_DIM ** -0.5
    outs = []
    for start in range(0, S, Q_BLOCK):
        end = start + Q_BLOCK
        t = jnp.arange(start, end)[:, None]
        s = jnp.arange(end)[None, :]
        allowed = (s // CHUNK) <= (t // CHUNK)
        bias = -slopes[:, None, None] * jnp.abs(t - s).astype(jnp.float32)
        bias = jnp.where(allowed[None], bias, -jnp.inf)

        def probs(q, k):
            logits = jnp.einsum('bhqd,bhkd->bhqk', q[:, :, start:end], k[:, :, :end]).astype(jnp.float32)
            return jax.nn.softmax(logits * scale + bias[None], axis=-1)

        p = probs(q1, k1) - lam * probs(q2, k2)
        outs.append(jnp.einsum('bhqk,bhkd->bhqd', p.astype(v.dtype), v[:, :, :end]))
    return jnp.concatenate(outs, axis=2)


def forgetting_attention(q, k, v, log_f_cum):
    S = q.shape[2]
    scale = HEAD_DIM ** -0.5
    outs = []
    for start in range(0, S, Q_BLOCK):
        end = start + Q_BLOCK
        t = jnp.arange(start, end)[:, None]
        s = jnp.arange(end)[None, :]
        allowed = s <= t
        decay = log_f_cum[:, :, start:end, None] - log_f_cum[:, :, None, :end]
        logits = jnp.einsum('bhqd,bhkd->bhqk', q[:, :, start:end], k[:, :, :end]).astype(jnp.float32)
        logits = jnp.where(allowed[None, None], logits * scale + decay, -jnp.inf)
        p = jax.nn.softmax(logits, axis=-1)
        outs.append(jnp.einsum('bhqk,bhkd->bhqd', p.astype(v.dtype), v[:, :, :end]))
    return jnp.concatenate(outs, axis=2)


def setup_inputs(seed: int = 0) -> dict:
    key = jax.random.key(seed)
    ks = jax.random.split(key, 24)
    f32 = jnp.float32

    def dense(k, fan_in, fan_out):
        return jax.random.normal(k, (DEPTH, fan_in, fan_out), f32) * fan_in ** -0.5

    def gain(k, n):
        return 1.0 + 0.05 * jax.random.normal(k, (DEPTH, n), f32)

    def small(k, n, scale=0.01):
        return scale * jax.random.normal(k, (DEPTH, n), f32)

    return {
        "x": jax.random.normal(ks[0], (BATCH, SEQ, D_MODEL), f32),
        "norm_mix": gain(ks[1], D_MODEL),
        "w_in": dense(ks[2], D_MODEL, D_IN_PROJ),
        "b_forget": 2.0 + 0.5 * jax.random.normal(ks[3], (DEPTH, N_HEADS_FOX), f32),
        "qnorm_diff": gain(ks[4], DIFF_QK_DIM),
        "knorm_diff": gain(ks[5], DIFF_QK_DIM),
        "lambda_q1": small(ks[6], DIFF_QK_DIM, 0.1),
        "lambda_k1": small(ks[7], DIFF_QK_DIM, 0.1),
        "lambda_q2": small(ks[8], DIFF_QK_DIM, 0.1),
        "lambda_k2": small(ks[9], DIFF_QK_DIM, 0.1),
        "subln_diff": gain(ks[10], HEAD_DIM),
        "qnorm_fox": gain(ks[11], HEAD_DIM),
        "knorm_fox": gain(ks[12], HEAD_DIM),
        "w_branch_diff": dense(ks[13], WIDTH_DIFF, D_MODEL),
        "w_branch_fox": dense(ks[14], WIDTH_FOX, D_MODEL),
        "w_gate": dense(ks[15], D_MODEL, 2 * D_MODEL),
        "b_gate": small(ks[16], 2 * D_MODEL),
        "w_out": dense(ks[17], D_MODEL, D_MODEL),
        "norm_mlp": gain(ks[18], D_MODEL),
        "w_mlp_up": dense(ks[19], D_MODEL, D_FF),
        "w_mlp_down": dense(ks[20], D_FF, D_MODEL),
    }


def reference(x, norm_mix, w_in, b_forget, qnorm_diff, knorm_diff, lambda_q1, lambda_k1,
              lambda_q2, lambda_k2, subln_diff, qnorm_fox, knorm_fox, w_branch_diff, w_branch_fox,
              w_gate, b_gate, w_out, norm_mlp, w_mlp_up, w_mlp_down):
    B, S, D = x.shape
    slopes = alibi_slopes(N_HEADS_DIFF)
    for layer in range(DEPTH):
        lambda_init = 0.8 - 0.6 * math.exp(-0.3 * layer)

        h = rms_norm(x, norm_mix[layer])
        proj = h @ w_in[layer]
        qa = proj[..., :OFF_QA]
        ka = proj[..., OFF_QA:OFF_KA]
        va = proj[..., OFF_KA:OFF_VA]
        qb = proj[..., OFF_VA:OFF_QB]
        kb = proj[..., OFF_QB:OFF_KB]
        vb = proj[..., OFF_KB:OFF_VB]
        f_logit = proj[..., OFF_VB:]

        qa = rms_norm(qa.reshape(B, S, N_HEADS_DIFF, 2, DIFF_QK_DIM), qnorm_diff[layer])
        ka = rms_norm(ka.reshape(B, S, N_HEADS_DIFF, 2, DIFF_QK_DIM), knorm_diff[layer])
        qa = jnp.transpose(qa, (0, 2, 3, 1, 4))
        ka = jnp.transpose(ka, (0, 2, 3, 1, 4))
        va = jnp.transpose(va.reshape(B, S, N_HEADS_DIFF, HEAD_DIM), (0, 2, 1, 3))
        lam = (jnp.exp(jnp.sum(lambda_q1[layer] * lambda_k1[layer]).astype(jnp.float32))
               - jnp.exp(jnp.sum(lambda_q2[layer] * lambda_k2[layer]).astype(jnp.float32))
               + lambda_init)
        oa = diff_attention(qa[:, :, 0], qa[:, :, 1], ka[:, :, 0], ka[:, :, 1], va, lam, slopes)
        oa = rms_norm(oa, subln_diff[layer]) * (1.0 - lambda_init)
        oa = jnp.transpose(oa, (0, 2, 1, 3)).reshape(B, S, WIDTH_DIFF)

        qb = rms_norm(qb.reshape(B, S, N_HEADS_FOX, HEAD_DIM), qnorm_fox[layer])
        kb = rms_norm(kb.reshape(B, S, N_HEADS_FOX, HEAD_DIM), knorm_fox[layer])
        qb = jnp.transpose(qb, (0, 2, 1, 3))
        kb = jnp.transpose(kb, (0, 2, 1, 3))
        vb = jnp.transpose(vb.reshape(B, S, N_HEADS_FOX, HEAD_DIM), (0, 2, 1, 3))
        log_f = jax.nn.log_sigmoid(f_logit.astype(jnp.float32) + b_forget[layer].astype(jnp.float32))
        log_f_cum = jnp.transpose(jnp.cumsum(log_f, axis=1), (0, 2, 1))
        ob = forgetting_attention(qb, kb, vb, log_f_cum)
        ob = jnp.transpose(ob, (0, 2, 1, 3)).reshape(B, S, WIDTH_FOX)

        gates = jax.nn.sigmoid((h @ w_gate[layer] + b_gate[layer]).astype(jnp.float32)).astype(h.dtype)
        g_diff = gates[..., :D_MODEL]
        g_fox = gates[..., D_MODEL:]
        merged = g_diff * (oa @ w_branch_diff[layer]) + g_fox * (ob @ w_branch_fox[layer])
        x = x + merged @ w_out[layer]

        h2 = rms_norm(x, norm_mlp[layer])
        u = jax.nn.relu(h2 @ w_mlp_up[layer])
        x = x + (u * u) @ w_mlp_down[layer]
    return x
```

```python
import functools

import jax
import jax.numpy as jnp
from jax import lax
from jax.experimental import pallas as pl
from jax.experimental.pallas import tpu as pltpu

D_MODEL = 2048
SEQ = 8192
HEAD_DIM = 128
N_HEADS = 8
QK_DIFF = 64
WIDTH = N_HEADS * HEAD_DIM
D_FF = 4 * D_MODEL
EPS = 1e-6
LAMBDA_INIT = 0.8 - 0.6 * 1.0
NEG_BIG = -1e30

TQ = 256
TK = 256
VMEM_LIMIT = 56 * 1024 * 1024

_f32 = jnp.float32
_bf16 = jnp.bfloat16


def _cparams(sem):
    return pltpu.CompilerParams(dimension_semantics=sem, vmem_limit_bytes=VMEM_LIMIT)


def _dot(a, b):
    return jnp.dot(a, b, preferred_element_type=_f32)


def _rmsnorm_kernel(x_ref, g_ref, o_ref):
    x = x_ref[...]
    r = lax.rsqrt(jnp.mean(x * x, axis=-1, keepdims=True) + EPS)
    o_ref[...] = ((x * r) * g_ref[...]).astype(o_ref.dtype)


def _rmsnorm(x, g, tm=512):
    s, d = x.shape
    return pl.pallas_call(
        _rmsnorm_kernel,
        grid=(s // tm,),
        in_specs=[pl.BlockSpec((tm, d), lambda i: (i, 0)),
                  pl.BlockSpec((1, d), lambda i: (0, 0))],
        out_specs=pl.BlockSpec((tm, d), lambda i: (i, 0)),
        out_shape=jax.ShapeDtypeStruct((s, d), _bf16),
        compiler_params=_cparams(("parallel",)),
        name="rmsnorm_in",
    )(x, g.reshape(1, d))


def _head_rmsnorm(y, gain_row, groups):
    y2 = y * y
    if groups == 1:
        r = lax.rsqrt(jnp.mean(y2, axis=-1, keepdims=True) + EPS)
    else:
        lane = lax.broadcasted_iota(jnp.int32, y.shape, 1)
        lo = lane < QK_DIFF
        s_lo = jnp.sum(jnp.where(lo, y2, 0.0), axis=-1, keepdims=True)
        s_hi = jnp.sum(jnp.where(lo, 0.0, y2), axis=-1, keepdims=True)
        r = jnp.where(lo, lax.rsqrt(s_lo * (1.0 / QK_DIFF) + EPS),
                      lax.rsqrt(s_hi * (1.0 / QK_DIFF) + EPS))
    return (y * r) * gain_row


def _proj_qdiff_kernel(a_ref, w_ref, g_ref, q1_ref, q2_ref):
    acc = _dot(a_ref[...], w_ref[...])
    for hh in range(N_HEADS):
        y = acc[:, hh * HEAD_DIM:(hh + 1) * HEAD_DIM]
        yn = _head_rmsnorm(y, g_ref[...], 2) * (QK_DIFF ** -0.5)
        lane = lax.broadcasted_iota(jnp.int32, yn.shape, 1)
        lo = lane < QK_DIFF
        q1_ref[hh] = jnp.where(lo, yn, 0.0).T.astype(q1_ref.dtype)
        q2_ref[hh] = jnp.where(lo, 0.0, yn).T.astype(q2_ref.dtype)


def _proj_qfox_kernel(a_ref, w_ref, g_ref, q_ref):
    acc = _dot(a_ref[...], w_ref[...])
    for hh in range(N_HEADS):
        y = acc[:, hh * HEAD_DIM:(hh + 1) * HEAD_DIM]
        yn = _head_rmsnorm(y, g_ref[...], 1) * (HEAD_DIM ** -0.5)
        q_ref[hh] = yn.T.astype(q_ref.dtype)


def _proj_k_kernel(a_ref, w_ref, g_ref, k_ref, *, groups):
    acc = _dot(a_ref[...], w_ref[...])
    for hh in range(N_HEADS):
        y = acc[:, hh * HEAD_DIM:(hh + 1) * HEAD_DIM]
        k_ref[hh] = _head_rmsnorm(y, g_ref[...], groups).astype(k_ref.dtype)


def _proj_vt_kernel(a_ref, w_ref, vt_ref):
    acc = _dot(a_ref[...], w_ref[...])
    tm = acc.shape[0]
    for hh in range(N_HEADS):
        for c in range(tm // TK):
            blk = acc[c * TK:(c + 1) * TK, hh * HEAD_DIM:(hh + 1) * HEAD_DIM]
            vt_ref[hh, c] = blk.T.astype(vt_ref.dtype)


def _proj_gates_kernel(a_ref, w_ref, b_ref, o_ref):
    z = _dot(a_ref[...], w_ref[...]) + b_ref[...]
    o_ref[...] = (1.0 / (1.0 + jnp.exp(-z))).astype(o_ref.dtype)


def _proj_call(kernel, h, w, extras, out_shapes, out_specs, name, tm=1024):
    s, d = h.shape
    n = w.shape[1]
    in_specs = [pl.BlockSpec((tm, d), lambda i: (i, 0)),
                pl.BlockSpec((d, n), lambda i: (0, 0))]
    in_specs += [pl.BlockSpec(e.shape, lambda i: (0, 0)) for e in extras]
    return pl.pallas_call(
        kernel,
        grid=(s // tm,),
        in_specs=in_specs,
        out_specs=out_specs,
        out_shape=out_shapes,
        compiler_params=_cparams(("parallel",)),
        name=name,
    )(h, w, *extras)


def _t_spec(tm):
    return pl.BlockSpec((N_HEADS, HEAD_DIM, tm), lambda i: (0, 0, i))


def _k_spec(tm):
    return pl.BlockSpec((N_HEADS, tm, HEAD_DIM), lambda i: (0, i, 0))


def _gates(h, w, b, tm=1024, tn=1024):
    s, d = h.shape
    n = w.shape[1]
    return pl.pallas_call(
        _proj_gates_kernel,
        grid=(s // tm, n // tn),
        in_specs=[pl.BlockSpec((tm, d), lambda i, j: (i, 0)),
                  pl.BlockSpec((d, tn), lambda i, j: (0, j)),
                  pl.BlockSpec((1, tn), lambda i, j: (0, j))],
        out_specs=pl.BlockSpec((tm, tn), lambda i, j: (i, j)),
        out_shape=jax.ShapeDtypeStruct((s, n), _bf16),
        compiler_params=_cparams(("parallel", "parallel")),
        name="proj_gates",
    )(h, w, b.reshape(1, n))


F_ROWS = 16
F_CHUNK = 256


def _forget_kernel(wt_ref, a_ref, b_ref, ft_ref, fsb_ref, carry_ref):
    i = pl.program_id(0)

    @pl.when(i == 0)
    def _():
        carry_ref[...] = jnp.zeros_like(carry_ref)

    z = lax.dot_general(wt_ref[...], a_ref[...], (((1,), (1,)), ((), ())),
                        preferred_element_type=_f32) + b_ref[...]
    logf = jnp.minimum(z, 0.0) - jnp.log(1.0 + jnp.exp(-jnp.abs(z)))
    tm = logf.shape[1]
    r = lax.broadcasted_iota(jnp.int32, (F_CHUNK, F_CHUNK), 0)
    c = lax.broadcasted_iota(jnp.int32, (F_CHUNK, F_CHUNK), 1)
    upper = jnp.where(r <= c, 1.0, 0.0).astype(_bf16)
    carry = carry_ref[...]
    for ch in range(tm // F_CHUNK):
        x = logf[:, ch * F_CHUNK:(ch + 1) * F_CHUNK]
        hi = x.astype(_bf16)
        r1 = x - hi.astype(_f32)
        mid = r1.astype(_bf16)
        lo = (r1 - mid.astype(_f32)).astype(_bf16)
        pre = _dot(hi, upper) + _dot(mid, upper) + _dot(lo, upper) + carry
        ft_ref[:, ch * F_CHUNK:(ch + 1) * F_CHUNK] = pre
        carry = pre[:, F_CHUNK - 1:F_CHUNK]
        for hh in range(N_HEADS):
            for sub in range(F_CHUNK // HEAD_DIM):
                row = pre[hh:hh + 1, sub * HEAD_DIM:(sub + 1) * HEAD_DIM]
                blk = jnp.broadcast_to(row, (HEAD_DIM, HEAD_DIM)).T
                base = ch * F_CHUNK + sub * HEAD_DIM
                fsb_ref[hh, base:base + HEAD_DIM, :] = blk
    carry_ref[...] = carry


def _forget(h, wf_t, b_col, tm=1024):
    s, d = h.shape
    return pl.pallas_call(
        _forget_kernel,
        grid=(s // tm,),
        in_specs=[pl.BlockSpec((F_ROWS, d), lambda i: (0, 0)),
                  pl.BlockSpec((tm, d), lambda i: (i, 0)),
                  pl.BlockSpec((F_ROWS, 1), lambda i: (0, 0))],
        out_specs=[pl.BlockSpec((F_ROWS, tm), lambda i: (0, i)),
                   pl.BlockSpec((N_HEADS, tm, HEAD_DIM), lambda i: (0, i, 0))],
        out_shape=[jax.ShapeDtypeStruct((F_ROWS, s), _f32),
                   jax.ShapeDtypeStruct((N_HEADS, s, HEAD_DIM), _f32)],
        scratch_shapes=[pltpu.VMEM((F_ROWS, 1), _f32)],
        compiler_params=_cparams(("arbitrary",)),
        name="forget_gate",
    )(wf_t, h, b_col)


def _flash_update(z, shift, vt, m_ref, l_ref, acc_ref, idx):
    m_old = m_ref[idx]
    m_new = jnp.maximum(m_old, jnp.max(z, axis=0, keepdims=True) + shift)
    alpha = jnp.exp(m_old - m_new)
    p = jnp.exp(z - (m_new - shift))
    l_ref[idx] = alpha * l_ref[idx] + jnp.sum(p, axis=0, keepdims=True)
    acc_ref[idx] = alpha * acc_ref[idx] + _dot(vt, p.astype(_bf16))
    m_ref[idx] = m_new


def _diff_attn_kernel(slope_ref, q1_ref, q2_ref, k_ref, vt_ref, lq1_ref, lk1_ref, lq2_ref, lk2_ref,
                      gsub_ref, o_ref, m_ref, l_ref, acc_ref):
    h = pl.program_id(0)
    qi = pl.program_id(1)
    neg_slope = -slope_ref[h]
    m_ref[...] = jnp.full_like(m_ref, NEG_BIG)
    l_ref[...] = jnp.zeros_like(l_ref)
    acc_ref[...] = jnp.zeros_like(acc_ref)
    q1 = q1_ref[0]
    q2 = q2_ref[0]
    lane = lax.broadcasted_iota(jnp.int32, (TK, TQ), 1)
    sub = lax.broadcasted_iota(jnp.int32, (TK, TQ), 0)
    delta = (lane - sub).astype(_f32)

    def off_diag(kv, carry):
        start = pl.multiple_of(kv * TK, TK)
        k = k_ref[0, pl.ds(start, TK), :]
        vt = vt_ref[0, kv]
        bias = delta * neg_slope
        shift = ((qi - kv) * TQ).astype(_f32) * neg_slope
        _flash_update(_dot(k, q1) + bias, shift, vt, m_ref, l_ref, acc_ref, 0)
        _flash_update(_dot(k, q2) + bias, shift, vt, m_ref, l_ref, acc_ref, 1)
        return carry

    lax.fori_loop(0, qi, off_diag, 0)

    start = pl.multiple_of(qi * TK, TK)
    k = k_ref[0, pl.ds(start, TK), :]
    vt = vt_ref[0, qi]
    allowed = (sub // 64) <= (lane // 64)
    bias = jnp.where(allowed, jnp.abs(delta) * neg_slope, NEG_BIG)
    _flash_update(_dot(k, q1) + bias, 0.0, vt, m_ref, l_ref, acc_ref, 0)
    _flash_update(_dot(k, q2) + bias, 0.0, vt, m_ref, l_ref, acc_ref, 1)

    lam = (jnp.exp(jnp.sum(lq1_ref[...] * lk1_ref[...], axis=-1, keepdims=True))
           - jnp.exp(jnp.sum(lq2_ref[...] * lk2_ref[...], axis=-1, keepdims=True))
           + LAMBDA_INIT)
    o = acc_ref[0] / l_ref[0] - lam * (acc_ref[1] / l_ref[1])
    r = lax.rsqrt(jnp.mean(o * o, axis=0, keepdims=True) + EPS)
    y = (o * r) * gsub_ref[...] * (1.0 - LAMBDA_INIT)
    o_ref[...] = y.T.astype(o_ref.dtype)


def _diff_attention(slopes, q1t, q2t, k, vt, lq1, lk1, lq2, lk2, gsub_col):
    nq = SEQ // TQ
    nkv = SEQ // TK
    vec = lambda: pl.BlockSpec((1, QK_DIFF), lambda h, q: (0, 0))
    return pl.pallas_call(
        _diff_attn_kernel,
        grid=(N_HEADS, nq),
        in_specs=[pl.BlockSpec(memory_space=pltpu.SMEM),
                  pl.BlockSpec((1, HEAD_DIM, TQ), lambda h, q: (h, 0, q)),
                  pl.BlockSpec((1, HEAD_DIM, TQ), lambda h, q: (h, 0, q)),
                  pl.BlockSpec((1, SEQ, HEAD_DIM), lambda h, q: (h, 0, 0)),
                  pl.BlockSpec((1, nkv, HEAD_DIM, TK), lambda h, q: (h, 0, 0, 0)),
                  vec(), vec(), vec(), vec(),
                  pl.BlockSpec((HEAD_DIM, 1), lambda h, q: (0, 0))],
        out_specs=pl.BlockSpec((TQ, HEAD_DIM), lambda h, q: (q, h)),
        out_shape=jax.ShapeDtypeStruct((SEQ, WIDTH), _bf16),
        scratch_shapes=[pltpu.VMEM((2, 1, TQ), _f32),
                        pltpu.VMEM((2, 1, TQ), _f32),
                        pltpu.VMEM((2, HEAD_DIM, TQ), _f32)],
        compiler_params=_cparams(("parallel", "parallel")),
        name="diff_attention",
    )(slopes, q1t, q2t, k, vt, lq1, lk1, lq2, lk2, gsub_col)


def _fox_attn_kernel(q_ref, k_ref, vt_ref, ft_ref, fsb_ref, o_ref, m_ref, l_ref, acc_ref):
    qi = pl.program_id(1)
    m_ref[...] = jnp.full_like(m_ref, NEG_BIG)
    l_ref[...] = jnp.zeros_like(l_ref)
    acc_ref[...] = jnp.zeros_like(acc_ref)
    q = q_ref[0]
    ft = ft_ref[0]

    def logits_minus_ft(kv):
        start = pl.multiple_of(kv * TK, TK)
        k = k_ref[0, pl.ds(start, TK), :]
        fs = fsb_ref[0, pl.ds(start, TK), :]
        fs = jnp.concatenate([fs] * (TQ // HEAD_DIM), axis=1)
        return _dot(k, q) - fs

    def off_diag(kv, carry):
        _flash_update(logits_minus_ft(kv), ft, vt_ref[0, kv], m_ref, l_ref, acc_ref, 0)
        return carry

    lax.fori_loop(0, qi, off_diag, 0)

    lane = lax.broadcasted_iota(jnp.int32, (TK, TQ), 1)
    sub = lax.broadcasted_iota(jnp.int32, (TK, TQ), 0)
    z = jnp.where(sub <= lane, logits_minus_ft(qi), NEG_BIG)
    _flash_update(z, ft, vt_ref[0, qi], m_ref, l_ref, acc_ref, 0)

    o = acc_ref[0] / l_ref[0]
    o_ref[...] = o.T.astype(o_ref.dtype)


def _fox_attention(qt, k, vt, ft3, fsb):
    nq = SEQ // TQ
    nkv = SEQ // TK
    return pl.pallas_call(
        _fox_attn_kernel,
        grid=(N_HEADS, nq),
        in_specs=[pl.BlockSpec((1, HEAD_DIM, TQ), lambda h, q: (h, 0, q)),
                  pl.BlockSpec((1, SEQ, HEAD_DIM), lambda h, q: (h, 0, 0)),
                  pl.BlockSpec((1, nkv, HEAD_DIM, TK), lambda h, q: (h, 0, 0, 0)),
                  pl.BlockSpec((1, 1, TQ), lambda h, q: (h, 0, q)),
                  pl.BlockSpec((1, SEQ, HEAD_DIM), lambda h, q: (h, 0, 0))],
        out_specs=pl.BlockSpec((TQ, HEAD_DIM), lambda h, q: (q, h)),
        out_shape=jax.ShapeDtypeStruct((SEQ, WIDTH), _bf16),
        scratch_shapes=[pltpu.VMEM((1, 1, TQ), _f32),
                        pltpu.VMEM((1, 1, TQ), _f32),
                        pltpu.VMEM((1, HEAD_DIM, TQ), _f32)],
        compiler_params=_cparams(("parallel", "parallel")),
        name="fox_attention",
    )(qt, k, vt, ft3, fsb)


def _merge_kernel(oa_ref, ob_ref, g_ref, x_ref, wbd_ref, wbf_ref, wout_ref, gm_ref, x1_ref, h2_ref):
    a = _dot(oa_ref[...], wbd_ref[...])
    b = _dot(ob_ref[...], wbf_ref[...])
    g = g_ref[...].astype(_f32)
    merged = g[:, :D_MODEL] * a + g[:, D_MODEL:] * b
    x1 = x_ref[...] + _dot(merged.astype(_bf16), wout_ref[...])
    x1_ref[...] = x1
    r = lax.rsqrt(jnp.mean(x1 * x1, axis=-1, keepdims=True) + EPS)
    h2_ref[...] = ((x1 * r) * gm_ref[...]).astype(h2_ref.dtype)


def _merge(oa, ob, gates, x, wbd, wbf, wout, gm, tm=256):
    s, d = x.shape
    const = lambda shape: pl.BlockSpec(shape, lambda i: (0, 0), pipeline_mode=pl.Buffered(1))
    return pl.pallas_call(
        _merge_kernel,
        grid=(s // tm,),
        in_specs=[pl.BlockSpec((tm, WIDTH), lambda i: (i, 0)),
                  pl.BlockSpec((tm, WIDTH), lambda i: (i, 0)),
                  pl.BlockSpec((tm, 2 * d), lambda i: (i, 0)),
                  pl.BlockSpec((tm, d), lambda i: (i, 0)),
                  const((WIDTH, d)), const((WIDTH, d)), const((d, d)),
                  pl.BlockSpec((1, d), lambda i: (0, 0))],
        out_specs=[pl.BlockSpec((tm, d), lambda i: (i, 0)),
                   pl.BlockSpec((tm, d), lambda i: (i, 0))],
        out_shape=[jax.ShapeDtypeStruct((s, d), _f32),
                   jax.ShapeDtypeStruct((s, d), _bf16)],
        compiler_params=_cparams(("parallel",)),
        name="merge_out_proj",
    )(oa, ob, gates, x, wbd, wbf, wout, gm.reshape(1, d))


def _mlp_kernel(h_ref, wu_ref, wd_ref, x_ref, o_ref):
    k = pl.program_id(1)

    @pl.when(k == 0)
    def _():
        o_ref[...] = x_ref[...]

    u = jnp.maximum(_dot(h_ref[...], wu_ref[...]), 0.0)
    o_ref[...] += _dot((u * u).astype(_bf16), wd_ref[...])


def _mlp(h2, wu, wd, x1, tm=512, tf=1024):
    s, d = x1.shape
    f = wu.shape[1]
    return pl.pallas_call(
        _mlp_kernel,
        grid=(s // tm, f // tf),
        in_specs=[pl.BlockSpec((tm, d), lambda i, k: (i, 0)),
                  pl.BlockSpec((d, tf), lambda i, k: (0, k)),
                  pl.BlockSpec((tf, d), lambda i, k: (k, 0)),
                  pl.BlockSpec((tm, d), lambda i, k: (i, 0))],
        out_specs=pl.BlockSpec((tm, d), lambda i, k: (i, 0)),
        out_shape=jax.ShapeDtypeStruct((s, d), _f32),
        compiler_params=_cparams(("parallel", "arbitrary")),
        name="mlp_relu2",
    )(h2, wu, wd, x1)


def kernel(x, norm_mix, w_in, b_forget, qnorm_diff, knorm_diff, lambda_q1, lambda_k1, lambda_q2, lambda_k2,
           subln_diff, qnorm_fox, knorm_fox, w_branch_diff, w_branch_fox, w_gate, b_gate, w_out, norm_mlp,
           w_mlp_up, w_mlp_down):
    assert x.shape == (1, SEQ, D_MODEL)
    x2 = x[0]
    w_in0 = w_in[0]
    region = lambda r: w_in0[:, r * WIDTH:(r + 1) * WIDTH].astype(_bf16)

    h = _rmsnorm(x2, norm_mix[0])

    g_qd = jnp.tile(qnorm_diff[0], 2).reshape(1, HEAD_DIM)
    g_kd = jnp.tile(knorm_diff[0], 2).reshape(1, HEAD_DIM)
    tm = 1024
    t_shape = jax.ShapeDtypeStruct((N_HEADS, HEAD_DIM, SEQ), _bf16)
    k_shape = jax.ShapeDtypeStruct((N_HEADS, SEQ, HEAD_DIM), _bf16)
    vt_shape = jax.ShapeDtypeStruct((N_HEADS, SEQ // TK, HEAD_DIM, TK), _bf16)
    vt_spec = pl.BlockSpec((N_HEADS, tm // TK, HEAD_DIM, TK), lambda i: (0, i, 0, 0))

    q1t, q2t = _proj_call(_proj_qdiff_kernel, h, region(0), [g_qd], [t_shape, t_shape],
                          [_t_spec(tm), _t_spec(tm)], "proj_q_diff")
    ka = _proj_call(functools.partial(_proj_k_kernel, groups=2), h, region(1), [g_kd], k_shape,
                    _k_spec(tm), "proj_k_diff")
    vat = _proj_call(_proj_vt_kernel, h, region(2), [], vt_shape, vt_spec, "proj_v_diff")
    qbt = _proj_call(_proj_qfox_kernel, h, region(3), [qnorm_fox[0].reshape(1, HEAD_DIM)], t_shape,
                     _t_spec(tm), "proj_q_fox")
    kb = _proj_call(functools.partial(_proj_k_kernel, groups=1), h, region(4),
                    [knorm_fox[0].reshape(1, HEAD_DIM)], k_shape, _k_spec(tm), "proj_k_fox")
    vbt = _proj_call(_proj_vt_kernel, h, region(5), [], vt_shape, vt_spec, "proj_v_fox")

    wf_t = jnp.zeros((F_ROWS, D_MODEL), _bf16).at[:N_HEADS].set(w_in0[:, 6 * WIDTH:].T.astype(_bf16))
    bf_col = jnp.zeros((F_ROWS, 1), _f32).at[:N_HEADS, 0].set(b_forget[0])
    ft, fsb = _forget(h, wf_t, bf_col)
    ft3 = ft.reshape(F_ROWS, 1, SEQ)

    gates = _gates(h, w_gate[0].astype(_bf16), b_gate[0])

    slopes = 2.0 ** (-8.0 * jnp.arange(1, N_HEADS + 1, dtype=_f32) / N_HEADS)
    row = lambda v: v[0].reshape(1, QK_DIFF)
    oa = _diff_attention(slopes, q1t, q2t, ka, vat, row(lambda_q1), row(lambda_k1), row(lambda_q2),
                         row(lambda_k2), subln_diff[0].reshape(HEAD_DIM, 1))
    ob = _fox_attention(qbt, kb, vbt, ft3, fsb)

    x1, h2 = _merge(oa, ob, gates, x2, w_branch_diff[0].astype(_bf16), w_branch_fox[0].astype(_bf16),
                    w_out[0].astype(_bf16), norm_mlp[0])
    out = _mlp(h2, w_mlp_up[0].astype(_bf16), w_mlp_down[0].astype(_bf16), x1)
    return out[None]
```

```python
import functools

import jax
import jax.numpy as jnp
from jax import lax
from jax.experimental import pallas as pl
from jax.experimental.pallas import tpu as pltpu

D_MODEL = 2048
SEQ = 8192
HEAD_DIM = 128
N_HEADS = 8
QK_DIFF = 64
WIDTH = N_HEADS * HEAD_DIM
D_FF = 4 * D_MODEL
EPS = 1e-6
LAMBDA_INIT = 0.8 - 0.6 * 1.0
NEG_BIG = -1e30

TQ = 1024
TQS = 256
TK = 256
TKM = TQ
NSUB = TQ // TQS
LOG2E = 1.4426950408889634
VMEM_LIMIT = 56 * 1024 * 1024

_f32 = jnp.float32
_bf16 = jnp.bfloat16


def _cparams(sem):
    return pltpu.CompilerParams(dimension_semantics=sem, vmem_limit_bytes=VMEM_LIMIT)


def _dot(a, b):
    return jnp.dot(a, b, preferred_element_type=_f32)


def _rmsnorm_kernel(x_ref, g_ref, o_ref):
    x = x_ref[...]
    r = lax.rsqrt(jnp.mean(x * x, axis=-1, keepdims=True) + EPS)
    o_ref[...] = ((x * r) * g_ref[...]).astype(o_ref.dtype)


def _rmsnorm(x, g, tm=512):
    s, d = x.shape
    return pl.pallas_call(
        _rmsnorm_kernel,
        grid=(s // tm,),
        in_specs=[pl.BlockSpec((tm, d), lambda i: (i, 0)),
                  pl.BlockSpec((1, d), lambda i: (0, 0))],
        out_specs=pl.BlockSpec((tm, d), lambda i: (i, 0)),
        out_shape=jax.ShapeDtypeStruct((s, d), _bf16),
        compiler_params=_cparams(("parallel",)),
        name="rmsnorm_in",
    )(x, g.reshape(1, d))


def _head_rmsnorm(y, gain_row, groups):
    y2 = y * y
    if groups == 1:
        r = lax.rsqrt(jnp.mean(y2, axis=-1, keepdims=True) + EPS)
    else:
        lane = lax.broadcasted_iota(jnp.int32, y.shape, 1)
        lo = lane < QK_DIFF
        s_lo = jnp.sum(jnp.where(lo, y2, 0.0), axis=-1, keepdims=True)
        s_hi = jnp.sum(jnp.where(lo, 0.0, y2), axis=-1, keepdims=True)
        r = jnp.where(lo, lax.rsqrt(s_lo * (1.0 / QK_DIFF) + EPS),
                      lax.rsqrt(s_hi * (1.0 / QK_DIFF) + EPS))
    return (y * r) * gain_row


def _proj_qdiff_kernel(a_ref, w_ref, g_ref, q1_ref, q2_ref):
    acc = _dot(a_ref[...], w_ref[...])
    for hh in range(N_HEADS):
        y = acc[:, hh * HEAD_DIM:(hh + 1) * HEAD_DIM]
        yn = _head_rmsnorm(y, g_ref[...], 2) * (QK_DIFF ** -0.5 * LOG2E)
        lane = lax.broadcasted_iota(jnp.int32, yn.shape, 1)
        lo = lane < QK_DIFF
        q1_ref[hh] = jnp.where(lo, yn, 0.0).T.astype(q1_ref.dtype)
        q2_ref[hh] = jnp.where(lo, 0.0, yn).T.astype(q2_ref.dtype)


def _proj_qfox_kernel(a_ref, w_ref, g_ref, q_ref):
    acc = _dot(a_ref[...], w_ref[...])
    for hh in range(N_HEADS):
        y = acc[:, hh * HEAD_DIM:(hh + 1) * HEAD_DIM]
        yn = _head_rmsnorm(y, g_ref[...], 1) * (HEAD_DIM ** -0.5 * LOG2E)
        q_ref[hh] = yn.T.astype(q_ref.dtype)


def _proj_k_kernel(a_ref, w_ref, g_ref, k_ref, *, groups):
    acc = _dot(a_ref[...], w_ref[...])
    for hh in range(N_HEADS):
        y = acc[:, hh * HEAD_DIM:(hh + 1) * HEAD_DIM]
        k_ref[hh] = _head_rmsnorm(y, g_ref[...], groups).astype(k_ref.dtype)


def _proj_vt_kernel(a_ref, w_ref, vt_ref):
    acc = _dot(a_ref[...], w_ref[...])
    tm = acc.shape[0]
    for hh in range(N_HEADS):
        for c in range(tm // TKM):
            blk = acc[c * TKM:(c + 1) * TKM, hh * HEAD_DIM:(hh + 1) * HEAD_DIM]
            vt_ref[hh, c] = blk.T.astype(vt_ref.dtype)


def _proj_gates_kernel(a_ref, w_ref, b_ref, o_ref):
    z = _dot(a_ref[...], w_ref[...]) + b_ref[...]
    o_ref[...] = (1.0 / (1.0 + jnp.exp(-z))).astype(o_ref.dtype)


def _proj_call(kernel, h, w, extras, out_shapes, out_specs, name, tm=1024):
    s, d = h.shape
    n = w.shape[1]
    in_specs = [pl.BlockSpec((tm, d), lambda i: (i, 0)),
                pl.BlockSpec((d, n), lambda i: (0, 0))]
    in_specs += [pl.BlockSpec(e.shape, lambda i: (0, 0)) for e in extras]
    return pl.pallas_call(
        kernel,
        grid=(s // tm,),
        in_specs=in_specs,
        out_specs=out_specs,
        out_shape=out_shapes,
        compiler_params=_cparams(("parallel",)),
        name=name,
    )(h, w, *extras)


def _t_spec(tm):
    return pl.BlockSpec((N_HEADS, HEAD_DIM, tm), lambda i: (0, 0, i))


def _k_spec(tm):
    return pl.BlockSpec((N_HEADS, tm, HEAD_DIM), lambda i: (0, i, 0))


def _gates(h, w, b, tm=1024, tn=1024):
    s, d = h.shape
    n = w.shape[1]
    return pl.pallas_call(
        _proj_gates_kernel,
        grid=(s // tm, n // tn),
        in_specs=[pl.BlockSpec((tm, d), lambda i, j: (i, 0)),
                  pl.BlockSpec((d, tn), lambda i, j: (0, j)),
                  pl.BlockSpec((1, tn), lambda i, j: (0, j))],
        out_specs=pl.BlockSpec((tm, tn), lambda i, j: (i, j)),
        out_shape=jax.ShapeDtypeStruct((s, n), _bf16),
        compiler_params=_cparams(("parallel", "parallel")),
        name="proj_gates",
    )(h, w, b.reshape(1, n))


F_ROWS = 16
F_CHUNK = 256


def _forget_kernel(wt_ref, a_ref, b_ref, ft_ref, fsb_ref, carry_ref):
    i = pl.program_id(0)

    @pl.when(i == 0)
    def _():
        carry_ref[...] = jnp.zeros_like(carry_ref)

    z = lax.dot_general(wt_ref[...], a_ref[...], (((1,), (1,)), ((), ())),
                        preferred_element_type=_f32) + b_ref[...]
    logf = (jnp.minimum(z, 0.0) - jnp.log(1.0 + jnp.exp(-jnp.abs(z)))) * LOG2E
    tm = logf.shape[1]
    r = lax.broadcasted_iota(jnp.int32, (F_CHUNK, F_CHUNK), 0)
    c = lax.broadcasted_iota(jnp.int32, (F_CHUNK, F_CHUNK), 1)
    upper = jnp.where(r <= c, 1.0, 0.0).astype(_bf16)
    carry = carry_ref[...]
    for ch in range(tm // F_CHUNK):
        x = logf[:, ch * F_CHUNK:(ch + 1) * F_CHUNK]
        hi = x.astype(_bf16)
        r1 = x - hi.astype(_f32)
        mid = r1.astype(_bf16)
        lo = (r1 - mid.astype(_f32)).astype(_bf16)
        pre = _dot(hi, upper) + _dot(mid, upper) + _dot(lo, upper) + carry
        ft_ref[:, ch * F_CHUNK:(ch + 1) * F_CHUNK] = pre
        carry = pre[:, F_CHUNK - 1:F_CHUNK]
        for hh in range(N_HEADS):
            for sub in range(F_CHUNK // HEAD_DIM):
                row = pre[hh:hh + 1, sub * HEAD_DIM:(sub + 1) * HEAD_DIM]
                blk = jnp.broadcast_to(row, (HEAD_DIM, HEAD_DIM)).T
                base = ch * F_CHUNK + sub * HEAD_DIM
                fsb_ref[hh, base:base + HEAD_DIM, :] = blk
    carry_ref[...] = carry


def _forget(h, wf_t, b_col, tm=1024):
    s, d = h.shape
    return pl.pallas_call(
        _forget_kernel,
        grid=(s // tm,),
        in_specs=[pl.BlockSpec((F_ROWS, d), lambda i: (0, 0)),
                  pl.BlockSpec((tm, d), lambda i: (i, 0)),
                  pl.BlockSpec((F_ROWS, 1), lambda i: (0, 0))],
        out_specs=[pl.BlockSpec((F_ROWS, tm), lambda i: (0, i)),
                   pl.BlockSpec((N_HEADS, tm, HEAD_DIM), lambda i: (0, i, 0))],
        out_shape=[jax.ShapeDtypeStruct((F_ROWS, s), _f32),
                   jax.ShapeDtypeStruct((N_HEADS, s, HEAD_DIM), _f32)],
        scratch_shapes=[pltpu.VMEM((F_ROWS, 1), _f32)],
        compiler_params=_cparams(("arbitrary",)),
        name="forget_gate",
    )(wf_t, h, b_col)


def _block_partial(z, vt):
    mloc = jnp.max(z, axis=0, keepdims=True)
    p = jnp.exp2(z - mloc)
    return mloc, jnp.sum(p, axis=0, keepdims=True), _dot(vt, p.astype(_bf16))


def _merge_partials(m_ref, l_ref, acc_ref, idx, c, parts):
    cs = slice(c * TQS, (c + 1) * TQS)
    m_old = m_ref[idx, :, cs]
    m_new = m_old
    for mt, _, _ in parts:
        m_new = jnp.maximum(m_new, mt)
    a = jnp.exp2(m_old - m_new)
    l_new = a * l_ref[idx, :, cs]
    acc_new = a * acc_ref[idx, :, cs]
    for mt, l, o in parts:
        b = jnp.exp2(mt - m_new)
        l_new = l_new + b * l
        acc_new = acc_new + b * o
    m_ref[idx, :, cs] = m_new
    l_ref[idx, :, cs] = l_new
    acc_ref[idx, :, cs] = acc_new


def _init_state(m_ref, l_ref, acc_ref):
    m_ref[...] = jnp.full_like(m_ref, NEG_BIG)
    l_ref[...] = jnp.zeros_like(l_ref)
    acc_ref[...] = jnp.zeros_like(acc_ref)


def _local_iotas():
    lane = lax.broadcasted_iota(jnp.int32, (TK, TQS), 1)
    sub = lax.broadcasted_iota(jnp.int32, (TK, TQS), 0)
    return lane, sub


def _diff_attn_kernel(slope_ref, q1_ref, q2_ref, k_ref, vt_ref, lq1_ref, lk1_ref, lq2_ref, lk2_ref,
                      gsub_ref, o_ref, m_ref, l_ref, acc_ref):
    h = pl.program_id(0)
    qi = pl.program_id(1)
    neg_slope = -slope_ref[h] * LOG2E
    _init_state(m_ref, l_ref, acc_ref)
    qs = (q1_ref[0], q2_ref[0])

    def main_body(it, carry):
        start = pl.multiple_of(it * TKM, TKM)
        k = k_ref[0, pl.ds(start, TKM), :]
        vt = vt_ref[0, it]
        lane = lax.broadcasted_iota(jnp.int32, (TKM, TQS), 1)
        sub = lax.broadcasted_iota(jnp.int32, (TKM, TQS), 0)
        bias = (lane - sub).astype(_f32) * neg_slope
        for idx in range(2):
            s = _dot(k, qs[idx])
            for c in range(NSUB):
                mloc, l, o = _block_partial(s[:, c * TQS:(c + 1) * TQS] + bias, vt)
                shift = ((qi - it) * TQ + c * TQS).astype(_f32) * neg_slope
                _merge_partials(m_ref, l_ref, acc_ref, idx, c, [(mloc + shift, l, o)])
        return carry

    lax.fori_loop(0, qi, main_body, 0)

    lane, sub = _local_iotas()
    delta = (lane - sub).astype(_f32)
    bias_off = delta * neg_slope
    bias_diag = jnp.where((sub // 64) <= (lane // 64), jnp.abs(delta) * neg_slope, NEG_BIG)
    parts = [[[] for _ in range(NSUB)] for _ in range(2)]
    vt_band = vt_ref[0, qi]
    for j in range(NSUB):
        start = pl.multiple_of(qi * TKM + j * TK, TK)
        k = k_ref[0, pl.ds(start, TK), :]
        vt = vt_band[:, j * TK:(j + 1) * TK]
        for idx in range(2):
            s = _dot(k, qs[idx][:, j * TQS:])
            for c in range(j, NSUB):
                z = s[:, (c - j) * TQS:(c - j + 1) * TQS]
                if c == j:
                    mloc, l, o = _block_partial(z + bias_diag, vt)
                    parts[idx][c].append((mloc, l, o))
                else:
                    mloc, l, o = _block_partial(z + bias_off, vt)
                    parts[idx][c].append((mloc + ((c - j) * TQS) * neg_slope, l, o))
    for idx in range(2):
        for c in range(NSUB):
            _merge_partials(m_ref, l_ref, acc_ref, idx, c, parts[idx][c])

    lam =(jnp.exp(jnp.sum(lq1_ref[...] * lk1_ref[...], axis=-1, keepdims=True))
           - jnp.exp(jnp.sum(lq2_ref[...] * lk2_ref[...], axis=-1, keepdims=True))
           + LAMBDA_INIT)
    o = acc_ref[0] / l_ref[0] - lam * (acc_ref[1] / l_ref[1])
    r = lax.rsqrt(jnp.mean(o * o, axis=0, keepdims=True) + EPS)
    y = (o * r) * gsub_ref[...] * (1.0 - LAMBDA_INIT)
    o_ref[...] = y.T.astype(o_ref.dtype)


def _diff_attention(slopes, q1t, q2t, k, vt, lq1, lk1, lq2, lk2, gsub_col):
    nq = SEQ // TQ
    nkv = SEQ // TKM
    vec = lambda: pl.BlockSpec((1, QK_DIFF), lambda h, q: (0, 0))
    return pl.pallas_call(
        _diff_attn_kernel,
        grid=(N_HEADS, nq),
        in_specs=[pl.BlockSpec(memory_space=pltpu.SMEM),
                  pl.BlockSpec((1, HEAD_DIM, TQ), lambda h, q: (h, 0, q)),
                  pl.BlockSpec((1, HEAD_DIM, TQ), lambda h, q: (h, 0, q)),
                  pl.BlockSpec((1, SEQ, HEAD_DIM), lambda h, q: (h, 0, 0)),
                  pl.BlockSpec((1, nkv, HEAD_DIM, TKM), lambda h, q: (h, 0, 0, 0)),
                  vec(), vec(), vec(), vec(),
                  pl.BlockSpec((HEAD_DIM, 1), lambda h, q: (0, 0))],
        out_specs=pl.BlockSpec((TQ, HEAD_DIM), lambda h, q: (q, h)),
        out_shape=jax.ShapeDtypeStruct((SEQ, WIDTH), _bf16),
        scratch_shapes=[pltpu.VMEM((2, 1, TQ), _f32),
                        pltpu.VMEM((2, 1, TQ), _f32),
                        pltpu.VMEM((2, HEAD_DIM, TQ), _f32)],
        compiler_params=_cparams(("parallel", "parallel")),
        name="diff_attention",
    )(slopes, q1t, q2t, k, vt, lq1, lk1, lq2, lk2, gsub_col)


def _fox_attn_kernel(q_ref, k_ref, vt_ref, ft_ref, fsb_ref, o_ref, m_ref, l_ref, acc_ref):
    qi = pl.program_id(1)
    _init_state(m_ref, l_ref, acc_ref)
    q = q_ref[0]
    ft = ft_ref[0]

    def logits_minus_ft(start, nk, q_cols):
        k = k_ref[0, pl.ds(start, nk), :]
        fs = fsb_ref[0, pl.ds(start, nk), :]
        return _dot(k, q_cols), jnp.concatenate([fs] * (TQS // HEAD_DIM), axis=1)

    def main_body(it, carry):
        s, fs = logits_minus_ft(pl.multiple_of(it * TKM, TKM), TKM, q)
        vt = vt_ref[0, it]
        for c in range(NSUB):
            cs = slice(c * TQS, (c + 1) * TQS)
            mloc, l, o = _block_partial(s[:, cs] - fs, vt)
            _merge_partials(m_ref, l_ref, acc_ref, 0, c, [(mloc + ft[:, cs], l, o)])
        return carry

    lax.fori_loop(0, qi, main_body, 0)

    lane, sub = _local_iotas()
    causal = sub <= lane
    parts = [[] for _ in range(NSUB)]
    vt_band = vt_ref[0, qi]
    for j in range(NSUB):
        s, fs = logits_minus_ft(pl.multiple_of(qi * TKM + j * TK, TK), TK, q[:, j * TQS:])
        vt = vt_band[:, j * TK:(j + 1) * TK]
        for c in range(j, NSUB):
            z = s[:, (c - j) * TQS:(c - j + 1) * TQS] - fs
            if c == j:
                z = jnp.where(causal, z, NEG_BIG)
            mloc, l, o = _block_partial(z, vt)
            parts[c].append((mloc + ft[:, c * TQS:(c + 1) * TQS], l, o))
    for c in range(NSUB):
        _merge_partials(m_ref, l_ref, acc_ref, 0, c, parts[c])

    o = acc_ref[0] / l_ref[0]
    o_ref[...] = o.T.astype(o_ref.dtype)


def _fox_attention(qt, k, vt, ft3, fsb):
    nq = SEQ // TQ
    nkv = SEQ // TKM
    return pl.pallas_call(
        _fox_attn_kernel,
        grid=(N_HEADS, nq),
        in_specs=[pl.BlockSpec((1, HEAD_DIM, TQ), lambda h, q: (h, 0, q)),
                  pl.BlockSpec((1, SEQ, HEAD_DIM), lambda h, q: (h, 0, 0)),
                  pl.BlockSpec((1, nkv, HEAD_DIM, TKM), lambda h, q: (h, 0, 0, 0)),
                  pl.BlockSpec((1, 1, TQ), lambda h, q: (h, 0, q)),
                  pl.BlockSpec((1, SEQ, HEAD_DIM), lambda h, q: (h, 0, 0))],
        out_specs=pl.BlockSpec((TQ, HEAD_DIM), lambda h, q: (q, h)),
        out_shape=jax.ShapeDtypeStruct((SEQ, WIDTH), _bf16),
        scratch_shapes=[pltpu.VMEM((1, 1, TQ), _f32),
                        pltpu.VMEM((1, 1, TQ), _f32),
                        pltpu.VMEM((1, HEAD_DIM, TQ), _f32)],
        compiler_params=_cparams(("parallel", "parallel")),
        name="fox_attention",
    )(qt, k, vt, ft3, fsb)


def _merge_kernel(oa_ref, ob_ref, g_ref, x_ref, wbd_ref, wbf_ref, wout_ref, gm_ref, x1_ref, h2_ref):
    a = _dot(oa_ref[...], wbd_ref[...])
    b = _dot(ob_ref[...], wbf_ref[...])
    g = g_ref[...].astype(_f32)
    merged = g[:, :D_MODEL] * a + g[:, D_MODEL:] * b
    x1 = x_ref[...] + _dot(merged.astype(_bf16), wout_ref[...])
    x1_ref[...] = x1
    r = lax.rsqrt(jnp.mean(x1 * x1, axis=-1, keepdims=True) + EPS)
    h2_ref[...] = ((x1 * r) * gm_ref[...]).astype(h2_ref.dtype)


def _merge(oa, ob, gates, x, wbd, wbf, wout, gm, tm=256):
    s, d = x.shape
    const = lambda shape: pl.BlockSpec(shape, lambda i: (0, 0), pipeline_mode=pl.Buffered(1))
    return pl.pallas_call(
        _merge_kernel,
        grid=(s // tm,),
        in_specs=[pl.BlockSpec((tm, WIDTH), lambda i: (i, 0)),
                  pl.BlockSpec((tm, WIDTH), lambda i: (i, 0)),
                  pl.BlockSpec((tm, 2 * d), lambda i: (i, 0)),
                  pl.BlockSpec((tm, d), lambda i: (i, 0)),
                  const((WIDTH, d)), const((WIDTH, d)), const((d, d)),
                  pl.BlockSpec((1, d), lambda i: (0, 0))],
        out_specs=[pl.BlockSpec((tm, d), lambda i: (i, 0)),
                   pl.BlockSpec((tm, d), lambda i: (i, 0))],
        out_shape=[jax.ShapeDtypeStruct((s, d), _f32),
                   jax.ShapeDtypeStruct((s, d), _bf16)],
        compiler_params=_cparams(("parallel",)),
        name="merge_out_proj",
    )(oa, ob, gates, x, wbd, wbf, wout, gm.reshape(1, d))


def _mlp_kernel(h_ref, wu_ref, wd_ref, x_ref, o_ref):
    k = pl.program_id(1)

    @pl.when(k == 0)
    def _():
        o_ref[...] = x_ref[...]

    u = jnp.maximum(_dot(h_ref[...], wu_ref[...]), 0.0)
    o_ref[...] += _dot((u * u).astype(_bf16), wd_ref[...])


def _mlp(h2, wu, wd, x1, tm=512, tf=1024):
    s, d = x1.shape
    f = wu.shape[1]
    return pl.pallas_call(
        _mlp_kernel,
        grid=(s // tm, f // tf),
        in_specs=[pl.BlockSpec((tm, d), lambda i, k: (i, 0)),
                  pl.BlockSpec((d, tf), lambda i, k: (0, k)),
                  pl.BlockSpec((tf, d), lambda i, k: (k, 0)),
                  pl.BlockSpec((tm, d), lambda i, k: (i, 0))],
        out_specs=pl.BlockSpec((tm, d), lambda i, k: (i, 0)),
        out_shape=jax.ShapeDtypeStruct((s, d), _f32),
        compiler_params=_cparams(("parallel", "arbitrary")),
        name="mlp_relu2",
    )(h2, wu, wd, x1)


def kernel(x, norm_mix, w_in, b_forget, qnorm_diff, knorm_diff, lambda_q1, lambda_k1, lambda_q2, lambda_k2,
           subln_diff, qnorm_fox, knorm_fox, w_branch_diff, w_branch_fox, w_gate, b_gate, w_out, norm_mlp,
           w_mlp_up, w_mlp_down):
    assert x.shape == (1, SEQ, D_MODEL)
    x2 = x[0]
    w_in0 = w_in[0]
    region = lambda r: w_in0[:, r * WIDTH:(r + 1) * WIDTH].astype(_bf16)

    h = _rmsnorm(x2, norm_mix[0])

    g_qd = jnp.tile(qnorm_diff[0], 2).reshape(1, HEAD_DIM)
    g_kd = jnp.tile(knorm_diff[0], 2).reshape(1, HEAD_DIM)
    tm = 1024
    t_shape = jax.ShapeDtypeStruct((N_HEADS, HEAD_DIM, SEQ), _bf16)
    k_shape = jax.ShapeDtypeStruct((N_HEADS, SEQ, HEAD_DIM), _bf16)
    vt_shape = jax.ShapeDtypeStruct((N_HEADS, SEQ // TKM, HEAD_DIM, TKM), _bf16)
    vt_spec = pl.BlockSpec((N_HEADS, tm // TKM, HEAD_DIM, TKM), lambda i: (0, i, 0, 0))

    q1t, q2t = _proj_call(_proj_qdiff_kernel, h, region(0), [g_qd], [t_shape, t_shape],
                          [_t_spec(tm), _t_spec(tm)], "proj_q_diff")
    ka = _proj_call(functools.partial(_proj_k_kernel, groups=2), h, region(1), [g_kd], k_shape,
                    _k_spec(tm), "proj_k_diff")
    vat = _proj_call(_proj_vt_kernel, h, region(2), [], vt_shape, vt_spec, "proj_v_diff")
    qbt = _proj_call(_proj_qfox_kernel, h, region(3), [qnorm_fox[0].reshape(1, HEAD_DIM)], t_shape,
                     _t_spec(tm), "proj_q_fox")
    kb = _proj_call(functools.partial(_proj_k_kernel, groups=1), h, region(4),
                    [knorm_fox[0].reshape(1, HEAD_DIM)], k_shape, _k_spec(tm), "proj_k_fox")
    vbt = _proj_call(_proj_vt_kernel, h, region(5), [], vt_shape, vt_spec, "proj_v_fox")

    wf_t = jnp.zeros((F_ROWS, D_MODEL), _bf16).at[:N_HEADS].set(w_in0[:, 6 * WIDTH:].T.astype(_bf16))
    bf_col = jnp.zeros((F_ROWS, 1), _f32).at[:N_HEADS, 0].set(b_forget[0])
    ft, fsb = _forget(h, wf_t, bf_col)
    ft3 = ft.reshape(F_ROWS, 1, SEQ)

    gates = _gates(h, w_gate[0].astype(_bf16), b_gate[0])

    slopes = 2.0 ** (-8.0 * jnp.arange(1, N_HEADS + 1, dtype=_f32) / N_HEADS)
    row = lambda v: v[0].reshape(1, QK_DIFF)
    oa = _diff_attention(slopes, q1t, q2t, ka, vat, row(lambda_q1), row(lambda_k1), row(lambda_q2),
                         row(lambda_k2), subln_diff[0].reshape(HEAD_DIM, 1))
    ob = _fox_attention(qbt, kb, vbt, ft3, fsb)

    x1, h2 = _merge(oa, ob, gates, x2, w_branch_diff[0].astype(_bf16), w_branch_fox[0].astype(_bf16),
                    w_out[0].astype(_bf16), norm_mlp[0])
    out = _mlp(h2, w_mlp_up[0].astype(_bf16), w_mlp_down[0].astype(_bf16), x1)
    return out[None]
```

```python
import functools

import jax
import jax.numpy as jnp
from jax import lax
from jax.experimental import pallas as pl
from jax.experimental.pallas import tpu as pltpu

D_MODEL = 2048
SEQ = 8192
HEAD_DIM = 128
N_HEADS = 8
QK_DIFF = 64
WIDTH = N_HEADS * HEAD_DIM
D_FF = 4 * D_MODEL
EPS = 1e-6
LAMBDA_INIT = 0.8 - 0.6 * 1.0
NEG_BIG = -1e30

TQ = 1024
TQS = 256
TK = 256
TKM = TQ
NSUB = TQ // TQS
LOG2E = 1.4426950408889634
V_ROWS = HEAD_DIM + 16
VMEM_LIMIT = 56 * 1024 * 1024

_f32 = jnp.float32
_bf16 = jnp.bfloat16


def _cparams(sem):
    return pltpu.CompilerParams(dimension_semantics=sem, vmem_limit_bytes=VMEM_LIMIT)


def _dot(a, b):
    return jnp.dot(a, b, preferred_element_type=_f32)


def _rmsnorm_kernel(x_ref, g_ref, o_ref):
    x = x_ref[...]
    r = lax.rsqrt(jnp.mean(x * x, axis=-1, keepdims=True) + EPS)
    o_ref[...] = ((x * r) * g_ref[...]).astype(o_ref.dtype)


def _rmsnorm(x, g, tm=512):
    s, d = x.shape
    return pl.pallas_call(
        _rmsnorm_kernel,
        grid=(s // tm,),
        in_specs=[pl.BlockSpec((tm, d), lambda i: (i, 0)),
                  pl.BlockSpec((1, d), lambda i: (0, 0))],
        out_specs=pl.BlockSpec((tm, d), lambda i: (i, 0)),
        out_shape=jax.ShapeDtypeStruct((s, d), _bf16),
        compiler_params=_cparams(("parallel",)),
        name="rmsnorm_in",
    )(x, g.reshape(1, d))


def _head_rmsnorm(y, gain_row, groups):
    y2 = y * y
    if groups == 1:
        r = lax.rsqrt(jnp.mean(y2, axis=-1, keepdims=True) + EPS)
    else:
        lane = lax.broadcasted_iota(jnp.int32, y.shape, 1)
        lo = lane < QK_DIFF
        s_lo = jnp.sum(jnp.where(lo, y2, 0.0), axis=-1, keepdims=True)
        s_hi = jnp.sum(jnp.where(lo, 0.0, y2), axis=-1, keepdims=True)
        r = jnp.where(lo, lax.rsqrt(s_lo * (1.0 / QK_DIFF) + EPS),
                      lax.rsqrt(s_hi * (1.0 / QK_DIFF) + EPS))
    return (y * r) * gain_row


def _proj_qdiff_kernel(a_ref, w_ref, g_ref, q1_ref, q2_ref):
    acc = _dot(a_ref[...], w_ref[...])
    for hh in range(N_HEADS):
        y = acc[:, hh * HEAD_DIM:(hh + 1) * HEAD_DIM]
        yn = _head_rmsnorm(y, g_ref[...], 2) * (QK_DIFF ** -0.5 * LOG2E)
        lane = lax.broadcasted_iota(jnp.int32, yn.shape, 1)
        lo = lane < QK_DIFF
        q1_ref[hh] = jnp.where(lo, yn, 0.0).T.astype(q1_ref.dtype)
        q2_ref[hh] = jnp.where(lo, 0.0, yn).T.astype(q2_ref.dtype)


def _proj_qfox_kernel(a_ref, w_ref, g_ref, q_ref):
    acc = _dot(a_ref[...], w_ref[...])
    for hh in range(N_HEADS):
        y = acc[:, hh * HEAD_DIM:(hh + 1) * HEAD_DIM]
        yn = _head_rmsnorm(y, g_ref[...], 1) * (HEAD_DIM ** -0.5 * LOG2E)
        q_ref[hh] = yn.T.astype(q_ref.dtype)


def _proj_k_kernel(a_ref, w_ref, g_ref, k_ref, *, groups):
    acc = _dot(a_ref[...], w_ref[...])
    for hh in range(N_HEADS):
        y = acc[:, hh * HEAD_DIM:(hh + 1) * HEAD_DIM]
        k_ref[hh] = _head_rmsnorm(y, g_ref[...], groups).astype(k_ref.dtype)


def _proj_vt_kernel(a_ref, w_ref, vt_ref):
    acc = _dot(a_ref[...], w_ref[...])
    tm = acc.shape[0]
    row = lax.broadcasted_iota(jnp.int32, (V_ROWS - HEAD_DIM, TKM), 0)
    ones_rows = jnp.where(row == 0, 1.0, 0.0).astype(vt_ref.dtype)
    for hh in range(N_HEADS):
        for c in range(tm // TKM):
            blk = acc[c * TKM:(c + 1) * TKM, hh * HEAD_DIM:(hh + 1) * HEAD_DIM]
            vt_ref[hh, c, :HEAD_DIM, :] = blk.T.astype(vt_ref.dtype)
            vt_ref[hh, c, HEAD_DIM:, :] = ones_rows


def _proj_gates_kernel(a_ref, w_ref, b_ref, o_ref):
    z = _dot(a_ref[...], w_ref[...]) + b_ref[...]
    o_ref[...] = (1.0 / (1.0 + jnp.exp(-z))).astype(o_ref.dtype)


def _proj_call(kernel, h, w, extras, out_shapes, out_specs, name, tm=1024):
    s, d = h.shape
    n = w.shape[1]
    in_specs = [pl.BlockSpec((tm, d), lambda i: (i, 0)),
                pl.BlockSpec((d, n), lambda i: (0, 0))]
    in_specs += [pl.BlockSpec(e.shape, lambda i: (0, 0)) for e in extras]
    return pl.pallas_call(
        kernel,
        grid=(s // tm,),
        in_specs=in_specs,
        out_specs=out_specs,
        out_shape=out_shapes,
        compiler_params=_cparams(("parallel",)),
        name=name,
    )(h, w, *extras)


def _t_spec(tm):
    return pl.BlockSpec((N_HEADS, HEAD_DIM, tm), lambda i: (0, 0, i))


def _k_spec(tm):
    return pl.BlockSpec((N_HEADS, tm, HEAD_DIM), lambda i: (0, i, 0))


def _gates(h, w, b, tm=1024, tn=1024):
    s, d = h.shape
    n = w.shape[1]
    return pl.pallas_call(
        _proj_gates_kernel,
        grid=(s // tm, n // tn),
        in_specs=[pl.BlockSpec((tm, d), lambda i, j: (i, 0)),
                  pl.BlockSpec((d, tn), lambda i, j: (0, j)),
                  pl.BlockSpec((1, tn), lambda i, j: (0, j))],
        out_specs=pl.BlockSpec((tm, tn), lambda i, j: (i, j)),
        out_shape=jax.ShapeDtypeStruct((s, n), _bf16),
        compiler_params=_cparams(("parallel", "parallel")),
        name="proj_gates",
    )(h, w, b.reshape(1, n))


F_ROWS = 16
F_CHUNK = 256


def _split3(x):
    hi = x.astype(_bf16)
    r1 = x - hi.astype(_f32)
    mid = r1.astype(_bf16)
    lo = (r1 - mid.astype(_f32)).astype(_bf16)
    return hi, mid, lo


def _select_rows_or_lanes(index, values):
    out = jnp.zeros(index.shape, _f32)
    for i in reversed(range(len(values))):
        out = jnp.where(index == i, values[i].astype(_f32), out)
    return out


def _forget_kernel(wt_ref, a_ref, b_ref, ft_ref, kx_ref, carry_ref):
    i = pl.program_id(0)

    @pl.when(i == 0)
    def _():
        carry_ref[...] = jnp.zeros_like(carry_ref)

    z = lax.dot_general(wt_ref[...], a_ref[...], (((1,), (1,)), ((), ())),
                        preferred_element_type=_f32) + b_ref[...]
    logf = (jnp.minimum(z, 0.0) - jnp.log(1.0 + jnp.exp(-jnp.abs(z)))) * LOG2E
    tm = logf.shape[1]
    r = lax.broadcasted_iota(jnp.int32, (F_CHUNK, F_CHUNK), 0)
    c = lax.broadcasted_iota(jnp.int32, (F_CHUNK, F_CHUNK), 1)
    upper = jnp.where(r <= c, 1.0, 0.0).astype(_bf16)
    carry = carry_ref[...]
    lane = lax.broadcasted_iota(jnp.int32, (HEAD_DIM, HEAD_DIM), 1)
    for ch in range(tm // F_CHUNK):
        x = logf[:, ch * F_CHUNK:(ch + 1) * F_CHUNK]
        hi, mid, lo = _split3(x)
        pre = _dot(hi, upper) + _dot(mid, upper) + _dot(lo, upper) + carry
        ft_ref[:, ch * F_CHUNK:(ch + 1) * F_CHUNK] = pre
        carry = pre[:, F_CHUNK - 1:F_CHUNK]
        for hh in range(N_HEADS):
            for sub in range(F_CHUNK // HEAD_DIM):
                row = pre[hh:hh + 1, sub * HEAD_DIM:(sub + 1) * HEAD_DIM]
                neg = -jnp.broadcast_to(row, (HEAD_DIM, HEAD_DIM)).T
                base = ch * F_CHUNK + sub * HEAD_DIM
                kx_ref[hh, base:base + HEAD_DIM, :] = _select_rows_or_lanes(
                    lane, _split3(neg)).astype(kx_ref.dtype)
    carry_ref[...] = carry


def _forget(h, wf_t, b_col, tm=1024):
    s, d = h.shape
    return pl.pallas_call(
        _forget_kernel,
        grid=(s // tm,),
        in_specs=[pl.BlockSpec((F_ROWS, d), lambda i: (0, 0)),
                  pl.BlockSpec((tm, d), lambda i: (i, 0)),
                  pl.BlockSpec((F_ROWS, 1), lambda i: (0, 0))],
        out_specs=[pl.BlockSpec((F_ROWS, tm), lambda i: (0, i)),
                   pl.BlockSpec((N_HEADS, tm, HEAD_DIM), lambda i: (0, i, 0))],
        out_shape=[jax.ShapeDtypeStruct((F_ROWS, s), _f32),
                   jax.ShapeDtypeStruct((N_HEADS, s, HEAD_DIM), _bf16)],
        scratch_shapes=[pltpu.VMEM((F_ROWS, 1), _f32)],
        compiler_params=_cparams(("arbitrary",)),
        name="forget_gate",
    )(wf_t, h, b_col)


def _block_partial(z, vt):
    mloc = jnp.max(z, axis=0, keepdims=True)
    o = _dot(vt, jnp.exp2(z - mloc).astype(_bf16))
    return mloc, o[HEAD_DIM:HEAD_DIM + 1], o[:HEAD_DIM]


def _merge_partials(m_ref, l_ref, acc_ref, idx, c, parts):
    cs = slice(c * TQS, (c + 1) * TQS)
    m_old = m_ref[idx, :, cs]
    m_new = m_old
    for mt, _, _ in parts:
        m_new = jnp.maximum(m_new, mt)
    a = jnp.exp2(m_old - m_new)
    l_new = a * l_ref[idx, :, cs]
    acc_new = a * acc_ref[idx, :, cs]
    for mt, l, o in parts:
        b = jnp.exp2(mt - m_new)
        l_new = l_new + b * l
        acc_new = acc_new + b * o
    m_ref[idx, :, cs] = m_new
    l_ref[idx, :, cs] = l_new
    acc_ref[idx, :, cs] = acc_new


def _init_state(m_ref, l_ref, acc_ref):
    m_ref[...] = jnp.full_like(m_ref, NEG_BIG)
    l_ref[...] = jnp.zeros_like(l_ref)
    acc_ref[...] = jnp.zeros_like(acc_ref)


def _pipelined_key_sweep(qi, scores, main_block, band_block, sa_ref, sb_ref):
    def fill(refs, block):
        for ref, val in zip(refs, scores(block)):
            ref[...] = val

    fill(sa_ref, 0)

    def pair(t, carry):
        fill(sb_ref, 2 * t + 1)
        main_block(2 * t, sa_ref)
        fill(sa_ref, 2 * t + 2)
        main_block(2 * t + 1, sb_ref)
        return carry

    lax.fori_loop(0, lax.shift_right_logical(qi, 1), pair, 0)
    odd = lax.rem(qi, 2) == 1

    @pl.when(odd)
    def _():
        fill(sb_ref, qi)
        main_block(qi - 1, sa_ref)
        band_block(sb_ref)

    @pl.when(jnp.logical_not(odd))
    def _():
        band_block(sa_ref)


def _local_iotas():
    lane = lax.broadcasted_iota(jnp.int32, (TK, TQS), 1)
    sub = lax.broadcasted_iota(jnp.int32, (TK, TQS), 0)
    return lane, sub


def _diff_attn_kernel(slope_ref, q1_ref, q2_ref, k_ref, vt_ref, lq1_ref, lk1_ref, lq2_ref, lk2_ref,
                      gsub_ref, o_ref, m_ref, l_ref, acc_ref, sa0_ref, sa1_ref, sb0_ref, sb1_ref):
    h = pl.program_id(0)
    qi = pl.program_id(1)
    neg_slope = -slope_ref[h] * LOG2E
    _init_state(m_ref, l_ref, acc_ref)
    qs = (q1_ref[0], q2_ref[0])

    slope2 = -neg_slope
    krow = lax.broadcasted_iota(jnp.int32, (TKM, HEAD_DIM), 0)
    klane = lax.broadcasted_iota(jnp.int32, (TKM, HEAD_DIM), 1)
    kx = jnp.where(klane < 3, (krow // HEAD_DIM).astype(_f32),
                   jnp.where(klane < 6, (krow % HEAD_DIM).astype(_f32), 0.0)).astype(_bf16)
    qrow = lax.broadcasted_iota(jnp.int32, (HEAD_DIM, TQ), 0)
    ones = jnp.ones((HEAD_DIM, TQ), _f32)
    qx = _select_rows_or_lanes(qrow, _split3(ones * (slope2 * HEAD_DIM)) + _split3(ones * slope2)).astype(_bf16)
    qs_aug = [jnp.concatenate([q, qx], axis=0) for q in qs]
    t_loc = lax.broadcasted_iota(jnp.int32, (1, TQS), 1).astype(_f32)

    def scores(block):
        start = pl.multiple_of(block * TKM, TKM)
        k_aug = jnp.concatenate([k_ref[0, pl.ds(start, TKM), :], kx], axis=1)
        return tuple(_dot(k_aug, q) for q in qs_aug)

    def query_shift(block, c):
        return (((qi - block) * TQ + c * TQS).astype(_f32) + t_loc) * neg_slope

    def main_block(block, s_refs):
        vt = vt_ref[0, block]
        for idx in range(2):
            for c in range(NSUB):
                mloc, l, o = _block_partial(s_refs[idx][:, c * TQS:(c + 1) * TQS], vt)
                _merge_partials(m_ref, l_ref, acc_ref, idx, c, [(mloc + query_shift(block, c), l, o)])

    def band_block(s_refs):
        lane, sub = _local_iotas()
        delta = (lane - sub).astype(_f32)
        fix_diag = jnp.where((sub // 64) <= (lane // 64),
                             (sub.astype(_f32) + jnp.abs(delta)) * neg_slope, NEG_BIG)
        parts = [[[] for _ in range(NSUB)] for _ in range(2)]
        vt_band = vt_ref[0, qi]
        for j in range(NSUB):
            vt = vt_band[:, j * TK:(j + 1) * TK]
            for idx in range(2):
                for c in range(j, NSUB):
                    z = s_refs[idx][j * TK:(j + 1) * TK, c * TQS:(c + 1) * TQS]
                    if c == j:
                        mloc, l, o = _block_partial(z + fix_diag, vt)
                        parts[idx][c].append((mloc + (j * TK) * neg_slope, l, o))
                    else:
                        mloc, l, o = _block_partial(z, vt)
                        parts[idx][c].append((mloc + query_shift(qi, c), l, o))
        for idx in range(2):
            for c in range(NSUB):
                _merge_partials(m_ref, l_ref, acc_ref, idx, c, parts[idx][c])

    _pipelined_key_sweep(qi, scores, main_block, band_block, (sa0_ref, sa1_ref), (sb0_ref, sb1_ref))

    lam =(jnp.exp(jnp.sum(lq1_ref[...] * lk1_ref[...], axis=-1, keepdims=True))
           - jnp.exp(jnp.sum(lq2_ref[...] * lk2_ref[...], axis=-1, keepdims=True))
           + LAMBDA_INIT)
    o = acc_ref[0] / l_ref[0] - lam * (acc_ref[1] / l_ref[1])
    r = lax.rsqrt(jnp.mean(o * o, axis=0, keepdims=True) + EPS)
    y = (o * r) * gsub_ref[...] * (1.0 - LAMBDA_INIT)
    o_ref[...] = y.T.astype(o_ref.dtype)


def _diff_attention(slopes, q1t, q2t, k, vt, lq1, lk1, lq2, lk2, gsub_col):
    nq = SEQ // TQ
    nkv = SEQ // TKM
    vec = lambda: pl.BlockSpec((1, QK_DIFF), lambda h, q: (0, 0))
    return pl.pallas_call(
        _diff_attn_kernel,
        grid=(N_HEADS, nq),
        in_specs=[pl.BlockSpec(memory_space=pltpu.SMEM),
                  pl.BlockSpec((1, HEAD_DIM, TQ), lambda h, q: (h, 0, q)),
                  pl.BlockSpec((1, HEAD_DIM, TQ), lambda h, q: (h, 0, q)),
                  pl.BlockSpec((1, SEQ, HEAD_DIM), lambda h, q: (h, 0, 0)),
                  pl.BlockSpec((1, nkv, V_ROWS, TKM), lambda h, q: (h, 0, 0, 0)),
                  vec(), vec(), vec(), vec(),
                  pl.BlockSpec((HEAD_DIM, 1), lambda h, q: (0, 0))],
        out_specs=pl.BlockSpec((TQ, HEAD_DIM), lambda h, q: (q, h)),
        out_shape=jax.ShapeDtypeStruct((SEQ, WIDTH), _bf16),
        scratch_shapes=[pltpu.VMEM((2, 1, TQ), _f32),
                        pltpu.VMEM((2, 1, TQ), _f32),
                        pltpu.VMEM((2, HEAD_DIM, TQ), _f32)] + [pltpu.VMEM((TKM, TQ), _f32)] * 4,
        compiler_params=_cparams(("parallel", "parallel")),
        name="diff_attention",
    )(slopes, q1t, q2t, k, vt, lq1, lk1, lq2, lk2, gsub_col)


def _fox_attn_kernel(q_ref, k_ref, vt_ref, ft_ref, kx_ref, o_ref, m_ref, l_ref, acc_ref, sa_ref, sb_ref):
    qi = pl.program_id(1)
    _init_state(m_ref, l_ref, acc_ref)
    q = q_ref[0]
    ft = ft_ref[0]
    qrow = lax.broadcasted_iota(jnp.int32, (HEAD_DIM, TQ), 0)
    q_aug = jnp.concatenate([q, jnp.where(qrow < 3, 1.0, 0.0).astype(_bf16)], axis=0)

    def scores(block):
        start = pl.multiple_of(block * TKM, TKM)
        k_aug = jnp.concatenate([k_ref[0, pl.ds(start, TKM), :], kx_ref[0, pl.ds(start, TKM), :]], axis=1)
        return (_dot(k_aug, q_aug),)

    def main_block(block, s_ref):
        vt = vt_ref[0, block]
        for c in range(NSUB):
            cs = slice(c * TQS, (c + 1) * TQS)
            mloc, l, o = _block_partial(s_ref[0][:, cs], vt)
            _merge_partials(m_ref, l_ref, acc_ref, 0, c, [(mloc + ft[:, cs], l, o)])

    def band_block(s_ref):
        lane, sub = _local_iotas()
        causal = sub <= lane
        parts = [[] for _ in range(NSUB)]
        vt_band = vt_ref[0, qi]
        for j in range(NSUB):
            vt = vt_band[:, j * TK:(j + 1) * TK]
            for c in range(j, NSUB):
                z = s_ref[0][j * TK:(j + 1) * TK, c * TQS:(c + 1) * TQS]
                if c == j:
                    z = jnp.where(causal, z, NEG_BIG)
                mloc, l, o = _block_partial(z, vt)
                parts[c].append((mloc + ft[:, c * TQS:(c + 1) * TQS], l, o))
        for c in range(NSUB):
            _merge_partials(m_ref, l_ref, acc_ref, 0, c, parts[c])

    _pipelined_key_sweep(qi, scores, main_block, band_block, (sa_ref,), (sb_ref,))

    o = acc_ref[0] / l_ref[0]
    o_ref[...] = o.T.astype(o_ref.dtype)


def _fox_attention(qt, k, vt, ft3, kx):
    nq = SEQ // TQ
    nkv = SEQ // TKM
    return pl.pallas_call(
        _fox_attn_kernel,
        grid=(N_HEADS, nq),
        in_specs=[pl.BlockSpec((1, HEAD_DIM, TQ), lambda h, q: (h, 0, q)),
                  pl.BlockSpec((1, SEQ, HEAD_DIM), lambda h, q: (h, 0, 0)),
                  pl.BlockSpec((1, nkv, V_ROWS, TKM), lambda h, q: (h, 0, 0, 0)),
                  pl.BlockSpec((1, 1, TQ), lambda h, q: (h, 0, q)),
                  pl.BlockSpec((1, SEQ, HEAD_DIM), lambda h, q: (h, 0, 0))],
        out_specs=pl.BlockSpec((TQ, HEAD_DIM), lambda h, q: (q, h)),
        out_shape=jax.ShapeDtypeStruct((SEQ, WIDTH), _bf16),
        scratch_shapes=[pltpu.VMEM((1, 1, TQ), _f32),
                        pltpu.VMEM((1, 1, TQ), _f32),
                        pltpu.VMEM((1, HEAD_DIM, TQ), _f32),
                        pltpu.VMEM((TKM, TQ), _f32),
                        pltpu.VMEM((TKM, TQ), _f32)],
        compiler_params=_cparams(("parallel", "parallel")),
        name="fox_attention",
    )(qt, k, vt, ft3, kx)


def _merge_kernel(oa_ref, ob_ref, g_ref, x_ref, wbd_ref, wbf_ref, wout_ref, gm_ref, x1_ref, h2_ref):
    a = _dot(oa_ref[...], wbd_ref[...])
    b = _dot(ob_ref[...], wbf_ref[...])
    g = g_ref[...].astype(_f32)
    merged = g[:, :D_MODEL] * a + g[:, D_MODEL:] * b
    x1 = x_ref[...] + _dot(merged.astype(_bf16), wout_ref[...])
    x1_ref[...] = x1
    r = lax.rsqrt(jnp.mean(x1 * x1, axis=-1, keepdims=True) + EPS)
    h2_ref[...] = ((x1 * r) * gm_ref[...]).astype(h2_ref.dtype)


def _merge(oa, ob, gates, x, wbd, wbf, wout, gm, tm=256):
    s, d = x.shape
    const = lambda shape: pl.BlockSpec(shape, lambda i: (0, 0), pipeline_mode=pl.Buffered(1))
    return pl.pallas_call(
        _merge_kernel,
        grid=(s // tm,),
        in_specs=[pl.BlockSpec((tm, WIDTH), lambda i: (i, 0)),
                  pl.BlockSpec((tm, WIDTH), lambda i: (i, 0)),
                  pl.BlockSpec((tm, 2 * d), lambda i: (i, 0)),
                  pl.BlockSpec((tm, d), lambda i: (i, 0)),
                  const((WIDTH, d)), const((WIDTH, d)), const((d, d)),
                  pl.BlockSpec((1, d), lambda i: (0, 0))],
        out_specs=[pl.BlockSpec((tm, d), lambda i: (i, 0)),
                   pl.BlockSpec((tm, d), lambda i: (i, 0))],
        out_shape=[jax.ShapeDtypeStruct((s, d), _f32),
                   jax.ShapeDtypeStruct((s, d), _bf16)],
        compiler_params=_cparams(("parallel",)),
        name="merge_out_proj",
    )(oa, ob, gates, x, wbd, wbf, wout, gm.reshape(1, d))


def _mlp_kernel(h_ref, wu_ref, wd_ref, x_ref, o_ref):
    k = pl.program_id(1)

    @pl.when(k == 0)
    def _():
        o_ref[...] = x_ref[...]

    u = jnp.maximum(_dot(h_ref[...], wu_ref[...]), 0.0)
    o_ref[...] += _dot((u * u).astype(_bf16), wd_ref[...])


def _mlp(h2, wu, wd, x1, tm=512, tf=1024):
    s, d = x1.shape
    f = wu.shape[1]
    return pl.pallas_call(
        _mlp_kernel,
        grid=(s // tm, f // tf),
        in_specs=[pl.BlockSpec((tm, d), lambda i, k: (i, 0)),
                  pl.BlockSpec((d, tf), lambda i, k: (0, k)),
                  pl.BlockSpec((tf, d), lambda i, k: (k, 0)),
                  pl.BlockSpec((tm, d), lambda i, k: (i, 0))],
        out_specs=pl.BlockSpec((tm, d), lambda i, k: (i, 0)),
        out_shape=jax.ShapeDtypeStruct((s, d), _f32),
        compiler_params=_cparams(("parallel", "arbitrary")),
        name="mlp_relu2",
    )(h2, wu, wd, x1)


def kernel(x, norm_mix, w_in, b_forget, qnorm_diff, knorm_diff, lambda_q1, lambda_k1, lambda_q2, lambda_k2,
           subln_diff, qnorm_fox, knorm_fox, w_branch_diff, w_branch_fox, w_gate, b_gate, w_out, norm_mlp,
           w_mlp_up, w_mlp_down):
    assert x.shape == (1, SEQ, D_MODEL)
    x2 = x[0]
    w_in0 = w_in[0]
    region = lambda r: w_in0[:, r * WIDTH:(r + 1) * WIDTH].astype(_bf16)

    h = _rmsnorm(x2, norm_mix[0])

    g_qd = jnp.tile(qnorm_diff[0], 2).reshape(1, HEAD_DIM)
    g_kd = jnp.tile(knorm_diff[0], 2).reshape(1, HEAD_DIM)
    tm = 1024
    t_shape = jax.ShapeDtypeStruct((N_HEADS, HEAD_DIM, SEQ), _bf16)
    k_shape = jax.ShapeDtypeStruct((N_HEADS, SEQ, HEAD_DIM), _bf16)
    vt_shape = jax.ShapeDtypeStruct((N_HEADS, SEQ // TKM, V_ROWS, TKM), _bf16)
    vt_spec = pl.BlockSpec((N_HEADS, tm // TKM, V_ROWS, TKM), lambda i: (0, i, 0, 0))

    q1t, q2t = _proj_call(_proj_qdiff_kernel, h, region(0), [g_qd], [t_shape, t_shape],
                          [_t_spec(tm), _t_spec(tm)], "proj_q_diff")
    ka = _proj_call(functools.partial(_proj_k_kernel, groups=2), h, region(1), [g_kd], k_shape,
                    _k_spec(tm), "proj_k_diff")
    vat = _proj_call(_proj_vt_kernel, h, region(2), [], vt_shape, vt_spec, "proj_v_diff")
    qbt = _proj_call(_proj_qfox_kernel, h, region(3), [qnorm_fox[0].reshape(1, HEAD_DIM)], t_shape,
                     _t_spec(tm), "proj_q_fox")
    kb = _proj_call(functools.partial(_proj_k_kernel, groups=1), h, region(4),
                    [knorm_fox[0].reshape(1, HEAD_DIM)], k_shape, _k_spec(tm), "proj_k_fox")
    vbt = _proj_call(_proj_vt_kernel, h, region(5), [], vt_shape, vt_spec, "proj_v_fox")

    wf_t = jnp.zeros((F_ROWS, D_MODEL), _bf16).at[:N_HEADS].set(w_in0[:, 6 * WIDTH:].T.astype(_bf16))
    bf_col = jnp.zeros((F_ROWS, 1), _f32).at[:N_HEADS, 0].set(b_forget[0])
    ft, kx_fox = _forget(h, wf_t, bf_col)
    ft3 = ft.reshape(F_ROWS, 1, SEQ)

    gates = _gates(h, w_gate[0].astype(_bf16), b_gate[0])

    slopes = 2.0 ** (-8.0 * jnp.arange(1, N_HEADS + 1, dtype=_f32) / N_HEADS)
    row = lambda v: v[0].reshape(1, QK_DIFF)
    oa = _diff_attention(slopes, q1t, q2t, ka, vat, row(lambda_q1), row(lambda_k1), row(lambda_q2),
                         row(lambda_k2), subln_diff[0].reshape(HEAD_DIM, 1))
    ob = _fox_attention(qbt, kb, vbt, ft3, kx_fox)

    x1, h2 = _merge(oa, ob, gates, x2, w_branch_diff[0].astype(_bf16), w_branch_fox[0].astype(_bf16),
                    w_out[0].astype(_bf16), norm_mlp[0])
    out = _mlp(h2, w_mlp_up[0].astype(_bf16), w_mlp_down[0].astype(_bf16), x1)
    return out[None]
```

```python
import functools

import jax
import jax.numpy as jnp
from jax import lax
from jax.experimental import pallas as pl
from jax.experimental.pallas import tpu as pltpu

D_MODEL = 2048
SEQ = 8192
HEAD_DIM = 128
N_HEADS = 8
QK_DIFF = 64
WIDTH = N_HEADS * HEAD_DIM
D_FF = 4 * D_MODEL
EPS = 1e-6
LAMBDA_INIT = 0.8 - 0.6 * 1.0
NEG_BIG = -1e30

TQ = 1024
TQS = 256
TK = 256
TKM = TQ
NSUB = TQ // TQS
LOG2E = 1.4426950408889634
V_ROWS = HEAD_DIM + 16
VMEM_LIMIT = 56 * 1024 * 1024

_f32 = jnp.float32
_bf16 = jnp.bfloat16


def _cparams(sem):
    return pltpu.CompilerParams(dimension_semantics=sem, vmem_limit_bytes=VMEM_LIMIT)


def _dot(a, b):
    return jnp.dot(a, b, preferred_element_type=_f32)


def _rmsnorm_kernel(x_ref, g_ref, o_ref):
    x = x_ref[...]
    r = lax.rsqrt(jnp.mean(x * x, axis=-1, keepdims=True) + EPS)
    o_ref[...] = ((x * r) * g_ref[...]).astype(o_ref.dtype)


def _rmsnorm(x, g, tm=512):
    s, d = x.shape
    return pl.pallas_call(
        _rmsnorm_kernel,
        grid=(s // tm,),
        in_specs=[pl.BlockSpec((tm, d), lambda i: (i, 0)),
                  pl.BlockSpec((1, d), lambda i: (0, 0))],
        out_specs=pl.BlockSpec((tm, d), lambda i: (i, 0)),
        out_shape=jax.ShapeDtypeStruct((s, d), _bf16),
        compiler_params=_cparams(("parallel",)),
        name="rmsnorm_in",
    )(x, g.reshape(1, d))


def _head_rmsnorm(y, gain_row, groups):
    y2 = y * y
    if groups == 1:
        r = lax.rsqrt(jnp.mean(y2, axis=-1, keepdims=True) + EPS)
    else:
        lane = lax.broadcasted_iota(jnp.int32, y.shape, 1)
        lo = lane < QK_DIFF
        s_lo = jnp.sum(jnp.where(lo, y2, 0.0), axis=-1, keepdims=True)
        s_hi = jnp.sum(jnp.where(lo, 0.0, y2), axis=-1, keepdims=True)
        r = jnp.where(lo, lax.rsqrt(s_lo * (1.0 / QK_DIFF) + EPS),
                      lax.rsqrt(s_hi * (1.0 / QK_DIFF) + EPS))
    return (y * r) * gain_row


PROJ_SUB = 256


def _proj_subtiles(a_ref, w_ref, wb_ref, first_step, epilogue):
    @pl.when(first_step)
    def _():
        wb_ref[...] = w_ref[...].astype(wb_ref.dtype)

    for sub in range(a_ref.shape[0] // PROJ_SUB):
        rows = slice(sub * PROJ_SUB, (sub + 1) * PROJ_SUB)
        epilogue(rows, _dot(a_ref[rows, :], wb_ref[...]))


def _heads(acc):
    return [acc[:, hh * HEAD_DIM:(hh + 1) * HEAD_DIM] for hh in range(N_HEADS)]


def _proj_qdiff_kernel(a_ref, w_ref, g_ref, q1_ref, q2_ref, wb_ref):
    def epilogue(rows, acc):
        lo = lax.broadcasted_iota(jnp.int32, (PROJ_SUB, HEAD_DIM), 1) < QK_DIFF
        for hh, y in enumerate(_heads(acc)):
            yn = _head_rmsnorm(y, g_ref[...], 2) * (QK_DIFF ** -0.5 * LOG2E)
            q1_ref[hh, :, rows] = jnp.where(lo, yn, 0.0).T.astype(q1_ref.dtype)
            q2_ref[hh, :, rows] = jnp.where(lo, 0.0, yn).T.astype(q2_ref.dtype)

    _proj_subtiles(a_ref, w_ref, wb_ref, pl.program_id(0) == 0, epilogue)


def _proj_qfox_kernel(a_ref, w_ref, g_ref, q_ref, wb_ref):
    def epilogue(rows, acc):
        for hh, y in enumerate(_heads(acc)):
            yn = _head_rmsnorm(y, g_ref[...], 1) * (HEAD_DIM ** -0.5 * LOG2E)
            q_ref[hh, :, rows] = yn.T.astype(q_ref.dtype)

    _proj_subtiles(a_ref, w_ref, wb_ref, pl.program_id(0) == 0, epilogue)


def _proj_k_kernel(a_ref, w_ref, g_ref, k_ref, wb_ref, *, groups):
    def epilogue(rows, acc):
        for hh, y in enumerate(_heads(acc)):
            k_ref[hh, rows, :] = _head_rmsnorm(y, g_ref[...], groups).astype(k_ref.dtype)

    _proj_subtiles(a_ref, w_ref, wb_ref, pl.program_id(0) == 0, epilogue)


def _proj_vt_kernel(a_ref, w_ref, vt_ref, wb_ref):
    row = lax.broadcasted_iota(jnp.int32, (V_ROWS - HEAD_DIM, TKM), 0)
    ones_rows = jnp.where(row == 0, 1.0, 0.0).astype(vt_ref.dtype)
    for hh in range(N_HEADS):
        vt_ref[hh, 0, HEAD_DIM:, :] = ones_rows

    def epilogue(rows, acc):
        for hh, y in enumerate(_heads(acc)):
            vt_ref[hh, 0, :HEAD_DIM, rows] = y.T.astype(vt_ref.dtype)

    _proj_subtiles(a_ref, w_ref, wb_ref, pl.program_id(0) == 0, epilogue)


def _proj_gates_kernel(a_ref, w_ref, b_ref, o_ref, wb_ref):
    def epilogue(rows, acc):
        o_ref[rows, :] = (1.0 / (1.0 + jnp.exp(-(acc + b_ref[...])))).astype(o_ref.dtype)

    _proj_subtiles(a_ref, w_ref, wb_ref, pl.program_id(1) == 0, epilogue)


def _proj_call(kernel, h, w, col_block, extras, out_shapes, out_specs, name, tm=TKM):
    s, d = h.shape
    in_specs = [pl.BlockSpec((tm, d), lambda i: (i, 0)),
                pl.BlockSpec((d, WIDTH), lambda i: (0, col_block), pipeline_mode=pl.Buffered(1))]
    in_specs += [pl.BlockSpec(e.shape, lambda i: (0, 0)) for e in extras]
    return pl.pallas_call(
        kernel,
        grid=(s // tm,),
        in_specs=in_specs,
        out_specs=out_specs,
        out_shape=out_shapes,
        scratch_shapes=[pltpu.VMEM((d, WIDTH), _bf16)],
        compiler_params=_cparams(("arbitrary",)),
        name=name,
    )(h, w, *extras)


def _t_spec(tm):
    return pl.BlockSpec((N_HEADS, HEAD_DIM, tm), lambda i: (0, 0, i))


def _k_spec(tm):
    return pl.BlockSpec((N_HEADS, tm, HEAD_DIM), lambda i: (0, i, 0))


def _gates(h, w, b, tm=1024, tn=1024):
    s, d = h.shape
    n = w.shape[1]
    return pl.pallas_call(
        _proj_gates_kernel,
        grid=(n // tn, s // tm),
        in_specs=[pl.BlockSpec((tm, d), lambda j, i: (i, 0)),
                  pl.BlockSpec((d, tn), lambda j, i: (0, j)),
                  pl.BlockSpec((1, tn), lambda j, i: (0, j))],
        out_specs=pl.BlockSpec((tm, tn), lambda j, i: (i, j)),
        out_shape=jax.ShapeDtypeStruct((s, n), _bf16),
        scratch_shapes=[pltpu.VMEM((d, tn), _bf16)],
        compiler_params=_cparams(("arbitrary", "arbitrary")),
        name="proj_gates",
    )(h, w, b.reshape(1, n))


F_ROWS = 16
F_CHUNK = 256


def _split3(x):
    hi = x.astype(_bf16)
    r1 = x - hi.astype(_f32)
    mid = r1.astype(_bf16)
    lo = (r1 - mid.astype(_f32)).astype(_bf16)
    return hi, mid, lo


def _select_rows_or_lanes(index, values):
    out = jnp.zeros(index.shape, _f32)
    for i in reversed(range(len(values))):
        out = jnp.where(index == i, values[i].astype(_f32), out)
    return out


def _forget_kernel(wt_ref, a_ref, b_ref, ft_ref, kx_ref, carry_ref):
    i = pl.program_id(0)

    @pl.when(i == 0)
    def _():
        carry_ref[...] = jnp.zeros_like(carry_ref)

    z = lax.dot_general(wt_ref[...], a_ref[...], (((1,), (1,)), ((), ())),
                        preferred_element_type=_f32) + b_ref[...]
    logf = (jnp.minimum(z, 0.0) - jnp.log(1.0 + jnp.exp(-jnp.abs(z)))) * LOG2E
    tm = logf.shape[1]
    r = lax.broadcasted_iota(jnp.int32, (F_CHUNK, F_CHUNK), 0)
    c = lax.broadcasted_iota(jnp.int32, (F_CHUNK, F_CHUNK), 1)
    upper = jnp.where(r <= c, 1.0, 0.0).astype(_bf16)
    carry = carry_ref[...]
    lane = lax.broadcasted_iota(jnp.int32, (HEAD_DIM, HEAD_DIM), 1)
    for ch in range(tm // F_CHUNK):
        x = logf[:, ch * F_CHUNK:(ch + 1) * F_CHUNK]
        hi, mid, lo = _split3(x)
        pre = _dot(hi, upper) + _dot(mid, upper) + _dot(lo, upper) + carry
        ft_ref[:, ch * F_CHUNK:(ch + 1) * F_CHUNK] = pre
        carry = pre[:, F_CHUNK - 1:F_CHUNK]
        for hh in range(N_HEADS):
            for sub in range(F_CHUNK // HEAD_DIM):
                row = pre[hh:hh + 1, sub * HEAD_DIM:(sub + 1) * HEAD_DIM]
                neg = -jnp.broadcast_to(row, (HEAD_DIM, HEAD_DIM)).T
                base = ch * F_CHUNK + sub * HEAD_DIM
                kx_ref[hh, base:base + HEAD_DIM, :] = _select_rows_or_lanes(
                    lane, _split3(neg)).astype(kx_ref.dtype)
    carry_ref[...] = carry


def _forget(h, wf_t, b_col, tm=1024):
    s, d = h.shape
    return pl.pallas_call(
        _forget_kernel,
        grid=(s // tm,),
        in_specs=[pl.BlockSpec((F_ROWS, d), lambda i: (0, 0)),
                  pl.BlockSpec((tm, d), lambda i: (i, 0)),
                  pl.BlockSpec((F_ROWS, 1), lambda i: (0, 0))],
        out_specs=[pl.BlockSpec((F_ROWS, tm), lambda i: (0, i)),
                   pl.BlockSpec((N_HEADS, tm, HEAD_DIM), lambda i: (0, i, 0))],
        out_shape=[jax.ShapeDtypeStruct((F_ROWS, s), _f32),
                   jax.ShapeDtypeStruct((N_HEADS, s, HEAD_DIM), _bf16)],
        scratch_shapes=[pltpu.VMEM((F_ROWS, 1), _f32)],
        compiler_params=_cparams(("arbitrary",)),
        name="forget_gate",
    )(wf_t, h, b_col)


def _block_partial(z, vt):
    mloc = jnp.max(z, axis=0, keepdims=True)
    o = _dot(vt, jnp.exp2(z - mloc).astype(_bf16))
    return mloc, o[HEAD_DIM:HEAD_DIM + 1], o[:HEAD_DIM]


def _merge_partials(m_ref, l_ref, acc_ref, idx, c, parts):
    cs = slice(c * TQS, (c + 1) * TQS)
    m_old = m_ref[idx, :, cs]
    m_new = m_old
    for mt, _, _ in parts:
        m_new = jnp.maximum(m_new, mt)
    a = jnp.exp2(m_old - m_new)
    l_new = a * l_ref[idx, :, cs]
    acc_new = a * acc_ref[idx, :, cs]
    for mt, l, o in parts:
        b = jnp.exp2(mt - m_new)
        l_new = l_new + b * l
        acc_new = acc_new + b * o
    m_ref[idx, :, cs] = m_new
    l_ref[idx, :, cs] = l_new
    acc_ref[idx, :, cs] = acc_new


def _init_state(m_ref, l_ref, acc_ref):
    m_ref[...] = jnp.full_like(m_ref, NEG_BIG)
    l_ref[...] = jnp.zeros_like(l_ref)
    acc_ref[...] = jnp.zeros_like(acc_ref)


def _pipelined_key_sweep(qi, scores, main_block, band_block, sa_ref, sb_ref):
    def fill(refs, block):
        for ref, val in zip(refs, scores(block)):
            ref[...] = val

    fill(sa_ref, 0)

    def pair(t, carry):
        fill(sb_ref, 2 * t + 1)
        main_block(2 * t, sa_ref)
        fill(sa_ref, 2 * t + 2)
        main_block(2 * t + 1, sb_ref)
        return carry

    lax.fori_loop(0, lax.shift_right_logical(qi, 1), pair, 0)
    odd = lax.rem(qi, 2) == 1

    @pl.when(odd)
    def _():
        fill(sb_ref, qi)
        main_block(qi - 1, sa_ref)
        band_block(sb_ref)

    @pl.when(jnp.logical_not(odd))
    def _():
        band_block(sa_ref)


def _local_iotas():
    lane = lax.broadcasted_iota(jnp.int32, (TK, TQS), 1)
    sub = lax.broadcasted_iota(jnp.int32, (TK, TQS), 0)
    return lane, sub


def _diff_attn_kernel(slope_ref, q1_ref, q2_ref, k_ref, vt_ref, lq1_ref, lk1_ref, lq2_ref, lk2_ref,
                      gsub_ref, o_ref, m_ref, l_ref, acc_ref, sa0_ref, sa1_ref, sb0_ref, sb1_ref):
    h = pl.program_id(0)
    qi = pl.program_id(1)
    neg_slope = -slope_ref[h] * LOG2E
    _init_state(m_ref, l_ref, acc_ref)
    qs = (q1_ref[0], q2_ref[0])

    slope2 = -neg_slope
    krow = lax.broadcasted_iota(jnp.int32, (TKM, HEAD_DIM), 0)
    klane = lax.broadcasted_iota(jnp.int32, (TKM, HEAD_DIM), 1)
    kx = jnp.where(klane < 3, (krow // HEAD_DIM).astype(_f32),
                   jnp.where(klane < 6, (krow % HEAD_DIM).astype(_f32), 0.0)).astype(_bf16)
    qrow = lax.broadcasted_iota(jnp.int32, (HEAD_DIM, TQ), 0)
    ones = jnp.ones((HEAD_DIM, TQ), _f32)
    qx = _select_rows_or_lanes(qrow, _split3(ones * (slope2 * HEAD_DIM)) + _split3(ones * slope2)).astype(_bf16)
    qs_aug = [jnp.concatenate([q, qx], axis=0) for q in qs]
    t_loc = lax.broadcasted_iota(jnp.int32, (1, TQS), 1).astype(_f32)

    def scores(block):
        start = pl.multiple_of(block * TKM, TKM)
        k_aug = jnp.concatenate([k_ref[0, pl.ds(start, TKM), :], kx], axis=1)
        return tuple(_dot(k_aug, q) for q in qs_aug)

    def query_shift(block, c):
        return (((qi - block) * TQ + c * TQS).astype(_f32) + t_loc) * neg_slope

    def main_block(block, s_refs):
        vt = vt_ref[0, block]
        for idx in range(2):
            for c in range(NSUB):
                mloc, l, o = _block_partial(s_refs[idx][:, c * TQS:(c + 1) * TQS], vt)
                _merge_partials(m_ref, l_ref, acc_ref, idx, c, [(mloc + query_shift(block, c), l, o)])

    def band_block(s_refs):
        lane, sub = _local_iotas()
        delta = (lane - sub).astype(_f32)
        fix_diag = jnp.where((sub // 64) <= (lane // 64),
                             (sub.astype(_f32) + jnp.abs(delta)) * neg_slope, NEG_BIG)
        parts = [[[] for _ in range(NSUB)] for _ in range(2)]
        vt_band = vt_ref[0, qi]
        for j in range(NSUB):
            vt = vt_band[:, j * TK:(j + 1) * TK]
            for idx in range(2):
                for c in range(j, NSUB):
                    z = s_refs[idx][j * TK:(j + 1) * TK, c * TQS:(c + 1) * TQS]
                    if c == j:
                        mloc, l, o = _block_partial(z + fix_diag, vt)
                        parts[idx][c].append((mloc + (j * TK) * neg_slope, l, o))
                    else:
                        mloc, l, o = _block_partial(z, vt)
                        parts[idx][c].append((mloc + query_shift(qi, c), l, o))
        for idx in range(2):
            for c in range(NSUB):
                _merge_partials(m_ref, l_ref, acc_ref, idx, c, parts[idx][c])

    _pipelined_key_sweep(qi, scores, main_block, band_block, (sa0_ref, sa1_ref), (sb0_ref, sb1_ref))

    lam =(jnp.exp(jnp.sum(lq1_ref[...] * lk1_ref[...], axis=-1, keepdims=True))
           - jnp.exp(jnp.sum(lq2_ref[...] * lk2_ref[...], axis=-1, keepdims=True))
           + LAMBDA_INIT)
    o = acc_ref[0] / l_ref[0] - lam * (acc_ref[1] / l_ref[1])
    r = lax.rsqrt(jnp.mean(o * o, axis=0, keepdims=True) + EPS)
    y = (o * r) * gsub_ref[...] * (1.0 - LAMBDA_INIT)
    o_ref[...] = y.T.astype(o_ref.dtype)


def _diff_attention(slopes, q1t, q2t, k, vt, lq1, lk1, lq2, lk2, gsub_col):
    nq = SEQ // TQ
    nkv = SEQ // TKM
    vec = lambda: pl.BlockSpec((1, QK_DIFF), lambda h, q: (0, 0))
    return pl.pallas_call(
        _diff_attn_kernel,
        grid=(N_HEADS, nq),
        in_specs=[pl.BlockSpec(memory_space=pltpu.SMEM),
                  pl.BlockSpec((1, HEAD_DIM, TQ), lambda h, q: (h, 0, q)),
                  pl.BlockSpec((1, HEAD_DIM, TQ), lambda h, q: (h, 0, q)),
                  pl.BlockSpec((1, SEQ, HEAD_DIM), lambda h, q: (h, 0, 0)),
                  pl.BlockSpec((1, nkv, V_ROWS, TKM), lambda h, q: (h, 0, 0, 0)),
                  vec(), vec(), vec(), vec(),
                  pl.BlockSpec((HEAD_DIM, 1), lambda h, q: (0, 0))],
        out_specs=pl.BlockSpec((TQ, HEAD_DIM), lambda h, q: (q, h)),
        out_shape=jax.ShapeDtypeStruct((SEQ, WIDTH), _bf16),
        scratch_shapes=[pltpu.VMEM((2, 1, TQ), _f32),
                        pltpu.VMEM((2, 1, TQ), _f32),
                        pltpu.VMEM((2, HEAD_DIM, TQ), _f32)] + [pltpu.VMEM((TKM, TQ), _f32)] * 4,
        compiler_params=_cparams(("parallel", "parallel")),
        name="diff_attention",
    )(slopes, q1t, q2t, k, vt, lq1, lk1, lq2, lk2, gsub_col)


def _fox_attn_kernel(q_ref, k_ref, vt_ref, ft_ref, kx_ref, o_ref, m_ref, l_ref, acc_ref, sa_ref, sb_ref):
    qi = pl.program_id(1)
    _init_state(m_ref, l_ref, acc_ref)
    q = q_ref[0]
    ft = ft_ref[0]
    qrow = lax.broadcasted_iota(jnp.int32, (HEAD_DIM, TQ), 0)
    q_aug = jnp.concatenate([q, jnp.where(qrow < 3, 1.0, 0.0).astype(_bf16)], axis=0)

    def scores(block):
        start = pl.multiple_of(block * TKM, TKM)
        k_aug = jnp.concatenate([k_ref[0, pl.ds(start, TKM), :], kx_ref[0, pl.ds(start, TKM), :]], axis=1)
        return (_dot(k_aug, q_aug),)

    def main_block(block, s_ref):
        vt = vt_ref[0, block]
        for c in range(NSUB):
            cs = slice(c * TQS, (c + 1) * TQS)
            mloc, l, o = _block_partial(s_ref[0][:, cs], vt)
            _merge_partials(m_ref, l_ref, acc_ref, 0, c, [(mloc + ft[:, cs], l, o)])

    def band_block(s_ref):
        lane, sub = _local_iotas()
        causal = sub <= lane
        parts = [[] for _ in range(NSUB)]
        vt_band = vt_ref[0, qi]
        for j in range(NSUB):
            vt = vt_band[:, j * TK:(j + 1) * TK]
            for c in range(j, NSUB):
                z = s_ref[0][j * TK:(j + 1) * TK, c * TQS:(c + 1) * TQS]
                if c == j:
                    z = jnp.where(causal, z, NEG_BIG)
                mloc, l, o = _block_partial(z, vt)
                parts[c].append((mloc + ft[:, c * TQS:(c + 1) * TQS], l, o))
        for c in range(NSUB):
            _merge_partials(m_ref, l_ref, acc_ref, 0, c, parts[c])

    _pipelined_key_sweep(qi, scores, main_block, band_block, (sa_ref,), (sb_ref,))

    o = acc_ref[0] / l_ref[0]
    o_ref[...] = o.T.astype(o_ref.dtype)


def _fox_attention(qt, k, vt, ft3, kx):
    nq = SEQ // TQ
    nkv = SEQ // TKM
    return pl.pallas_call(
        _fox_attn_kernel,
        grid=(N_HEADS, nq),
        in_specs=[pl.BlockSpec((1, HEAD_DIM, TQ), lambda h, q: (h, 0, q)),
                  pl.BlockSpec((1, SEQ, HEAD_DIM), lambda h, q: (h, 0, 0)),
                  pl.BlockSpec((1, nkv, V_ROWS, TKM), lambda h, q: (h, 0, 0, 0)),
                  pl.BlockSpec((1, 1, TQ), lambda h, q: (h, 0, q)),
                  pl.BlockSpec((1, SEQ, HEAD_DIM), lambda h, q: (h, 0, 0))],
        out_specs=pl.BlockSpec((TQ, HEAD_DIM), lambda h, q: (q, h)),
        out_shape=jax.ShapeDtypeStruct((SEQ, WIDTH), _bf16),
        scratch_shapes=[pltpu.VMEM((1, 1, TQ), _f32),
                        pltpu.VMEM((1, 1, TQ), _f32),
                        pltpu.VMEM((1, HEAD_DIM, TQ), _f32),
                        pltpu.VMEM((TKM, TQ), _f32),
                        pltpu.VMEM((TKM, TQ), _f32)],
        compiler_params=_cparams(("parallel", "parallel")),
        name="fox_attention",
    )(qt, k, vt, ft3, kx)


def _merge_kernel(oa_ref, ob_ref, g_ref, x_ref, wbd_ref, wbf_ref, wout_ref, gm_ref, x1_ref, h2_ref):
    a = _dot(oa_ref[...], wbd_ref[...])
    b = _dot(ob_ref[...], wbf_ref[...])
    g = g_ref[...].astype(_f32)
    merged = g[:, :D_MODEL] * a + g[:, D_MODEL:] * b
    x1 = x_ref[...] + _dot(merged.astype(_bf16), wout_ref[...])
    x1_ref[...] = x1
    r = lax.rsqrt(jnp.mean(x1 * x1, axis=-1, keepdims=True) + EPS)
    h2_ref[...] = ((x1 * r) * gm_ref[...]).astype(h2_ref.dtype)


def _merge(oa, ob, gates, x, wbd, wbf, wout, gm, tm=256):
    s, d = x.shape
    const = lambda shape: pl.BlockSpec(shape, lambda i: (0, 0), pipeline_mode=pl.Buffered(1))
    return pl.pallas_call(
        _merge_kernel,
        grid=(s // tm,),
        in_specs=[pl.BlockSpec((tm, WIDTH), lambda i: (i, 0)),
                  pl.BlockSpec((tm, WIDTH), lambda i: (i, 0)),
                  pl.BlockSpec((tm, 2 * d), lambda i: (i, 0)),
                  pl.BlockSpec((tm, d), lambda i: (i, 0)),
                  const((WIDTH, d)), const((WIDTH, d)), const((d, d)),
                  pl.BlockSpec((1, d), lambda i: (0, 0))],
        out_specs=[pl.BlockSpec((tm, d), lambda i: (i, 0)),
                   pl.BlockSpec((tm, d), lambda i: (i, 0))],
        out_shape=[jax.ShapeDtypeStruct((s, d), _f32),
                   jax.ShapeDtypeStruct((s, d), _bf16)],
        compiler_params=_cparams(("parallel",)),
        name="merge_out_proj",
    )(oa, ob, gates, x, wbd, wbf, wout, gm.reshape(1, d))


def _mlp_kernel(h_ref, wu_ref, wd_ref, x_ref, o_ref):
    k = pl.program_id(1)

    @pl.when(k == 0)
    def _():
        o_ref[...] = x_ref[...]

    u = jnp.maximum(_dot(h_ref[...], wu_ref[...]), 0.0)
    o_ref[...] += _dot((u * u).astype(_bf16), wd_ref[...])


def _mlp(h2, wu, wd, x1, tm=512, tf=1024):
    s, d = x1.shape
    f = wu.shape[1]
    return pl.pallas_call(
        _mlp_kernel,
        grid=(s // tm, f // tf),
        in_specs=[pl.BlockSpec((tm, d), lambda i, k: (i, 0)),
                  pl.BlockSpec((d, tf), lambda i, k: (0, k)),
                  pl.BlockSpec((tf, d), lambda i, k: (k, 0)),
                  pl.BlockSpec((tm, d), lambda i, k: (i, 0))],
        out_specs=pl.BlockSpec((tm, d), lambda i, k: (i, 0)),
        out_shape=jax.ShapeDtypeStruct((s, d), _f32),
        compiler_params=_cparams(("parallel", "arbitrary")),
        name="mlp_relu2",
    )(h2, wu, wd, x1)


def kernel(x, norm_mix, w_in, b_forget, qnorm_diff, knorm_diff, lambda_q1, lambda_k1, lambda_q2, lambda_k2,
           subln_diff, qnorm_fox, knorm_fox, w_branch_diff, w_branch_fox, w_gate, b_gate, w_out, norm_mlp,
           w_mlp_up, w_mlp_down):
    assert x.shape == (1, SEQ, D_MODEL)
    x2 = x[0]
    w_in0 = w_in[0]

    h = _rmsnorm(x2, norm_mix[0])

    g_qd = jnp.tile(qnorm_diff[0], 2).reshape(1, HEAD_DIM)
    g_kd = jnp.tile(knorm_diff[0], 2).reshape(1, HEAD_DIM)
    tm = TKM
    t_shape = jax.ShapeDtypeStruct((N_HEADS, HEAD_DIM, SEQ), _bf16)
    k_shape = jax.ShapeDtypeStruct((N_HEADS, SEQ, HEAD_DIM), _bf16)
    vt_shape = jax.ShapeDtypeStruct((N_HEADS, SEQ // TKM, V_ROWS, TKM), _bf16)
    vt_spec = pl.BlockSpec((N_HEADS, 1, V_ROWS, TKM), lambda i: (0, i, 0, 0))

    q1t, q2t = _proj_call(_proj_qdiff_kernel, h, w_in0, 0, [g_qd], [t_shape, t_shape],
                          [_t_spec(tm), _t_spec(tm)], "proj_q_diff")
    ka = _proj_call(functools.partial(_proj_k_kernel, groups=2), h, w_in0, 1, [g_kd], k_shape,
                    _k_spec(tm), "proj_k_diff")
    vat = _proj_call(_proj_vt_kernel, h, w_in0, 2, [], vt_shape, vt_spec, "proj_v_diff")
    qbt = _proj_call(_proj_qfox_kernel, h, w_in0, 3, [qnorm_fox[0].reshape(1, HEAD_DIM)], t_shape,
                     _t_spec(tm), "proj_q_fox")
    kb = _proj_call(functools.partial(_proj_k_kernel, groups=1), h, w_in0, 4,
                    [knorm_fox[0].reshape(1, HEAD_DIM)], k_shape, _k_spec(tm), "proj_k_fox")
    vbt = _proj_call(_proj_vt_kernel, h, w_in0, 5, [], vt_shape, vt_spec, "proj_v_fox")

    wf_t = jnp.zeros((F_ROWS, D_MODEL), _bf16).at[:N_HEADS].set(w_in0[:, 6 * WIDTH:].T.astype(_bf16))
    bf_col = jnp.zeros((F_ROWS, 1), _f32).at[:N_HEADS, 0].set(b_forget[0])
    ft, kx_fox = _forget(h, wf_t, bf_col)
    ft3 = ft.reshape(F_ROWS, 1, SEQ)

    gates = _gates(h, w_gate[0], b_gate[0])

    slopes = 2.0 ** (-8.0 * jnp.arange(1, N_HEADS + 1, dtype=_f32) / N_HEADS)
    row = lambda v: v[0].reshape(1, QK_DIFF)
    oa = _diff_attention(slopes, q1t, q2t, ka, vat, row(lambda_q1), row(lambda_k1), row(lambda_q2),
                         row(lambda_k2), subln_diff[0].reshape(HEAD_DIM, 1))
    ob = _fox_attention(qbt, kb, vbt, ft3, kx_fox)

    x1, h2 = _merge(oa, ob, gates, x2, w_branch_diff[0].astype(_bf16), w_branch_fox[0].astype(_bf16),
                    w_out[0].astype(_bf16), norm_mlp[0])
    out = _mlp(h2, w_mlp_up[0].astype(_bf16), w_mlp_down[0].astype(_bf16), x1)
    return out[None]
```

```python
import functools

import jax
import jax.numpy as jnp
from jax import lax
from jax.experimental import pallas as pl
from jax.experimental.pallas import tpu as pltpu

D_MODEL = 2048
SEQ = 8192
HEAD_DIM = 128
N_HEADS = 8
QK_DIFF = 64
WIDTH = N_HEADS * HEAD_DIM
D_FF = 4 * D_MODEL
EPS = 1e-6
LAMBDA_INIT = 0.8 - 0.6 * 1.0
NEG_BIG = -1e30

TQ = 1024
TQS = 256
TK = 256
TKM = TQ
NSUB = TQ // TQS
LOG2E = 1.4426950408889634
V_ROWS = HEAD_DIM + 16
VMEM_LIMIT = 56 * 1024 * 1024

_f32 = jnp.float32
_bf16 = jnp.bfloat16


def _cparams(sem):
    return pltpu.CompilerParams(dimension_semantics=sem, vmem_limit_bytes=VMEM_LIMIT)


def _dot(a, b):
    return jnp.dot(a, b, preferred_element_type=_f32)


def _rmsnorm_kernel(x_ref, g_ref, o_ref):
    x = x_ref[...]
    r = lax.rsqrt(jnp.mean(x * x, axis=-1, keepdims=True) + EPS)
    o_ref[...] = ((x * r) * g_ref[...]).astype(o_ref.dtype)


def _rmsnorm(x, g, tm=512):
    s, d = x.shape
    return pl.pallas_call(
        _rmsnorm_kernel,
        grid=(s // tm,),
        in_specs=[pl.BlockSpec((tm, d), lambda i: (i, 0)),
                  pl.BlockSpec((1, d), lambda i: (0, 0))],
        out_specs=pl.BlockSpec((tm, d), lambda i: (i, 0)),
        out_shape=jax.ShapeDtypeStruct((s, d), _bf16),
        compiler_params=_cparams(("parallel",)),
        name="rmsnorm_in",
    )(x, g.reshape(1, d))


def _head_rmsnorm(y, gain_row, groups):
    y2 = y * y
    if groups == 1:
        r = lax.rsqrt(jnp.mean(y2, axis=-1, keepdims=True) + EPS)
    else:
        lane = lax.broadcasted_iota(jnp.int32, y.shape, 1)
        lo = lane < QK_DIFF
        s_lo = jnp.sum(jnp.where(lo, y2, 0.0), axis=-1, keepdims=True)
        s_hi = jnp.sum(jnp.where(lo, 0.0, y2), axis=-1, keepdims=True)
        r = jnp.where(lo, lax.rsqrt(s_lo * (1.0 / QK_DIFF) + EPS),
                      lax.rsqrt(s_hi * (1.0 / QK_DIFF) + EPS))
    return (y * r) * gain_row


PROJ_SUB = 256


def _proj_subtiles(a_ref, w_ref, wb_ref, first_step, epilogue):
    @pl.when(first_step)
    def _():
        wb_ref[...] = w_ref[...].astype(wb_ref.dtype)

    for sub in range(a_ref.shape[0] // PROJ_SUB):
        rows = slice(sub * PROJ_SUB, (sub + 1) * PROJ_SUB)
        epilogue(rows, _dot(a_ref[rows, :], wb_ref[...]))


def _heads(acc):
    return [acc[:, hh * HEAD_DIM:(hh + 1) * HEAD_DIM] for hh in range(N_HEADS)]


def _proj_qdiff_kernel(a_ref, w_ref, g_ref, q1_ref, q2_ref, wb_ref):
    def epilogue(rows, acc):
        lo = lax.broadcasted_iota(jnp.int32, (PROJ_SUB, HEAD_DIM), 1) < QK_DIFF
        for hh, y in enumerate(_heads(acc)):
            yn = _head_rmsnorm(y, g_ref[...], 2) * (QK_DIFF ** -0.5 * LOG2E)
            q1_ref[hh, :, rows] = jnp.where(lo, yn, 0.0).T.astype(q1_ref.dtype)
            q2_ref[hh, :, rows] = jnp.where(lo, 0.0, yn).T.astype(q2_ref.dtype)

    _proj_subtiles(a_ref, w_ref, wb_ref, pl.program_id(0) == 0, epilogue)


def _proj_qfox_kernel(a_ref, w_ref, g_ref, q_ref, wb_ref):
    def epilogue(rows, acc):
        for hh, y in enumerate(_heads(acc)):
            yn = _head_rmsnorm(y, g_ref[...], 1) * (HEAD_DIM ** -0.5 * LOG2E)
            q_ref[hh, :, rows] = yn.T.astype(q_ref.dtype)

    _proj_subtiles(a_ref, w_ref, wb_ref, pl.program_id(0) == 0, epilogue)


def _proj_k_kernel(a_ref, w_ref, g_ref, k_ref, wb_ref, *, groups):
    def epilogue(rows, acc):
        for hh, y in enumerate(_heads(acc)):
            k_ref[hh, rows, :] = _head_rmsnorm(y, g_ref[...], groups).astype(k_ref.dtype)

    _proj_subtiles(a_ref, w_ref, wb_ref, pl.program_id(0) == 0, epilogue)


def _proj_vt_kernel(a_ref, w_ref, vt_ref, wb_ref):
    row = lax.broadcasted_iota(jnp.int32, (V_ROWS - HEAD_DIM, TKM), 0)
    ones_rows = jnp.where(row == 0, 1.0, 0.0).astype(vt_ref.dtype)
    for hh in range(N_HEADS):
        vt_ref[hh, 0, HEAD_DIM:, :] = ones_rows

    def epilogue(rows, acc):
        for hh, y in enumerate(_heads(acc)):
            vt_ref[hh, 0, :HEAD_DIM, rows] = y.T.astype(vt_ref.dtype)

    _proj_subtiles(a_ref, w_ref, wb_ref, pl.program_id(0) == 0, epilogue)


def _proj_gates_kernel(a_ref, w_ref, b_ref, *refs, n_cast):
    cast_in, o_ref, cast_out, wb_ref = refs[:n_cast], refs[n_cast], refs[n_cast + 1:-1], refs[-1]
    for src, dst in zip(cast_in, cast_out):
        dst[...] = src[...].astype(dst.dtype)

    def epilogue(rows, acc):
        o_ref[rows, :] = (1.0 / (1.0 + jnp.exp(-(acc + b_ref[...])))).astype(o_ref.dtype)

    _proj_subtiles(a_ref, w_ref, wb_ref, pl.program_id(1) == 0, epilogue)


def _proj_call(kernel, h, w, col_block, extras, out_shapes, out_specs, name, tm=TKM):
    s, d = h.shape
    in_specs = [pl.BlockSpec((tm, d), lambda i: (i, 0)),
                pl.BlockSpec((None, d, WIDTH), lambda i: (0, 0, col_block), pipeline_mode=pl.Buffered(1))]
    in_specs += [pl.BlockSpec(e.shape, lambda i: (0, 0)) for e in extras]
    return pl.pallas_call(
        kernel,
        grid=(s // tm,),
        in_specs=in_specs,
        out_specs=out_specs,
        out_shape=out_shapes,
        scratch_shapes=[pltpu.VMEM((d, WIDTH), _bf16)],
        compiler_params=_cparams(("arbitrary",)),
        name=name,
    )(h, w, *extras)


def _t_spec(tm):
    return pl.BlockSpec((N_HEADS, HEAD_DIM, tm), lambda i: (0, 0, i))


def _k_spec(tm):
    return pl.BlockSpec((N_HEADS, tm, HEAD_DIM), lambda i: (0, i, 0))


def _gates(h, w, b, cast_weights, tm=1024, tn=1024):
    s, d = h.shape
    n = w.shape[2]
    ni = s // tm
    nsteps = (n // tn) * ni
    cast_in, cast_out, cast_shapes = [], [], []
    for cw in cast_weights:
        _, rows, cols = cw.shape
        slab = rows // nsteps
        assert slab * nsteps == rows and slab % 16 == 0
        cast_in.append(pl.BlockSpec((None, slab, cols), lambda j, i: (0, j * ni + i, 0)))
        cast_out.append(pl.BlockSpec((slab, cols), lambda j, i: (j * ni + i, 0)))
        cast_shapes.append(jax.ShapeDtypeStruct((rows, cols), _bf16))
    outs = pl.pallas_call(
        functools.partial(_proj_gates_kernel, n_cast=len(cast_weights)),
        grid=(n // tn, ni),
        in_specs=[pl.BlockSpec((tm, d), lambda j, i: (i, 0)),
                  pl.BlockSpec((None, d, tn), lambda j, i: (0, 0, j)),
                  pl.BlockSpec((1, tn), lambda j, i: (0, j))] + cast_in,
        out_specs=[pl.BlockSpec((tm, tn), lambda j, i: (i, j))] + cast_out,
        out_shape=[jax.ShapeDtypeStruct((s, n), _bf16)] + cast_shapes,
        scratch_shapes=[pltpu.VMEM((d, tn), _bf16)],
        compiler_params=_cparams(("arbitrary", "arbitrary")),
        name="proj_gates",
    )(h, w, b.reshape(1, n), *cast_weights)
    return outs[0], outs[1:]


F_ROWS = 16
F_CHUNK = 256


def _split3(x):
    hi = x.astype(_bf16)
    r1 = x - hi.astype(_f32)
    mid = r1.astype(_bf16)
    lo = (r1 - mid.astype(_f32)).astype(_bf16)
    return hi, mid, lo


def _select_rows_or_lanes(index, values):
    out = jnp.zeros(index.shape, _f32)
    for i in reversed(range(len(values))):
        out = jnp.where(index == i, values[i].astype(_f32), out)
    return out


def _forget_kernel(wt_ref, a_ref, b_ref, ft_ref, kx_ref, carry_ref):
    i = pl.program_id(0)

    @pl.when(i == 0)
    def _():
        carry_ref[...] = jnp.zeros_like(carry_ref)

    z = lax.dot_general(wt_ref[...], a_ref[...], (((1,), (1,)), ((), ())),
                        preferred_element_type=_f32) + b_ref[...]
    logf = (jnp.minimum(z, 0.0) - jnp.log(1.0 + jnp.exp(-jnp.abs(z)))) * LOG2E
    tm = logf.shape[1]
    r = lax.broadcasted_iota(jnp.int32, (F_CHUNK, F_CHUNK), 0)
    c = lax.broadcasted_iota(jnp.int32, (F_CHUNK, F_CHUNK), 1)
    upper = jnp.where(r <= c, 1.0, 0.0).astype(_bf16)
    carry = carry_ref[...]
    lane = lax.broadcasted_iota(jnp.int32, (HEAD_DIM, HEAD_DIM), 1)
    for ch in range(tm // F_CHUNK):
        x = logf[:, ch * F_CHUNK:(ch + 1) * F_CHUNK]
        hi, mid, lo = _split3(x)
        pre = _dot(hi, upper) + _dot(mid, upper) + _dot(lo, upper) + carry
        ft_ref[:, ch * F_CHUNK:(ch + 1) * F_CHUNK] = pre
        carry = pre[:, F_CHUNK - 1:F_CHUNK]
        for hh in range(N_HEADS):
            for sub in range(F_CHUNK // HEAD_DIM):
                row = pre[hh:hh + 1, sub * HEAD_DIM:(sub + 1) * HEAD_DIM]
                neg = -jnp.broadcast_to(row, (HEAD_DIM, HEAD_DIM)).T
                base = ch * F_CHUNK + sub * HEAD_DIM
                kx_ref[hh, base:base + HEAD_DIM, :] = _select_rows_or_lanes(
                    lane, _split3(neg)).astype(kx_ref.dtype)
    carry_ref[...] = carry


def _forget(h, wf_t, b_col, tm=1024):
    s, d = h.shape
    return pl.pallas_call(
        _forget_kernel,
        grid=(s // tm,),
        in_specs=[pl.BlockSpec((F_ROWS, d), lambda i: (0, 0)),
                  pl.BlockSpec((tm, d), lambda i: (i, 0)),
                  pl.BlockSpec((F_ROWS, 1), lambda i: (0, 0))],
        out_specs=[pl.BlockSpec((F_ROWS, tm), lambda i: (0, i)),
                   pl.BlockSpec((N_HEADS, tm, HEAD_DIM), lambda i: (0, i, 0))],
        out_shape=[jax.ShapeDtypeStruct((F_ROWS, s), _f32),
                   jax.ShapeDtypeStruct((N_HEADS, s, HEAD_DIM), _bf16)],
        scratch_shapes=[pltpu.VMEM((F_ROWS, 1), _f32)],
        compiler_params=_cparams(("arbitrary",)),
        name="forget_gate",
    )(wf_t, h, b_col)


def _block_partial(z, vt):
    mloc = jnp.max(z, axis=0, keepdims=True)
    o = _dot(vt, jnp.exp2(z - mloc).astype(_bf16))
    return mloc, o[HEAD_DIM:HEAD_DIM + 1], o[:HEAD_DIM]


def _merge_partials(m_ref, l_ref, acc_ref, idx, c, parts):
    cs = slice(c * TQS, (c + 1) * TQS)
    m_old = m_ref[idx, :, cs]
    m_new = m_old
    for mt, _, _ in parts:
        m_new = jnp.maximum(m_new, mt)
    a = jnp.exp2(m_old - m_new)
    l_new = a * l_ref[idx, :, cs]
    acc_new = a * acc_ref[idx, :, cs]
    for mt, l, o in parts:
        b = jnp.exp2(mt - m_new)
        l_new = l_new + b * l
        acc_new = acc_new + b * o
    m_ref[idx, :, cs] = m_new
    l_ref[idx, :, cs] = l_new
    acc_ref[idx, :, cs] = acc_new


def _init_state(m_ref, l_ref, acc_ref):
    m_ref[...] = jnp.full_like(m_ref, NEG_BIG)
    l_ref[...] = jnp.zeros_like(l_ref)
    acc_ref[...] = jnp.zeros_like(acc_ref)


def _pipelined_key_sweep(qi, scores, main_block, band_block, sa_ref, sb_ref):
    def fill(refs, block):
        for ref, val in zip(refs, scores(block)):
            ref[...] = val

    fill(sa_ref, 0)

    def pair(t, carry):
        fill(sb_ref, 2 * t + 1)
        main_block(2 * t, sa_ref)
        fill(sa_ref, 2 * t + 2)
        main_block(2 * t + 1, sb_ref)
        return carry

    lax.fori_loop(0, lax.shift_right_logical(qi, 1), pair, 0)
    odd = lax.rem(qi, 2) == 1

    @pl.when(odd)
    def _():
        fill(sb_ref, qi)
        main_block(qi - 1, sa_ref)
        band_block(sb_ref)

    @pl.when(jnp.logical_not(odd))
    def _():
        band_block(sa_ref)


def _local_iotas():
    lane = lax.broadcasted_iota(jnp.int32, (TK, TQS), 1)
    sub = lax.broadcasted_iota(jnp.int32, (TK, TQS), 0)
    return lane, sub


def _diff_attn_kernel(slope_ref, q1_ref, q2_ref, k_ref, vt_ref, lq1_ref, lk1_ref, lq2_ref, lk2_ref,
                      gsub_ref, o_ref, m_ref, l_ref, acc_ref, sa0_ref, sa1_ref, sb0_ref, sb1_ref):
    h = pl.program_id(0)
    qi = pl.program_id(1)
    neg_slope = -slope_ref[h] * LOG2E
    _init_state(m_ref, l_ref, acc_ref)
    qs = (q1_ref[0], q2_ref[0])

    slope2 = -neg_slope
    krow = lax.broadcasted_iota(jnp.int32, (TKM, HEAD_DIM), 0)
    klane = lax.broadcasted_iota(jnp.int32, (TKM, HEAD_DIM), 1)
    kx = jnp.where(klane < 3, (krow // HEAD_DIM).astype(_f32),
                   jnp.where(klane < 6, (krow % HEAD_DIM).astype(_f32), 0.0)).astype(_bf16)
    qrow = lax.broadcasted_iota(jnp.int32, (HEAD_DIM, TQ), 0)
    ones = jnp.ones((HEAD_DIM, TQ), _f32)
    qx = _select_rows_or_lanes(qrow, _split3(ones * (slope2 * HEAD_DIM)) + _split3(ones * slope2)).astype(_bf16)
    qs_aug = [jnp.concatenate([q, qx], axis=0) for q in qs]
    t_loc = lax.broadcasted_iota(jnp.int32, (1, TQS), 1).astype(_f32)

    def scores(block):
        start = pl.multiple_of(block * TKM, TKM)
        k_aug = jnp.concatenate([k_ref[0, pl.ds(start, TKM), :], kx], axis=1)
        return tuple(_dot(k_aug, q) for q in qs_aug)

    def query_shift(block, c):
        return (((qi - block) * TQ + c * TQS).astype(_f32) + t_loc) * neg_slope

    def main_block(block, s_refs):
        vt = vt_ref[0, block]
        for idx in range(2):
            for c in range(NSUB):
                mloc, l, o = _block_partial(s_refs[idx][:, c * TQS:(c + 1) * TQS], vt)
                _merge_partials(m_ref, l_ref, acc_ref, idx, c, [(mloc + query_shift(block, c), l, o)])

    def band_block(s_refs):
        lane, sub = _local_iotas()
        delta = (lane - sub).astype(_f32)
        fix_diag = jnp.where((sub // 64) <= (lane // 64),
                             (sub.astype(_f32) + jnp.abs(delta)) * neg_slope, NEG_BIG)
        parts = [[[] for _ in range(NSUB)] for _ in range(2)]
        vt_band = vt_ref[0, qi]
        for j in range(NSUB):
            vt = vt_band[:, j * TK:(j + 1) * TK]
            for idx in range(2):
                for c in range(j, NSUB):
                    z = s_refs[idx][j * TK:(j + 1) * TK, c * TQS:(c + 1) * TQS]
                    if c == j:
                        mloc, l, o = _block_partial(z + fix_diag, vt)
                        parts[idx][c].append((mloc + (j * TK) * neg_slope, l, o))
                    else:
                        mloc, l, o = _block_partial(z, vt)
                        parts[idx][c].append((mloc + query_shift(qi, c), l, o))
        for idx in range(2):
            for c in range(NSUB):
                _merge_partials(m_ref, l_ref, acc_ref, idx, c, parts[idx][c])

    _pipelined_key_sweep(qi, scores, main_block, band_block, (sa0_ref, sa1_ref), (sb0_ref, sb1_ref))

    lam =(jnp.exp(jnp.sum(lq1_ref[...] * lk1_ref[...], axis=-1, keepdims=True))
           - jnp.exp(jnp.sum(lq2_ref[...] * lk2_ref[...], axis=-1, keepdims=True))
           + LAMBDA_INIT)
    o = acc_ref[0] / l_ref[0] - lam * (acc_ref[1] / l_ref[1])
    r = lax.rsqrt(jnp.mean(o * o, axis=0, keepdims=True) + EPS)
    y = (o * r) * gsub_ref[...] * (1.0 - LAMBDA_INIT)
    o_ref[...] = y.T.astype(o_ref.dtype)


def _diff_attention(slopes, q1t, q2t, k, vt, lq1, lk1, lq2, lk2, gsub_col):
    nq = SEQ // TQ
    nkv = SEQ // TKM
    vec = lambda: pl.BlockSpec((1, QK_DIFF), lambda h, q: (0, 0))
    return pl.pallas_call(
        _diff_attn_kernel,
        grid=(N_HEADS, nq),
        in_specs=[pl.BlockSpec(memory_space=pltpu.SMEM),
                  pl.BlockSpec((1, HEAD_DIM, TQ), lambda h, q: (h, 0, q)),
                  pl.BlockSpec((1, HEAD_DIM, TQ), lambda h, q: (h, 0, q)),
                  pl.BlockSpec((1, SEQ, HEAD_DIM), lambda h, q: (h, 0, 0)),
                  pl.BlockSpec((1, nkv, V_ROWS, TKM), lambda h, q: (h, 0, 0, 0)),
                  vec(), vec(), vec(), vec(),
                  pl.BlockSpec((HEAD_DIM, 1), lambda h, q: (0, 0))],
        out_specs=pl.BlockSpec((TQ, HEAD_DIM), lambda h, q: (q, h)),
        out_shape=jax.ShapeDtypeStruct((SEQ, WIDTH), _bf16),
        scratch_shapes=[pltpu.VMEM((2, 1, TQ), _f32),
                        pltpu.VMEM((2, 1, TQ), _f32),
                        pltpu.VMEM((2, HEAD_DIM, TQ), _f32)] + [pltpu.VMEM((TKM, TQ), _f32)] * 4,
        compiler_params=_cparams(("parallel", "parallel")),
        name="diff_attention",
    )(slopes, q1t, q2t, k, vt, lq1, lk1, lq2, lk2, gsub_col)


def _fox_attn_kernel(q_ref, k_ref, vt_ref, ft_ref, kx_ref, o_ref, m_ref, l_ref, acc_ref, sa_ref, sb_ref):
    qi = pl.program_id(1)
    _init_state(m_ref, l_ref, acc_ref)
    q = q_ref[0]
    ft = ft_ref[0]
    qrow = lax.broadcasted_iota(jnp.int32, (HEAD_DIM, TQ), 0)
    q_aug = jnp.concatenate([q, jnp.where(qrow < 3, 1.0, 0.0).astype(_bf16)], axis=0)

    def scores(block):
        start = pl.multiple_of(block * TKM, TKM)
        k_aug = jnp.concatenate([k_ref[0, pl.ds(start, TKM), :], kx_ref[0, pl.ds(start, TKM), :]], axis=1)
        return (_dot(k_aug, q_aug),)

    def main_block(block, s_ref):
        vt = vt_ref[0, block]
        for c in range(NSUB):
            cs = slice(c * TQS, (c + 1) * TQS)
            mloc, l, o = _block_partial(s_ref[0][:, cs], vt)
            _merge_partials(m_ref, l_ref, acc_ref, 0, c, [(mloc + ft[:, cs], l, o)])

    def band_block(s_ref):
        lane, sub = _local_iotas()
        causal = sub <= lane
        parts = [[] for _ in range(NSUB)]
        vt_band = vt_ref[0, qi]
        for j in range(NSUB):
            vt = vt_band[:, j * TK:(j + 1) * TK]
            for c in range(j, NSUB):
                z = s_ref[0][j * TK:(j + 1) * TK, c * TQS:(c + 1) * TQS]
                if c == j:
                    z = jnp.where(causal, z, NEG_BIG)
                mloc, l, o = _block_partial(z, vt)
                parts[c].append((mloc + ft[:, c * TQS:(c + 1) * TQS], l, o))
        for c in range(NSUB):
            _merge_partials(m_ref, l_ref, acc_ref, 0, c, parts[c])

    _pipelined_key_sweep(qi, scores, main_block, band_block, (sa_ref,), (sb_ref,))

    o = acc_ref[0] / l_ref[0]
    o_ref[...] = o.T.astype(o_ref.dtype)


def _fox_attention(qt, k, vt, ft3, kx):
    nq = SEQ // TQ
    nkv = SEQ // TKM
    return pl.pallas_call(
        _fox_attn_kernel,
        grid=(N_HEADS, nq),
        in_specs=[pl.BlockSpec((1, HEAD_DIM, TQ), lambda h, q: (h, 0, q)),
                  pl.BlockSpec((1, SEQ, HEAD_DIM), lambda h, q: (h, 0, 0)),
                  pl.BlockSpec((1, nkv, V_ROWS, TKM), lambda h, q: (h, 0, 0, 0)),
                  pl.BlockSpec((1, 1, TQ), lambda h, q: (h, 0, q)),
                  pl.BlockSpec((1, SEQ, HEAD_DIM), lambda h, q: (h, 0, 0))],
        out_specs=pl.BlockSpec((TQ, HEAD_DIM), lambda h, q: (q, h)),
        out_shape=jax.ShapeDtypeStruct((SEQ, WIDTH), _bf16),
        scratch_shapes=[pltpu.VMEM((1, 1, TQ), _f32),
                        pltpu.VMEM((1, 1, TQ), _f32),
                        pltpu.VMEM((1, HEAD_DIM, TQ), _f32),
                        pltpu.VMEM((TKM, TQ), _f32),
                        pltpu.VMEM((TKM, TQ), _f32)],
        compiler_params=_cparams(("parallel", "parallel")),
        name="fox_attention",
    )(qt, k, vt, ft3, kx)


def _merge_kernel(oa_ref, ob_ref, g_ref, x_ref, wbd_ref, wbf_ref, wout_ref, gm_ref, x1_ref, h2_ref):
    a = _dot(oa_ref[...], wbd_ref[...])
    b = _dot(ob_ref[...], wbf_ref[...])
    g = g_ref[...].astype(_f32)
    merged = g[:, :D_MODEL] * a + g[:, D_MODEL:] * b
    x1 = x_ref[...] + _dot(merged.astype(_bf16), wout_ref[...])
    x1_ref[...] = x1
    r = lax.rsqrt(jnp.mean(x1 * x1, axis=-1, keepdims=True) + EPS)
    h2_ref[...] = ((x1 * r) * gm_ref[...]).astype(h2_ref.dtype)


def _merge(oa, ob, gates, x, wbd, wbf, wout, gm, tm=256):
    s, d = x.shape
    const = lambda shape: pl.BlockSpec(shape, lambda i: (0, 0), pipeline_mode=pl.Buffered(1))
    return pl.pallas_call(
        _merge_kernel,
        grid=(s // tm,),
        in_specs=[pl.BlockSpec((tm, WIDTH), lambda i: (i, 0)),
                  pl.BlockSpec((tm, WIDTH), lambda i: (i, 0)),
                  pl.BlockSpec((tm, 2 * d), lambda i: (i, 0)),
                  pl.BlockSpec((tm, d), lambda i: (i, 0)),
                  const((WIDTH, d)), const((WIDTH, d)), const((d, d)),
                  pl.BlockSpec((1, d), lambda i: (0, 0))],
        out_specs=[pl.BlockSpec((tm, d), lambda i: (i, 0)),
                   pl.BlockSpec((tm, d), lambda i: (i, 0))],
        out_shape=[jax.ShapeDtypeStruct((s, d), _f32),
                   jax.ShapeDtypeStruct((s, d), _bf16)],
        compiler_params=_cparams(("parallel",)),
        name="merge_out_proj",
    )(oa, ob, gates, x, wbd, wbf, wout, gm.reshape(1, d))


def _mlp_kernel(h_ref, wu_ref, wd_ref, x_ref, o_ref):
    k = pl.program_id(1)

    @pl.when(k == 0)
    def _():
        o_ref[...] = x_ref[...]

    u = jnp.maximum(_dot(h_ref[...], wu_ref[...]), 0.0)
    o_ref[...] += _dot((u * u).astype(_bf16), wd_ref[...])


def _mlp(h2, wu, wd, x1, tm=512, tf=1024):
    s, d = x1.shape
    f = wu.shape[1]
    return pl.pallas_call(
        _mlp_kernel,
        grid=(s // tm, f // tf),
        in_specs=[pl.BlockSpec((tm, d), lambda i, k: (i, 0)),
                  pl.BlockSpec((d, tf), lambda i, k: (0, k)),
                  pl.BlockSpec((tf, d), lambda i, k: (k, 0)),
                  pl.BlockSpec((tm, d), lambda i, k: (i, 0))],
        out_specs=pl.BlockSpec((tm, d), lambda i, k: (i, 0)),
        out_shape=jax.ShapeDtypeStruct((s, d), _f32),
        compiler_params=_cparams(("parallel", "arbitrary")),
        name="mlp_relu2",
    )(h2, wu, wd, x1)


def kernel(x, norm_mix, w_in, b_forget, qnorm_diff, knorm_diff, lambda_q1, lambda_k1, lambda_q2, lambda_k2,
           subln_diff, qnorm_fox, knorm_fox, w_branch_diff, w_branch_fox, w_gate, b_gate, w_out, norm_mlp,
           w_mlp_up, w_mlp_down):
    assert x.shape == (1, SEQ, D_MODEL)
    x2 = x[0]

    h = _rmsnorm(x2, norm_mix[0])

    g_qd = jnp.tile(qnorm_diff[0], 2).reshape(1, HEAD_DIM)
    g_kd = jnp.tile(knorm_diff[0], 2).reshape(1, HEAD_DIM)
    tm = TKM
    t_shape = jax.ShapeDtypeStruct((N_HEADS, HEAD_DIM, SEQ), _bf16)
    k_shape = jax.ShapeDtypeStruct((N_HEADS, SEQ, HEAD_DIM), _bf16)
    vt_shape = jax.ShapeDtypeStruct((N_HEADS, SEQ // TKM, V_ROWS, TKM), _bf16)
    vt_spec = pl.BlockSpec((N_HEADS, 1, V_ROWS, TKM), lambda i: (0, i, 0, 0))

    q1t, q2t = _proj_call(_proj_qdiff_kernel, h, w_in, 0, [g_qd], [t_shape, t_shape],
                          [_t_spec(tm), _t_spec(tm)], "proj_q_diff")
    ka = _proj_call(functools.partial(_proj_k_kernel, groups=2), h, w_in, 1, [g_kd], k_shape,
                    _k_spec(tm), "proj_k_diff")
    vat = _proj_call(_proj_vt_kernel, h, w_in, 2, [], vt_shape, vt_spec, "proj_v_diff")
    qbt = _proj_call(_proj_qfox_kernel, h, w_in, 3, [qnorm_fox[0].reshape(1, HEAD_DIM)], t_shape,
                     _t_spec(tm), "proj_q_fox")
    kb = _proj_call(functools.partial(_proj_k_kernel, groups=1), h, w_in, 4,
                    [knorm_fox[0].reshape(1, HEAD_DIM)], k_shape, _k_spec(tm), "proj_k_fox")
    vbt = _proj_call(_proj_vt_kernel, h, w_in, 5, [], vt_shape, vt_spec, "proj_v_fox")

    wf_t = jnp.zeros((F_ROWS, D_MODEL), _bf16).at[:N_HEADS].set(w_in[0, :, 6 * WIDTH:].T.astype(_bf16))
    bf_col = jnp.zeros((F_ROWS, 1), _f32).at[:N_HEADS, 0].set(b_forget[0])
    ft, kx_fox = _forget(h, wf_t, bf_col)
    ft3 = ft.reshape(F_ROWS, 1, SEQ)

    gates, (wbd, wbf, wout, wup, wdown) = _gates(
        h, w_gate, b_gate[0], [w_branch_diff, w_branch_fox, w_out, w_mlp_up, w_mlp_down])

    slopes = 2.0 ** (-8.0 * jnp.arange(1, N_HEADS + 1, dtype=_f32) / N_HEADS)
    row = lambda v: v[0].reshape(1, QK_DIFF)
    oa = _diff_attention(slopes, q1t, q2t, ka, vat, row(lambda_q1), row(lambda_k1), row(lambda_q2),
                         row(lambda_k2), subln_diff[0].reshape(HEAD_DIM, 1))
    ob = _fox_attention(qbt, kb, vbt, ft3, kx_fox)

    x1, h2 = _merge(oa, ob, gates, x2, wbd, wbf, wout, norm_mlp[0])
    out = _mlp(h2, wup, wdown, x1)
    return out[None]
```

```python
import functools

import jax
import jax.numpy as jnp
from jax import lax
from jax.experimental import pallas as pl
from jax.experimental.pallas import tpu as pltpu

D_MODEL = 2048
SEQ = 8192
HEAD_DIM = 128
N_HEADS = 8
QK_DIFF = 64
WIDTH = N_HEADS * HEAD_DIM
D_FF = 4 * D_MODEL
EPS = 1e-6
LAMBDA_INIT = 0.8 - 0.6 * 1.0
NEG_BIG = -1e30

TQ = 1024
TQS = 256
TK = 256
TKM = TQ
NSUB = TQ // TQS
LOG2E = 1.4426950408889634
V_ROWS = HEAD_DIM + 16
VMEM_LIMIT = 56 * 1024 * 1024

_f32 = jnp.float32
_bf16 = jnp.bfloat16


def _cparams(sem):
    return pltpu.CompilerParams(dimension_semantics=sem, vmem_limit_bytes=VMEM_LIMIT)


def _dot(a, b):
    return jnp.dot(a, b, preferred_element_type=_f32)


def _rmsnorm_kernel(x_ref, g_ref, o_ref):
    x = x_ref[...]
    r = lax.rsqrt(jnp.mean(x * x, axis=-1, keepdims=True) + EPS)
    o_ref[...] = ((x * r) * g_ref[...]).astype(o_ref.dtype)


def _rmsnorm(x, g, tm=512):
    s, d = x.shape
    return pl.pallas_call(
        _rmsnorm_kernel,
        grid=(s // tm,),
        in_specs=[pl.BlockSpec((tm, d), lambda i: (i, 0)),
                  pl.BlockSpec((1, d), lambda i: (0, 0))],
        out_specs=pl.BlockSpec((tm, d), lambda i: (i, 0)),
        out_shape=jax.ShapeDtypeStruct((s, d), _bf16),
        compiler_params=_cparams(("parallel",)),
        name="rmsnorm_in",
    )(x, g.reshape(1, d))


def _head_rmsnorm(y, gain_row, groups):
    y2 = y * y
    if groups == 1:
        r = lax.rsqrt(jnp.mean(y2, axis=-1, keepdims=True) + EPS)
    else:
        lane = lax.broadcasted_iota(jnp.int32, y.shape, 1)
        lo = lane < QK_DIFF
        s_lo = jnp.sum(jnp.where(lo, y2, 0.0), axis=-1, keepdims=True)
        s_hi = jnp.sum(jnp.where(lo, 0.0, y2), axis=-1, keepdims=True)
        r = jnp.where(lo, lax.rsqrt(s_lo * (1.0 / QK_DIFF) + EPS),
                      lax.rsqrt(s_hi * (1.0 / QK_DIFF) + EPS))
    return (y * r) * gain_row


PROJ_SUB = 256


def _proj_subtiles(a_ref, w_ref, wb_ref, first_step, epilogue, w_transposed=False):
    @pl.when(first_step)
    def _():
        w = w_ref[...].T if w_transposed else w_ref[...]
        wb_ref[...] = w.astype(wb_ref.dtype)

    for sub in range(a_ref.shape[0] // PROJ_SUB):
        rows = slice(sub * PROJ_SUB, (sub + 1) * PROJ_SUB)
        epilogue(rows, _dot(a_ref[rows, :], wb_ref[...]))


def _heads(acc):
    return [acc[:, hh * HEAD_DIM:(hh + 1) * HEAD_DIM] for hh in range(N_HEADS)]


def _proj_qdiff_kernel(a_ref, w_ref, g_ref, q1_ref, q2_ref, wb_ref):
    def epilogue(rows, acc):
        lo = lax.broadcasted_iota(jnp.int32, (PROJ_SUB, HEAD_DIM), 1) < QK_DIFF
        for hh, y in enumerate(_heads(acc)):
            yn = _head_rmsnorm(y, g_ref[...], 2) * (QK_DIFF ** -0.5 * LOG2E)
            q1_ref[hh, :, rows] = jnp.where(lo, yn, 0.0).T.astype(q1_ref.dtype)
            q2_ref[hh, :, rows] = jnp.where(lo, 0.0, yn).T.astype(q2_ref.dtype)

    _proj_subtiles(a_ref, w_ref, wb_ref, pl.program_id(0) == 0, epilogue, w_transposed=True)


def _proj_qfox_kernel(a_ref, w_ref, g_ref, q_ref, wb_ref):
    def epilogue(rows, acc):
        for hh, y in enumerate(_heads(acc)):
            yn = _head_rmsnorm(y, g_ref[...], 1) * (HEAD_DIM ** -0.5 * LOG2E)
            q_ref[hh, :, rows] = yn.T.astype(q_ref.dtype)

    _proj_subtiles(a_ref, w_ref, wb_ref, pl.program_id(0) == 0, epilogue, w_transposed=True)


def _proj_k_kernel(a_ref, w_ref, g_ref, k_ref, wb_ref, *, groups):
    def epilogue(rows, acc):
        for hh, y in enumerate(_heads(acc)):
            k_ref[hh, rows, :] = _head_rmsnorm(y, g_ref[...], groups).astype(k_ref.dtype)

    _proj_subtiles(a_ref, w_ref, wb_ref, pl.program_id(0) == 0, epilogue, w_transposed=True)


def _proj_vt_kernel(a_ref, w_ref, vt_ref, wb_ref):
    row = lax.broadcasted_iota(jnp.int32, (V_ROWS - HEAD_DIM, TKM), 0)
    ones_rows = jnp.where(row == 0, 1.0, 0.0).astype(vt_ref.dtype)
    for hh in range(N_HEADS):
        vt_ref[hh, 0, HEAD_DIM:, :] = ones_rows

    def epilogue(rows, acc):
        for hh, y in enumerate(_heads(acc)):
            vt_ref[hh, 0, :HEAD_DIM, rows] = y.T.astype(vt_ref.dtype)

    _proj_subtiles(a_ref, w_ref, wb_ref, pl.program_id(0) == 0, epilogue, w_transposed=True)


def _proj_gates_kernel(a_ref, w_ref, b_ref, *refs, n_cast):
    cast_in, o_ref, cast_out, wb_ref = refs[:n_cast], refs[n_cast], refs[n_cast + 1:-1], refs[-1]
    for src, dst in zip(cast_in, cast_out):
        dst[...] = src[...].astype(dst.dtype)

    def epilogue(rows, acc):
        o_ref[rows, :] = (1.0 / (1.0 + jnp.exp(-(acc + b_ref[...])))).astype(o_ref.dtype)

    _proj_subtiles(a_ref, w_ref, wb_ref, pl.program_id(1) == 0, epilogue)


def _proj_call(kernel, h, w, col_block, extras, out_shapes, out_specs, name, tm=TKM):
    s, d = h.shape
    in_specs = [pl.BlockSpec((tm, d), lambda i: (i, 0)),
                pl.BlockSpec((None, WIDTH, d), lambda i: (0, col_block, 0), pipeline_mode=pl.Buffered(1))]
    in_specs += [pl.BlockSpec(e.shape, lambda i: (0, 0)) for e in extras]
    return pl.pallas_call(
        kernel,
        grid=(s // tm,),
        in_specs=in_specs,
        out_specs=out_specs,
        out_shape=out_shapes,
        scratch_shapes=[pltpu.VMEM((d, WIDTH), _bf16)],
        compiler_params=_cparams(("arbitrary",)),
        name=name,
    )(h, w, *extras)


def _t_spec(tm):
    return pl.BlockSpec((N_HEADS, HEAD_DIM, tm), lambda i: (0, 0, i))


def _k_spec(tm):
    return pl.BlockSpec((N_HEADS, tm, HEAD_DIM), lambda i: (0, i, 0))


def _gates(h, w, b, cast_weights, tm=1024, tn=1024):
    s, d = h.shape
    n = w.shape[2]
    ni = s // tm
    nsteps = (n // tn) * ni
    cast_in, cast_out, cast_shapes = [], [], []
    for cw in cast_weights:
        _, rows, cols = cw.shape
        slab = rows // nsteps
        assert slab * nsteps == rows and slab % 16 == 0
        cast_in.append(pl.BlockSpec((None, slab, cols), lambda j, i: (0, j * ni + i, 0)))
        cast_out.append(pl.BlockSpec((slab, cols), lambda j, i: (j * ni + i, 0)))
        cast_shapes.append(jax.ShapeDtypeStruct((rows, cols), _bf16))
    outs = pl.pallas_call(
        functools.partial(_proj_gates_kernel, n_cast=len(cast_weights)),
        grid=(n // tn, ni),
        in_specs=[pl.BlockSpec((tm, d), lambda j, i: (i, 0)),
                  pl.BlockSpec((None, d, tn), lambda j, i: (0, 0, j)),
                  pl.BlockSpec((1, tn), lambda j, i: (0, j))] + cast_in,
        out_specs=[pl.BlockSpec((tm, tn), lambda j, i: (i, j))] + cast_out,
        out_shape=[jax.ShapeDtypeStruct((s, n), _bf16)] + cast_shapes,
        scratch_shapes=[pltpu.VMEM((d, tn), _bf16)],
        compiler_params=_cparams(("arbitrary", "arbitrary")),
        name="proj_gates",
    )(h, w, b.reshape(1, n), *cast_weights)
    return outs[0], outs[1:]


F_ROWS = 16
F_CHUNK = 256


def _split3(x):
    hi = x.astype(_bf16)
    r1 = x - hi.astype(_f32)
    mid = r1.astype(_bf16)
    lo = (r1 - mid.astype(_f32)).astype(_bf16)
    return hi, mid, lo


def _select_rows_or_lanes(index, values):
    out = jnp.zeros(index.shape, _f32)
    for i in reversed(range(len(values))):
        out = jnp.where(index == i, values[i].astype(_f32), out)
    return out


def _forget_kernel(wt_ref, a_ref, b_ref, ft_ref, kx_ref, carry_ref):
    i = pl.program_id(0)

    @pl.when(i == 0)
    def _():
        carry_ref[...] = jnp.zeros_like(carry_ref)

    wt8 = wt_ref[...]
    wt = jnp.concatenate([wt8, jnp.zeros_like(wt8)], axis=0).astype(_bf16)
    z = lax.dot_general(wt, a_ref[...], (((1,), (1,)), ((), ())),
                        preferred_element_type=_f32) + b_ref[...]
    logf = (jnp.minimum(z, 0.0) - jnp.log(1.0 + jnp.exp(-jnp.abs(z)))) * LOG2E
    tm = logf.shape[1]
    r = lax.broadcasted_iota(jnp.int32, (F_CHUNK, F_CHUNK), 0)
    c = lax.broadcasted_iota(jnp.int32, (F_CHUNK, F_CHUNK), 1)
    upper = jnp.where(r <= c, 1.0, 0.0).astype(_bf16)
    carry = carry_ref[...]
    lane = lax.broadcasted_iota(jnp.int32, (HEAD_DIM, HEAD_DIM), 1)
    for ch in range(tm // F_CHUNK):
        x = logf[:, ch * F_CHUNK:(ch + 1) * F_CHUNK]
        hi, mid, lo = _split3(x)
        pre = _dot(hi, upper) + _dot(mid, upper) + _dot(lo, upper) + carry
        ft_ref[:, ch * F_CHUNK:(ch + 1) * F_CHUNK] = pre
        carry = pre[:, F_CHUNK - 1:F_CHUNK]
        for hh in range(N_HEADS):
            for sub in range(F_CHUNK // HEAD_DIM):
                row = pre[hh:hh + 1, sub * HEAD_DIM:(sub + 1) * HEAD_DIM]
                neg = -jnp.broadcast_to(row, (HEAD_DIM, HEAD_DIM)).T
                base = ch * F_CHUNK + sub * HEAD_DIM
                kx_ref[hh, base:base + HEAD_DIM, :] = _select_rows_or_lanes(
                    lane, _split3(neg)).astype(kx_ref.dtype)
    carry_ref[...] = carry


def _forget(h, w_in_t, b_col, tm=1024):
    s, d = h.shape
    f_block = 6 * WIDTH // N_HEADS
    return pl.pallas_call(
        _forget_kernel,
        grid=(s // tm,),
        in_specs=[pl.BlockSpec((None, N_HEADS, d), lambda i: (0, f_block, 0)),
                  pl.BlockSpec((tm, d), lambda i: (i, 0)),
                  pl.BlockSpec((F_ROWS, 1), lambda i: (0, 0))],
        out_specs=[pl.BlockSpec((F_ROWS, tm), lambda i: (0, i)),
                   pl.BlockSpec((N_HEADS, tm, HEAD_DIM), lambda i: (0, i, 0))],
        out_shape=[jax.ShapeDtypeStruct((F_ROWS, s), _f32),
                   jax.ShapeDtypeStruct((N_HEADS, s, HEAD_DIM), _bf16)],
        scratch_shapes=[pltpu.VMEM((F_ROWS, 1), _f32)],
        compiler_params=_cparams(("arbitrary",)),
        name="forget_gate",
    )(w_in_t, h, b_col)


def _block_partial(z, vt):
    mloc = jnp.max(z, axis=0, keepdims=True)
    o = _dot(vt, jnp.exp2(z - mloc).astype(_bf16))
    return mloc, o[HEAD_DIM:HEAD_DIM + 1], o[:HEAD_DIM]


def _merge_partials(m_ref, l_ref, acc_ref, idx, c, parts):
    cs = slice(c * TQS, (c + 1) * TQS)
    m_old = m_ref[idx, :, cs]
    m_new = m_old
    for mt, _, _ in parts:
        m_new = jnp.maximum(m_new, mt)
    a = jnp.exp2(m_old - m_new)
    l_new = a * l_ref[idx, :, cs]
    acc_new = a * acc_ref[idx, :, cs]
    for mt, l, o in parts:
        b = jnp.exp2(mt - m_new)
        l_new = l_new + b * l
        acc_new = acc_new + b * o
    m_ref[idx, :, cs] = m_new
    l_ref[idx, :, cs] = l_new
    acc_ref[idx, :, cs] = acc_new


def _init_state(m_ref, l_ref, acc_ref):
    m_ref[...] = jnp.full_like(m_ref, NEG_BIG)
    l_ref[...] = jnp.zeros_like(l_ref)
    acc_ref[...] = jnp.zeros_like(acc_ref)


def _pipelined_key_sweep(qi, scores, main_block, band_block, sa_ref, sb_ref):
    def fill(refs, block):
        for ref, val in zip(refs, scores(block)):
            ref[...] = val

    fill(sa_ref, 0)

    def pair(t, carry):
        fill(sb_ref, 2 * t + 1)
        main_block(2 * t, sa_ref)
        fill(sa_ref, 2 * t + 2)
        main_block(2 * t + 1, sb_ref)
        return carry

    lax.fori_loop(0, lax.shift_right_logical(qi, 1), pair, 0)
    odd = lax.rem(qi, 2) == 1

    @pl.when(odd)
    def _():
        fill(sb_ref, qi)
        main_block(qi - 1, sa_ref)
        band_block(sb_ref)

    @pl.when(jnp.logical_not(odd))
    def _():
        band_block(sa_ref)


def _local_iotas():
    lane = lax.broadcasted_iota(jnp.int32, (TK, TQS), 1)
    sub = lax.broadcasted_iota(jnp.int32, (TK, TQS), 0)
    return lane, sub


def _diff_attn_kernel(slope_ref, q1_ref, q2_ref, k_ref, vt_ref, lq1_ref, lk1_ref, lq2_ref, lk2_ref,
                      gsub_ref, o_ref, m_ref, l_ref, acc_ref, sa0_ref, sa1_ref, sb0_ref, sb1_ref):
    h = pl.program_id(0)
    qi = pl.program_id(1)
    neg_slope = -slope_ref[h] * LOG2E
    _init_state(m_ref, l_ref, acc_ref)
    qs = (q1_ref[0], q2_ref[0])

    slope2 = -neg_slope
    krow = lax.broadcasted_iota(jnp.int32, (TKM, HEAD_DIM), 0)
    klane = lax.broadcasted_iota(jnp.int32, (TKM, HEAD_DIM), 1)
    kx = jnp.where(klane < 3, (krow // HEAD_DIM).astype(_f32),
                   jnp.where(klane < 6, (krow % HEAD_DIM).astype(_f32), 0.0)).astype(_bf16)
    qrow = lax.broadcasted_iota(jnp.int32, (HEAD_DIM, TQ), 0)
    ones = jnp.ones((HEAD_DIM, TQ), _f32)
    qx = _select_rows_or_lanes(qrow, _split3(ones * (slope2 * HEAD_DIM)) + _split3(ones * slope2)).astype(_bf16)
    qs_aug = [jnp.concatenate([q, qx], axis=0) for q in qs]
    t_loc = lax.broadcasted_iota(jnp.int32, (1, TQS), 1).astype(_f32)

    def scores(block):
        start = pl.multiple_of(block * TKM, TKM)
        k_aug = jnp.concatenate([k_ref[0, pl.ds(start, TKM), :], kx], axis=1)
        return tuple(_dot(k_aug, q) for q in qs_aug)

    def query_shift(block, c):
        return (((qi - block) * TQ + c * TQS).astype(_f32) + t_loc) * neg_slope

    def main_block(block, s_refs):
        vt = vt_ref[0, block]
        for idx in range(2):
            for c in range(NSUB):
                mloc, l, o = _block_partial(s_refs[idx][:, c * TQS:(c + 1) * TQS], vt)
                _merge_partials(m_ref, l_ref, acc_ref, idx, c, [(mloc + query_shift(block, c), l, o)])

    def band_block(s_refs):
        lane, sub = _local_iotas()
        delta = (lane - sub).astype(_f32)
        fix_diag = jnp.where((sub // 64) <= (lane // 64),
                             (sub.astype(_f32) + jnp.abs(delta)) * neg_slope, NEG_BIG)
        parts = [[[] for _ in range(NSUB)] for _ in range(2)]
        vt_band = vt_ref[0, qi]
        for j in range(NSUB):
            vt = vt_band[:, j * TK:(j + 1) * TK]
            for idx in range(2):
                for c in range(j, NSUB):
                    z = s_refs[idx][j * TK:(j + 1) * TK, c * TQS:(c + 1) * TQS]
                    if c == j:
                        mloc, l, o = _block_partial(z + fix_diag, vt)
                        parts[idx][c].append((mloc + (j * TK) * neg_slope, l, o))
                    else:
                        mloc, l, o = _block_partial(z, vt)
                        parts[idx][c].append((mloc + query_shift(qi, c), l, o))
        for idx in range(2):
            for c in range(NSUB):
                _merge_partials(m_ref, l_ref, acc_ref, idx, c, parts[idx][c])

    _pipelined_key_sweep(qi, scores, main_block, band_block, (sa0_ref, sa1_ref), (sb0_ref, sb1_ref))

    lam =(jnp.exp(jnp.sum(lq1_ref[...] * lk1_ref[...], axis=-1, keepdims=True))
           - jnp.exp(jnp.sum(lq2_ref[...] * lk2_ref[...], axis=-1, keepdims=True))
           + LAMBDA_INIT)
    o = acc_ref[0] / l_ref[0] - lam * (acc_ref[1] / l_ref[1])
    r = lax.rsqrt(jnp.mean(o * o, axis=0, keepdims=True) + EPS)
    y = (o * r) * gsub_ref[...] * (1.0 - LAMBDA_INIT)
    o_ref[...] = y.T.astype(o_ref.dtype)


def _diff_attention(slopes, q1t, q2t, k, vt, lq1, lk1, lq2, lk2, gsub_col):
    nq = SEQ // TQ
    nkv = SEQ // TKM
    vec = lambda: pl.BlockSpec((1, QK_DIFF), lambda h, q: (0, 0))
    return pl.pallas_call(
        _diff_attn_kernel,
        grid=(N_HEADS, nq),
        in_specs=[pl.BlockSpec(memory_space=pltpu.SMEM),
                  pl.BlockSpec((1, HEAD_DIM, TQ), lambda h, q: (h, 0, q)),
                  pl.BlockSpec((1, HEAD_DIM, TQ), lambda h, q: (h, 0, q)),
                  pl.BlockSpec((1, SEQ, HEAD_DIM), lambda h, q: (h, 0, 0)),
                  pl.BlockSpec((1, nkv, V_ROWS, TKM), lambda h, q: (h, 0, 0, 0)),
                  vec(), vec(), vec(), vec(),
                  pl.BlockSpec((HEAD_DIM, 1), lambda h, q: (0, 0))],
        out_specs=pl.BlockSpec((TQ, HEAD_DIM), lambda h, q: (q, h)),
        out_shape=jax.ShapeDtypeStruct((SEQ, WIDTH), _bf16),
        scratch_shapes=[pltpu.VMEM((2, 1, TQ), _f32),
                        pltpu.VMEM((2, 1, TQ), _f32),
                        pltpu.VMEM((2, HEAD_DIM, TQ), _f32)] + [pltpu.VMEM((TKM, TQ), _f32)] * 4,
        compiler_params=_cparams(("parallel", "parallel")),
        name="diff_attention",
    )(slopes, q1t, q2t, k, vt, lq1, lk1, lq2, lk2, gsub_col)


def _fox_attn_kernel(q_ref, k_ref, vt_ref, ft_ref, kx_ref, o_ref, m_ref, l_ref, acc_ref, sa_ref, sb_ref):
    qi = pl.program_id(1)
    _init_state(m_ref, l_ref, acc_ref)
    q = q_ref[0]
    ft = ft_ref[0]
    qrow = lax.broadcasted_iota(jnp.int32, (HEAD_DIM, TQ), 0)
    q_aug = jnp.concatenate([q, jnp.where(qrow < 3, 1.0, 0.0).astype(_bf16)], axis=0)

    def scores(block):
        start = pl.multiple_of(block * TKM, TKM)
        k_aug = jnp.concatenate([k_ref[0, pl.ds(start, TKM), :], kx_ref[0, pl.ds(start, TKM), :]], axis=1)
        return (_dot(k_aug, q_aug),)

    def main_block(block, s_ref):
        vt = vt_ref[0, block]
        for c in range(NSUB):
            cs = slice(c * TQS, (c + 1) * TQS)
            mloc, l, o = _block_partial(s_ref[0][:, cs], vt)
            _merge_partials(m_ref, l_ref, acc_ref, 0, c, [(mloc + ft[:, cs], l, o)])

    def band_block(s_ref):
        lane, sub = _local_iotas()
        causal = sub <= lane
        parts = [[] for _ in range(NSUB)]
        vt_band = vt_ref[0, qi]
        for j in range(NSUB):
            vt = vt_band[:, j * TK:(j + 1) * TK]
            for c in range(j, NSUB):
                z = s_ref[0][j * TK:(j + 1) * TK, c * TQS:(c + 1) * TQS]
                if c == j:
                    z = jnp.where(causal, z, NEG_BIG)
                mloc, l, o = _block_partial(z, vt)
                parts[c].append((mloc + ft[:, c * TQS:(c + 1) * TQS], l, o))
        for c in range(NSUB):
            _merge_partials(m_ref, l_ref, acc_ref, 0, c, parts[c])

    _pipelined_key_sweep(qi, scores, main_block, band_block, (sa_ref,), (sb_ref,))

    o = acc_ref[0] / l_ref[0]
    o_ref[...] = o.T.astype(o_ref.dtype)


def _fox_attention(qt, k, vt, ft3, kx):
    nq = SEQ // TQ
    nkv = SEQ // TKM
    return pl.pallas_call(
        _fox_attn_kernel,
        grid=(N_HEADS, nq),
        in_specs=[pl.BlockSpec((1, HEAD_DIM, TQ), lambda h, q: (h, 0, q)),
                  pl.BlockSpec((1, SEQ, HEAD_DIM), lambda h, q: (h, 0, 0)),
                  pl.BlockSpec((1, nkv, V_ROWS, TKM), lambda h, q: (h, 0, 0, 0)),
                  pl.BlockSpec((1, 1, TQ), lambda h, q: (h, 0, q)),
                  pl.BlockSpec((1, SEQ, HEAD_DIM), lambda h, q: (h, 0, 0))],
        out_specs=pl.BlockSpec((TQ, HEAD_DIM), lambda h, q: (q, h)),
        out_shape=jax.ShapeDtypeStruct((SEQ, WIDTH), _bf16),
        scratch_shapes=[pltpu.VMEM((1, 1, TQ), _f32),
                        pltpu.VMEM((1, 1, TQ), _f32),
                        pltpu.VMEM((1, HEAD_DIM, TQ), _f32),
                        pltpu.VMEM((TKM, TQ), _f32),
                        pltpu.VMEM((TKM, TQ), _f32)],
        compiler_params=_cparams(("parallel", "parallel")),
        name="fox_attention",
    )(qt, k, vt, ft3, kx)


def _merge_kernel(oa_ref, ob_ref, g_ref, x_ref, wbd_ref, wbf_ref, wout_ref, gm_ref, x1_ref, h2_ref):
    a = _dot(oa_ref[...], wbd_ref[...])
    b = _dot(ob_ref[...], wbf_ref[...])
    g = g_ref[...].astype(_f32)
    merged = g[:, :D_MODEL] * a + g[:, D_MODEL:] * b
    x1 = x_ref[...] + _dot(merged.astype(_bf16), wout_ref[...])
    x1_ref[...] = x1
    r = lax.rsqrt(jnp.mean(x1 * x1, axis=-1, keepdims=True) + EPS)
    h2_ref[...] = ((x1 * r) * gm_ref[...]).astype(h2_ref.dtype)


def _merge(oa, ob, gates, x, wbd, wbf, wout, gm, tm=256):
    s, d = x.shape
    const = lambda shape: pl.BlockSpec(shape, lambda i: (0, 0), pipeline_mode=pl.Buffered(1))
    return pl.pallas_call(
        _merge_kernel,
        grid=(s // tm,),
        in_specs=[pl.BlockSpec((tm, WIDTH), lambda i: (i, 0)),
                  pl.BlockSpec((tm, WIDTH), lambda i: (i, 0)),
                  pl.BlockSpec((tm, 2 * d), lambda i: (i, 0)),
                  pl.BlockSpec((tm, d), lambda i: (i, 0)),
                  const((WIDTH, d)), const((WIDTH, d)), const((d, d)),
                  pl.BlockSpec((1, d), lambda i: (0, 0))],
        out_specs=[pl.BlockSpec((tm, d), lambda i: (i, 0)),
                   pl.BlockSpec((tm, d), lambda i: (i, 0))],
        out_shape=[jax.ShapeDtypeStruct((s, d), _f32),
                   jax.ShapeDtypeStruct((s, d), _bf16)],
        compiler_params=_cparams(("parallel",)),
        name="merge_out_proj",
    )(oa, ob, gates, x, wbd, wbf, wout, gm.reshape(1, d))


def _mlp_kernel(h_ref, wu_ref, wd_ref, x_ref, o_ref):
    k = pl.program_id(1)

    @pl.when(k == 0)
    def _():
        o_ref[...] = x_ref[...]

    u = jnp.maximum(_dot(h_ref[...], wu_ref[...]), 0.0)
    o_ref[...] += _dot((u * u).astype(_bf16), wd_ref[...])


def _mlp(h2, wu, wd, x1, tm=512, tf=1024):
    s, d = x1.shape
    f = wu.shape[1]
    return pl.pallas_call(
        _mlp_kernel,
        grid=(s // tm, f // tf),
        in_specs=[pl.BlockSpec((tm, d), lambda i, k: (i, 0)),
                  pl.BlockSpec((d, tf), lambda i, k: (0, k)),
                  pl.BlockSpec((tf, d), lambda i, k: (k, 0)),
                  pl.BlockSpec((tm, d), lambda i, k: (i, 0))],
        out_specs=pl.BlockSpec((tm, d), lambda i, k: (i, 0)),
        out_shape=jax.ShapeDtypeStruct((s, d), _f32),
        compiler_params=_cparams(("parallel", "arbitrary")),
        name="mlp_relu2",
    )(h2, wu, wd, x1)


def kernel(x, norm_mix, w_in, b_forget, qnorm_diff, knorm_diff, lambda_q1, lambda_k1, lambda_q2, lambda_k2,
           subln_diff, qnorm_fox, knorm_fox, w_branch_diff, w_branch_fox, w_gate, b_gate, w_out, norm_mlp,
           w_mlp_up, w_mlp_down):
    assert x.shape == (1, SEQ, D_MODEL)
    x2 = x[0]
    w_in_t = jnp.swapaxes(w_in, 1, 2)

    h = _rmsnorm(x2, norm_mix[0])

    g_qd = jnp.tile(qnorm_diff[0], 2).reshape(1, HEAD_DIM)
    g_kd = jnp.tile(knorm_diff[0], 2).reshape(1, HEAD_DIM)
    tm = TKM
    t_shape = jax.ShapeDtypeStruct((N_HEADS, HEAD_DIM, SEQ), _bf16)
    k_shape = jax.ShapeDtypeStruct((N_HEADS, SEQ, HEAD_DIM), _bf16)
    vt_shape = jax.ShapeDtypeStruct((N_HEADS, SEQ // TKM, V_ROWS, TKM), _bf16)
    vt_spec = pl.BlockSpec((N_HEADS, 1, V_ROWS, TKM), lambda i: (0, i, 0, 0))

    q1t, q2t = _proj_call(_proj_qdiff_kernel, h, w_in_t, 0, [g_qd], [t_shape, t_shape],
                          [_t_spec(tm), _t_spec(tm)], "proj_q_diff")
    ka = _proj_call(functools.partial(_proj_k_kernel, groups=2), h, w_in_t, 1, [g_kd], k_shape,
                    _k_spec(tm), "proj_k_diff")
    vat = _proj_call(_proj_vt_kernel, h, w_in_t, 2, [], vt_shape, vt_spec, "proj_v_diff")
    qbt = _proj_call(_proj_qfox_kernel, h, w_in_t, 3, [qnorm_fox[0].reshape(1, HEAD_DIM)], t_shape,
                     _t_spec(tm), "proj_q_fox")
    kb = _proj_call(functools.partial(_proj_k_kernel, groups=1), h, w_in_t, 4,
                    [knorm_fox[0].reshape(1, HEAD_DIM)], k_shape, _k_spec(tm), "proj_k_fox")
    vbt = _proj_call(_proj_vt_kernel, h, w_in_t, 5, [], vt_shape, vt_spec, "proj_v_fox")

    bf_col = jnp.zeros((F_ROWS, 1), _f32).at[:N_HEADS, 0].set(b_forget[0])
    ft, kx_fox = _forget(h, w_in_t, bf_col)
    ft3 = ft.reshape(F_ROWS, 1, SEQ)

    gates, (wbd, wbf, wout, wup, wdown) = _gates(
        h, w_gate, b_gate[0], [w_branch_diff, w_branch_fox, w_out, w_mlp_up, w_mlp_down])

    slopes = 2.0 ** (-8.0 * jnp.arange(1, N_HEADS + 1, dtype=_f32) / N_HEADS)
    row = lambda v: v[0].reshape(1, QK_DIFF)
    oa = _diff_attention(slopes, q1t, q2t, ka, vat, row(lambda_q1), row(lambda_k1), row(lambda_q2),
                         row(lambda_k2), subln_diff[0].reshape(HEAD_DIM, 1))
    ob = _fox_attention(qbt, kb, vbt, ft3, kx_fox)

    x1, h2 = _merge(oa, ob, gates, x2, wbd, wbf, wout, norm_mlp[0])
    out = _mlp(h2, wup, wdown, x1)
    return out[None]
```

```python
import functools

import jax
import jax.numpy as jnp
from jax import lax
from jax.experimental import pallas as pl
from jax.experimental.pallas import tpu as pltpu

D_MODEL = 2048
SEQ = 8192
HEAD_DIM = 128
N_HEADS = 8
QK_DIFF = 64
WIDTH = N_HEADS * HEAD_DIM
D_FF = 4 * D_MODEL
EPS = 1e-6
LAMBDA_INIT = 0.8 - 0.6 * 1.0
NEG_BIG = -1e30

TQ = 1024
TQS = 256
TK = 256
TKM = TQ
NSUB = TQ // TQS
LOG2E = 1.4426950408889634
V_ROWS = HEAD_DIM + 16
VMEM_LIMIT = 56 * 1024 * 1024

_f32 = jnp.float32
_bf16 = jnp.bfloat16


def _cparams(sem):
    return pltpu.CompilerParams(dimension_semantics=sem, vmem_limit_bytes=VMEM_LIMIT)


def _dot(a, b):
    return jnp.dot(a, b, preferred_element_type=_f32)


def _cast_slab_specs(weights, nsteps, step_of):
    in_specs, out_specs, out_shapes = [], [], []
    for w in weights:
        _, rows, cols = w.shape
        slab = rows // nsteps
        assert slab * nsteps == rows and slab % 16 == 0
        in_specs.append(pl.BlockSpec((None, slab, cols), lambda *ids: (0, step_of(*ids), 0)))
        out_specs.append(pl.BlockSpec((slab, cols), lambda *ids: (step_of(*ids), 0)))
        out_shapes.append(jax.ShapeDtypeStruct((rows, cols), _bf16))
    return in_specs, out_specs, out_shapes


def _cast_slabs(src_refs, dst_refs):
    for src, dst in zip(src_refs, dst_refs):
        dst[...] = src[...].astype(dst.dtype)


def _head_rmsnorm(y, gain_row, groups):
    y2 = y * y
    if groups == 1:
        r = lax.rsqrt(jnp.mean(y2, axis=-1, keepdims=True) + EPS)
    else:
        lane = lax.broadcasted_iota(jnp.int32, y.shape, 1)
        lo = lane < QK_DIFF
        s_lo = jnp.sum(jnp.where(lo, y2, 0.0), axis=-1, keepdims=True)
        s_hi = jnp.sum(jnp.where(lo, 0.0, y2), axis=-1, keepdims=True)
        r = jnp.where(lo, lax.rsqrt(s_lo * (1.0 / QK_DIFF) + EPS),
                      lax.rsqrt(s_hi * (1.0 / QK_DIFF) + EPS))
    return (y * r) * gain_row


PROJ_SUB = 256


def _proj_subtiles(a_ref, w_ref, wb_ref, first_step, epilogue, w_transposed=False):
    @pl.when(first_step)
    def _():
        w = w_ref[...].T if w_transposed else w_ref[...]
        wb_ref[...] = w.astype(wb_ref.dtype)

    for sub in range(a_ref.shape[0] // PROJ_SUB):
        rows = slice(sub * PROJ_SUB, (sub + 1) * PROJ_SUB)
        epilogue(rows, _dot(a_ref[rows, :], wb_ref[...]))


def _heads(acc):
    return [acc[:, hh * HEAD_DIM:(hh + 1) * HEAD_DIM] for hh in range(N_HEADS)]


def _proj_qdiff_kernel(a_ref, w_ref, g_ref, q1_ref, q2_ref, wb_ref):
    def epilogue(rows, acc):
        lo = lax.broadcasted_iota(jnp.int32, (PROJ_SUB, HEAD_DIM), 1) < QK_DIFF
        for hh, y in enumerate(_heads(acc)):
            yn = _head_rmsnorm(y, g_ref[...], 2) * (QK_DIFF ** -0.5 * LOG2E)
            q1_ref[hh, :, rows] = jnp.where(lo, yn, 0.0).T.astype(q1_ref.dtype)
            q2_ref[hh, :, rows] = jnp.where(lo, 0.0, yn).T.astype(q2_ref.dtype)

    _proj_subtiles(a_ref, w_ref, wb_ref, pl.program_id(0) == 0, epilogue, w_transposed=True)


def _proj_qfox_kernel(a_ref, w_ref, g_ref, q_ref, wb_ref):
    def epilogue(rows, acc):
        for hh, y in enumerate(_heads(acc)):
            yn = _head_rmsnorm(y, g_ref[...], 1) * (HEAD_DIM ** -0.5 * LOG2E)
            q_ref[hh, :, rows] = yn.T.astype(q_ref.dtype)

    _proj_subtiles(a_ref, w_ref, wb_ref, pl.program_id(0) == 0, epilogue, w_transposed=True)


def _proj_k_kernel(a_ref, w_ref, g_ref, k_ref, wb_ref, *, groups):
    def epilogue(rows, acc):
        for hh, y in enumerate(_heads(acc)):
            k_ref[hh, rows, :] = _head_rmsnorm(y, g_ref[...], groups).astype(k_ref.dtype)

    _proj_subtiles(a_ref, w_ref, wb_ref, pl.program_id(0) == 0, epilogue, w_transposed=True)


def _proj_vt_kernel(a_ref, w_ref, vt_ref, wb_ref):
    row = lax.broadcasted_iota(jnp.int32, (V_ROWS - HEAD_DIM, TKM), 0)
    ones_rows = jnp.where(row == 0, 1.0, 0.0).astype(vt_ref.dtype)
    for hh in range(N_HEADS):
        vt_ref[hh, 0, HEAD_DIM:, :] = ones_rows

    def epilogue(rows, acc):
        for hh, y in enumerate(_heads(acc)):
            vt_ref[hh, 0, :HEAD_DIM, rows] = y.T.astype(vt_ref.dtype)

    _proj_subtiles(a_ref, w_ref, wb_ref, pl.program_id(0) == 0, epilogue, w_transposed=True)


def _proj_gates_kernel(a_ref, w_ref, b_ref, o_ref, wb_ref):
    def epilogue(rows, acc):
        o_ref[rows, :] = (1.0 / (1.0 + jnp.exp(-(acc + b_ref[...])))).astype(o_ref.dtype)

    _proj_subtiles(a_ref, w_ref, wb_ref, pl.program_id(1) == 0, epilogue)


def _proj_call(kernel, h, w, col_block, extras, out_shapes, out_specs, name, tm=TKM):
    s, d = h.shape
    in_specs = [pl.BlockSpec((tm, d), lambda i: (i, 0)),
                pl.BlockSpec((None, WIDTH, d), lambda i: (0, col_block, 0), pipeline_mode=pl.Buffered(1))]
    in_specs += [pl.BlockSpec(e.shape, lambda i: (0, 0)) for e in extras]
    return pl.pallas_call(
        kernel,
        grid=(s // tm,),
        in_specs=in_specs,
        out_specs=out_specs,
        out_shape=out_shapes,
        scratch_shapes=[pltpu.VMEM((d, WIDTH), _bf16)],
        compiler_params=_cparams(("arbitrary",)),
        name=name,
    )(h, w, *extras)


def _t_spec(tm):
    return pl.BlockSpec((N_HEADS, HEAD_DIM, tm), lambda i: (0, 0, i))


def _k_spec(tm):
    return pl.BlockSpec((N_HEADS, tm, HEAD_DIM), lambda i: (0, i, 0))


def _gates(h, w, b, tm=1024, tn=1024):
    s, d = h.shape
    n = w.shape[2]
    return pl.pallas_call(
        _proj_gates_kernel,
        grid=(n // tn, s // tm),
        in_specs=[pl.BlockSpec((tm, d), lambda j, i: (i, 0)),
                  pl.BlockSpec((None, d, tn), lambda j, i: (0, 0, j)),
                  pl.BlockSpec((1, tn), lambda j, i: (0, j))],
        out_specs=pl.BlockSpec((tm, tn), lambda j, i: (i, j)),
        out_shape=jax.ShapeDtypeStruct((s, n), _bf16),
        scratch_shapes=[pltpu.VMEM((d, tn), _bf16)],
        compiler_params=_cparams(("arbitrary", "arbitrary")),
        name="proj_gates",
    )(h, w, b.reshape(1, n))


F_ROWS = 16
F_CHUNK = 256


def _split3(x):
    hi = x.astype(_bf16)
    r1 = x - hi.astype(_f32)
    mid = r1.astype(_bf16)
    lo = (r1 - mid.astype(_f32)).astype(_bf16)
    return hi, mid, lo


def _select_rows_or_lanes(index, values):
    out = jnp.zeros(index.shape, _f32)
    for i in reversed(range(len(values))):
        out = jnp.where(index == i, values[i].astype(_f32), out)
    return out


def _forget_kernel(wt_ref, x_ref, g_ref, b_ref, h_ref, ft_ref, kx_ref, carry_ref):
    i = pl.program_id(0)

    @pl.when(i == 0)
    def _():
        carry_ref[...] = jnp.zeros_like(carry_ref)

    x = x_ref[...]
    r = lax.rsqrt(jnp.mean(x * x, axis=-1, keepdims=True) + EPS)
    h = ((x * r) * g_ref[...]).astype(h_ref.dtype)
    h_ref[...] = h
    wt8 = wt_ref[...]
    wt = jnp.concatenate([wt8, jnp.zeros_like(wt8)], axis=0).astype(_bf16)
    z = lax.dot_general(wt, h, (((1,), (1,)), ((), ())),
                        preferred_element_type=_f32) + b_ref[...]
    logf = (jnp.minimum(z, 0.0) - jnp.log(1.0 + jnp.exp(-jnp.abs(z)))) * LOG2E
    tm = logf.shape[1]
    r = lax.broadcasted_iota(jnp.int32, (F_CHUNK, F_CHUNK), 0)
    c = lax.broadcasted_iota(jnp.int32, (F_CHUNK, F_CHUNK), 1)
    upper = jnp.where(r <= c, 1.0, 0.0).astype(_bf16)
    carry = carry_ref[...]
    lane = lax.broadcasted_iota(jnp.int32, (HEAD_DIM, HEAD_DIM), 1)
    for ch in range(tm // F_CHUNK):
        x = logf[:, ch * F_CHUNK:(ch + 1) * F_CHUNK]
        hi, mid, lo = _split3(x)
        pre = _dot(hi, upper) + _dot(mid, upper) + _dot(lo, upper) + carry
        ft_ref[:, ch * F_CHUNK:(ch + 1) * F_CHUNK] = pre
        carry = pre[:, F_CHUNK - 1:F_CHUNK]
        for hh in range(N_HEADS):
            for sub in range(F_CHUNK // HEAD_DIM):
                row = pre[hh:hh + 1, sub * HEAD_DIM:(sub + 1) * HEAD_DIM]
                neg = -jnp.broadcast_to(row, (HEAD_DIM, HEAD_DIM)).T
                base = ch * F_CHUNK + sub * HEAD_DIM
                kx_ref[hh, base:base + HEAD_DIM, :] = _select_rows_or_lanes(
                    lane, _split3(neg)).astype(kx_ref.dtype)
    carry_ref[...] = carry


def _norm_and_forget(x, g, w_in_t, b_col, tm=1024):
    s, d = x.shape
    f_block = 6 * WIDTH // N_HEADS
    return pl.pallas_call(
        _forget_kernel,
        grid=(s // tm,),
        in_specs=[pl.BlockSpec((None, N_HEADS, d), lambda i: (0, f_block, 0)),
                  pl.BlockSpec((tm, d), lambda i: (i, 0)),
                  pl.BlockSpec((1, d), lambda i: (0, 0)),
                  pl.BlockSpec((F_ROWS, 1), lambda i: (0, 0))],
        out_specs=[pl.BlockSpec((tm, d), lambda i: (i, 0)),
                   pl.BlockSpec((F_ROWS, tm), lambda i: (0, i)),
                   pl.BlockSpec((N_HEADS, tm, HEAD_DIM), lambda i: (0, i, 0))],
        out_shape=[jax.ShapeDtypeStruct((s, d), _bf16),
                   jax.ShapeDtypeStruct((F_ROWS, s), _f32),
                   jax.ShapeDtypeStruct((N_HEADS, s, HEAD_DIM), _bf16)],
        scratch_shapes=[pltpu.VMEM((F_ROWS, 1), _f32)],
        compiler_params=_cparams(("arbitrary",)),
        name="norm_forget_gate",
    )(w_in_t, x, g.reshape(1, d), b_col)


def _block_partial(z, vt):
    mloc = jnp.max(z, axis=0, keepdims=True)
    o = _dot(vt, jnp.exp2(z - mloc).astype(_bf16))
    return mloc, o[HEAD_DIM:HEAD_DIM + 1], o[:HEAD_DIM]


def _merge_partials(m_ref, l_ref, acc_ref, idx, c, parts):
    cs = slice(c * TQS, (c + 1) * TQS)
    m_old = m_ref[idx, :, cs]
    m_new = m_old
    for mt, _, _ in parts:
        m_new = jnp.maximum(m_new, mt)
    a = jnp.exp2(m_old - m_new)
    l_new = a * l_ref[idx, :, cs]
    acc_new = a * acc_ref[idx, :, cs]
    for mt, l, o in parts:
        b = jnp.exp2(mt - m_new)
        l_new = l_new + b * l
        acc_new = acc_new + b * o
    m_ref[idx, :, cs] = m_new
    l_ref[idx, :, cs] = l_new
    acc_ref[idx, :, cs] = acc_new


def _init_state(m_ref, l_ref, acc_ref):
    m_ref[...] = jnp.full_like(m_ref, NEG_BIG)
    l_ref[...] = jnp.zeros_like(l_ref)
    acc_ref[...] = jnp.zeros_like(acc_ref)


def _pipelined_key_sweep(qi, scores, main_block, band_block, sa_ref, sb_ref):
    def fill(refs, block):
        for ref, val in zip(refs, scores(block)):
            ref[...] = val

    fill(sa_ref, 0)

    def pair(t, carry):
        fill(sb_ref, 2 * t + 1)
        main_block(2 * t, sa_ref)
        fill(sa_ref, 2 * t + 2)
        main_block(2 * t + 1, sb_ref)
        return carry

    lax.fori_loop(0, lax.shift_right_logical(qi, 1), pair, 0)
    odd = lax.rem(qi, 2) == 1

    @pl.when(odd)
    def _():
        fill(sb_ref, qi)
        main_block(qi - 1, sa_ref)
        band_block(sb_ref)

    @pl.when(jnp.logical_not(odd))
    def _():
        band_block(sa_ref)


def _local_iotas():
    lane = lax.broadcasted_iota(jnp.int32, (TK, TQS), 1)
    sub = lax.broadcasted_iota(jnp.int32, (TK, TQS), 0)
    return lane, sub


def _diff_attn_kernel(slope_ref, q1_ref, q2_ref, k_ref, vt_ref, lq1_ref, lk1_ref, lq2_ref, lk2_ref,
                      gsub_ref, o_ref, m_ref, l_ref, acc_ref, sa0_ref, sa1_ref, sb0_ref, sb1_ref):
    h = pl.program_id(0)
    qi = pl.program_id(1)
    neg_slope = -slope_ref[h] * LOG2E
    _init_state(m_ref, l_ref, acc_ref)
    qs = (q1_ref[0], q2_ref[0])

    slope2 = -neg_slope
    krow = lax.broadcasted_iota(jnp.int32, (TKM, HEAD_DIM), 0)
    klane = lax.broadcasted_iota(jnp.int32, (TKM, HEAD_DIM), 1)
    kx = jnp.where(klane < 3, (krow // HEAD_DIM).astype(_f32),
                   jnp.where(klane < 6, (krow % HEAD_DIM).astype(_f32), 0.0)).astype(_bf16)
    qrow = lax.broadcasted_iota(jnp.int32, (HEAD_DIM, TQ), 0)
    ones = jnp.ones((HEAD_DIM, TQ), _f32)
    qx = _select_rows_or_lanes(qrow, _split3(ones * (slope2 * HEAD_DIM)) + _split3(ones * slope2)).astype(_bf16)
    qs_aug = [jnp.concatenate([q, qx], axis=0) for q in qs]
    t_loc = lax.broadcasted_iota(jnp.int32, (1, TQS), 1).astype(_f32)

    def scores(block):
        start = pl.multiple_of(block * TKM, TKM)
        k_aug = jnp.concatenate([k_ref[0, pl.ds(start, TKM), :], kx], axis=1)
        return tuple(_dot(k_aug, q) for q in qs_aug)

    def query_shift(block, c):
        return (((qi - block) * TQ + c * TQS).astype(_f32) + t_loc) * neg_slope

    def main_block(block, s_refs):
        vt = vt_ref[0, block]
        for idx in range(2):
            for c in range(NSUB):
                mloc, l, o = _block_partial(s_refs[idx][:, c * TQS:(c + 1) * TQS], vt)
                _merge_partials(m_ref, l_ref, acc_ref, idx, c, [(mloc + query_shift(block, c), l, o)])

    def band_block(s_refs):
        lane, sub = _local_iotas()
        delta = (lane - sub).astype(_f32)
        fix_diag = jnp.where((sub // 64) <= (lane // 64),
                             (sub.astype(_f32) + jnp.abs(delta)) * neg_slope, NEG_BIG)
        parts = [[[] for _ in range(NSUB)] for _ in range(2)]
        vt_band = vt_ref[0, qi]
        for j in range(NSUB):
            vt = vt_band[:, j * TK:(j + 1) * TK]
            for idx in range(2):
                for c in range(j, NSUB):
                    z = s_refs[idx][j * TK:(j + 1) * TK, c * TQS:(c + 1) * TQS]
                    if c == j:
                        mloc, l, o = _block_partial(z + fix_diag, vt)
                        parts[idx][c].append((mloc + (j * TK) * neg_slope, l, o))
                    else:
                        mloc, l, o = _block_partial(z, vt)
                        parts[idx][c].append((mloc + query_shift(qi, c), l, o))
        for idx in range(2):
            for c in range(NSUB):
                _merge_partials(m_ref, l_ref, acc_ref, idx, c, parts[idx][c])

    _pipelined_key_sweep(qi, scores, main_block, band_block, (sa0_ref, sa1_ref), (sb0_ref, sb1_ref))

    lam =(jnp.exp(jnp.sum(lq1_ref[...] * lk1_ref[...], axis=-1, keepdims=True))
           - jnp.exp(jnp.sum(lq2_ref[...] * lk2_ref[...], axis=-1, keepdims=True))
           + LAMBDA_INIT)
    o = acc_ref[0] / l_ref[0] - lam * (acc_ref[1] / l_ref[1])
    r = lax.rsqrt(jnp.mean(o * o, axis=0, keepdims=True) + EPS)
    y = (o * r) * gsub_ref[...] * (1.0 - LAMBDA_INIT)
    o_ref[...] = y.T.astype(o_ref.dtype)


def _diff_attention(slopes, q1t, q2t, k, vt, lq1, lk1, lq2, lk2, gsub_col):
    nq = SEQ // TQ
    nkv = SEQ // TKM
    vec = lambda: pl.BlockSpec((1, QK_DIFF), lambda h, q: (0, 0))
    return pl.pallas_call(
        _diff_attn_kernel,
        grid=(N_HEADS, nq),
        in_specs=[pl.BlockSpec(memory_space=pltpu.SMEM),
                  pl.BlockSpec((1, HEAD_DIM, TQ), lambda h, q: (h, 0, q)),
                  pl.BlockSpec((1, HEAD_DIM, TQ), lambda h, q: (h, 0, q)),
                  pl.BlockSpec((1, SEQ, HEAD_DIM), lambda h, q: (h, 0, 0)),
                  pl.BlockSpec((1, nkv, V_ROWS, TKM), lambda h, q: (h, 0, 0, 0)),
                  vec(), vec(), vec(), vec(),
                  pl.BlockSpec((HEAD_DIM, 1), lambda h, q: (0, 0))],
        out_specs=pl.BlockSpec((TQ, HEAD_DIM), lambda h, q: (q, h)),
        out_shape=jax.ShapeDtypeStruct((SEQ, WIDTH), _bf16),
        scratch_shapes=[pltpu.VMEM((2, 1, TQ), _f32),
                        pltpu.VMEM((2, 1, TQ), _f32),
                        pltpu.VMEM((2, HEAD_DIM, TQ), _f32)] + [pltpu.VMEM((TKM, TQ), _f32)] * 4,
        compiler_params=_cparams(("parallel", "parallel")),
        name="diff_attention",
    )(slopes, q1t, q2t, k, vt, lq1, lk1, lq2, lk2, gsub_col)


def _fox_attn_kernel(q_ref, k_ref, vt_ref, ft_ref, kx_ref, *refs, n_cast):
    cast_in, o_ref, cast_out = refs[:n_cast], refs[n_cast], refs[n_cast + 1:2 * n_cast + 1]
    m_ref, l_ref, acc_ref, sa_ref, sb_ref = refs[2 * n_cast + 1:]
    qi = pl.program_id(1)
    _cast_slabs(cast_in, cast_out)
    _init_state(m_ref, l_ref, acc_ref)
    q = q_ref[0]
    ft = ft_ref[0]
    qrow = lax.broadcasted_iota(jnp.int32, (HEAD_DIM, TQ), 0)
    q_aug = jnp.concatenate([q, jnp.where(qrow < 3, 1.0, 0.0).astype(_bf16)], axis=0)

    def scores(block):
        start = pl.multiple_of(block * TKM, TKM)
        k_aug = jnp.concatenate([k_ref[0, pl.ds(start, TKM), :], kx_ref[0, pl.ds(start, TKM), :]], axis=1)
        return (_dot(k_aug, q_aug),)

    def main_block(block, s_ref):
        vt = vt_ref[0, block]
        for c in range(NSUB):
            cs = slice(c * TQS, (c + 1) * TQS)
            mloc, l, o = _block_partial(s_ref[0][:, cs], vt)
            _merge_partials(m_ref, l_ref, acc_ref, 0, c, [(mloc + ft[:, cs], l, o)])

    def band_block(s_ref):
        lane, sub = _local_iotas()
        causal = sub <= lane
        parts = [[] for _ in range(NSUB)]
        vt_band = vt_ref[0, qi]
        for j in range(NSUB):
            vt = vt_band[:, j * TK:(j + 1) * TK]
            for c in range(j, NSUB):
                z = s_ref[0][j * TK:(j + 1) * TK, c * TQS:(c + 1) * TQS]
                if c == j:
                    z = jnp.where(causal, z, NEG_BIG)
                mloc, l, o = _block_partial(z, vt)
                parts[c].append((mloc + ft[:, c * TQS:(c + 1) * TQS], l, o))
        for c in range(NSUB):
            _merge_partials(m_ref, l_ref, acc_ref, 0, c, parts[c])

    _pipelined_key_sweep(qi, scores, main_block, band_block, (sa_ref,), (sb_ref,))

    o = acc_ref[0] / l_ref[0]
    o_ref[...] = o.T.astype(o_ref.dtype)


def _fox_attention(qt, k, vt, ft3, kx, cast_weights):
    nq = SEQ // TQ
    nkv = SEQ // TKM
    cast_in, cast_out, cast_shapes = _cast_slab_specs(cast_weights, N_HEADS * nq, lambda h, q: h * nq + q)
    outs = pl.pallas_call(
        functools.partial(_fox_attn_kernel, n_cast=len(cast_weights)),
        grid=(N_HEADS, nq),
        in_specs=[pl.BlockSpec((1, HEAD_DIM, TQ), lambda h, q: (h, 0, q)),
                  pl.BlockSpec((1, SEQ, HEAD_DIM), lambda h, q: (h, 0, 0)),
                  pl.BlockSpec((1, nkv, V_ROWS, TKM), lambda h, q: (h, 0, 0, 0)),
                  pl.BlockSpec((1, 1, TQ), lambda h, q: (h, 0, q)),
                  pl.BlockSpec((1, SEQ, HEAD_DIM), lambda h, q: (h, 0, 0))] + cast_in,
        out_specs=[pl.BlockSpec((TQ, HEAD_DIM), lambda h, q: (q, h))] + cast_out,
        out_shape=[jax.ShapeDtypeStruct((SEQ, WIDTH), _bf16)] + cast_shapes,
        scratch_shapes=[pltpu.VMEM((1, 1, TQ), _f32),
                        pltpu.VMEM((1, 1, TQ), _f32),
                        pltpu.VMEM((1, HEAD_DIM, TQ), _f32),
                        pltpu.VMEM((TKM, TQ), _f32),
                        pltpu.VMEM((TKM, TQ), _f32)],
        compiler_params=_cparams(("parallel", "parallel")),
        name="fox_attention",
    )(qt, k, vt, ft3, kx, *cast_weights)
    return outs[0], outs[1:]


def _merge_kernel(oa_ref, ob_ref, g_ref, x_ref, wbd_ref, wbf_ref, wout_ref, gm_ref, x1_ref, h2_ref):
    a = _dot(oa_ref[...], wbd_ref[...])
    b = _dot(ob_ref[...], wbf_ref[...])
    g = g_ref[...].astype(_f32)
    merged = g[:, :D_MODEL] * a + g[:, D_MODEL:] * b
    x1 = x_ref[...] + _dot(merged.astype(_bf16), wout_ref[...])
    x1_ref[...] = x1
    r = lax.rsqrt(jnp.mean(x1 * x1, axis=-1, keepdims=True) + EPS)
    h2_ref[...] = ((x1 * r) * gm_ref[...]).astype(h2_ref.dtype)


def _merge(oa, ob, gates, x, wbd, wbf, wout, gm, tm=256):
    s, d = x.shape
    const = lambda shape: pl.BlockSpec(shape, lambda i: (0, 0), pipeline_mode=pl.Buffered(1))
    return pl.pallas_call(
        _merge_kernel,
        grid=(s // tm,),
        in_specs=[pl.BlockSpec((tm, WIDTH), lambda i: (i, 0)),
                  pl.BlockSpec((tm, WIDTH), lambda i: (i, 0)),
                  pl.BlockSpec((tm, 2 * d), lambda i: (i, 0)),
                  pl.BlockSpec((tm, d), lambda i: (i, 0)),
                  const((WIDTH, d)), const((WIDTH, d)), const((d, d)),
                  pl.BlockSpec((1, d), lambda i: (0, 0))],
        out_specs=[pl.BlockSpec((tm, d), lambda i: (i, 0)),
                   pl.BlockSpec((tm, d), lambda i: (i, 0))],
        out_shape=[jax.ShapeDtypeStruct((s, d), _f32),
                   jax.ShapeDtypeStruct((s, d), _bf16)],
        compiler_params=_cparams(("parallel",)),
        name="merge_out_proj",
    )(oa, ob, gates, x, wbd, wbf, wout, gm.reshape(1, d))


def _mlp_kernel(h_ref, wu_ref, wd_ref, x_ref, o_ref):
    k = pl.program_id(1)

    @pl.when(k == 0)
    def _():
        o_ref[...] = x_ref[...]

    u = jnp.maximum(_dot(h_ref[...], wu_ref[...]), 0.0)
    o_ref[...] += _dot((u * u).astype(_bf16), wd_ref[...])


def _mlp(h2, wu, wd, x1, tm=512, tf=1024):
    s, d = x1.shape
    f = wu.shape[1]
    return pl.pallas_call(
        _mlp_kernel,
        grid=(s // tm, f // tf),
        in_specs=[pl.BlockSpec((tm, d), lambda i, k: (i, 0)),
                  pl.BlockSpec((d, tf), lambda i, k: (0, k)),
                  pl.BlockSpec((tf, d), lambda i, k: (k, 0)),
                  pl.BlockSpec((tm, d), lambda i, k: (i, 0))],
        out_specs=pl.BlockSpec((tm, d), lambda i, k: (i, 0)),
        out_shape=jax.ShapeDtypeStruct((s, d), _f32),
        compiler_params=_cparams(("parallel", "arbitrary")),
        name="mlp_relu2",
    )(h2, wu, wd, x1)


def kernel(x, norm_mix, w_in, b_forget, qnorm_diff, knorm_diff, lambda_q1, lambda_k1, lambda_q2, lambda_k2,
           subln_diff, qnorm_fox, knorm_fox, w_branch_diff, w_branch_fox, w_gate, b_gate, w_out, norm_mlp,
           w_mlp_up, w_mlp_down):
    assert x.shape == (1, SEQ, D_MODEL)
    x2 = x[0]
    w_in_t = jnp.swapaxes(w_in, 1, 2)

    bf_col = jnp.zeros((F_ROWS, 1), _f32).at[:N_HEADS, 0].set(b_forget[0])
    h, ft, kx_fox = _norm_and_forget(x2, norm_mix[0], w_in_t, bf_col)
    ft3 = ft.reshape(F_ROWS, 1, SEQ)

    g_qd = jnp.tile(qnorm_diff[0], 2).reshape(1, HEAD_DIM)
    g_kd = jnp.tile(knorm_diff[0], 2).reshape(1, HEAD_DIM)
    tm = TKM
    t_shape = jax.ShapeDtypeStruct((N_HEADS, HEAD_DIM, SEQ), _bf16)
    k_shape = jax.ShapeDtypeStruct((N_HEADS, SEQ, HEAD_DIM), _bf16)
    vt_shape = jax.ShapeDtypeStruct((N_HEADS, SEQ // TKM, V_ROWS, TKM), _bf16)
    vt_spec = pl.BlockSpec((N_HEADS, 1, V_ROWS, TKM), lambda i: (0, i, 0, 0))

    q1t, q2t = _proj_call(_proj_qdiff_kernel, h, w_in_t, 0, [g_qd], [t_shape, t_shape],
                          [_t_spec(tm), _t_spec(tm)], "proj_q_diff")
    ka = _proj_call(functools.partial(_proj_k_kernel, groups=2), h, w_in_t, 1, [g_kd], k_shape,
                    _k_spec(tm), "proj_k_diff")
    vat = _proj_call(_proj_vt_kernel, h, w_in_t, 2, [], vt_shape, vt_spec, "proj_v_diff")
    qbt = _proj_call(_proj_qfox_kernel, h, w_in_t, 3, [qnorm_fox[0].reshape(1, HEAD_DIM)], t_shape,
                     _t_spec(tm), "proj_q_fox")
    kb = _proj_call(functools.partial(_proj_k_kernel, groups=1), h, w_in_t, 4,
                    [knorm_fox[0].reshape(1, HEAD_DIM)], k_shape, _k_spec(tm), "proj_k_fox")
    vbt = _proj_call(_proj_vt_kernel, h, w_in_t, 5, [], vt_shape, vt_spec, "proj_v_fox")

    gates = _gates(h, w_gate, b_gate[0])

    slopes = 2.0 ** (-8.0 * jnp.arange(1, N_HEADS + 1, dtype=_f32) / N_HEADS)
    row = lambda v: v[0].reshape(1, QK_DIFF)
    oa = _diff_attention(slopes, q1t, q2t, ka, vat, row(lambda_q1), row(lambda_k1), row(lambda_q2),
                         row(lambda_k2), subln_diff[0].reshape(HEAD_DIM, 1))
    ob, (wbd, wbf, wout, wup, wdown) = _fox_attention(
        qbt, kb, vbt, ft3, kx_fox, [w_branch_diff, w_branch_fox, w_out, w_mlp_up, w_mlp_down])

    x1, h2 = _merge(oa, ob, gates, x2, wbd, wbf, wout, norm_mlp[0])
    out = _mlp(h2, wup, wdown, x1)
    return out[None]
```

```python
import functools

import jax
import jax.numpy as jnp
from jax import lax
from jax.experimental import pallas as pl
from jax.experimental.pallas import tpu as pltpu

D_MODEL = 2048
SEQ = 8192
HEAD_DIM = 128
N_HEADS = 8
QK_DIFF = 64
WIDTH = N_HEADS * HEAD_DIM
D_FF = 4 * D_MODEL
EPS = 1e-6
LAMBDA_INIT = 0.8 - 0.6 * 1.0
NEG_BIG = -1e30

TQ = 1024
TQS = 256
TK = 256
TKM = TQ
NSUB = TQ // TQS
LOG2E = 1.4426950408889634
V_ROWS = HEAD_DIM + 16
VMEM_LIMIT = 56 * 1024 * 1024

_f32 = jnp.float32
_bf16 = jnp.bfloat16


def _cparams(sem):
    return pltpu.CompilerParams(dimension_semantics=sem, vmem_limit_bytes=VMEM_LIMIT)


def _dot(a, b):
    return jnp.dot(a, b, preferred_element_type=_f32)


def _cast_slab_specs(weights, nsteps, step_of):
    in_specs, out_specs, out_shapes = [], [], []
    for w in weights:
        _, rows, cols = w.shape
        slab = rows // nsteps
        assert slab * nsteps == rows and slab % 16 == 0
        in_specs.append(pl.BlockSpec((None, slab, cols), lambda *ids: (0, step_of(*ids), 0)))
        out_specs.append(pl.BlockSpec((slab, cols), lambda *ids: (step_of(*ids), 0)))
        out_shapes.append(jax.ShapeDtypeStruct((rows, cols), _bf16))
    return in_specs, out_specs, out_shapes


def _cast_slabs(src_refs, dst_refs):
    for src, dst in zip(src_refs, dst_refs):
        dst[...] = src[...].astype(dst.dtype)


def _head_rmsnorm(y, gain_row, groups):
    y2 = y * y
    if groups == 1:
        r = lax.rsqrt(jnp.mean(y2, axis=-1, keepdims=True) + EPS)
    else:
        lane = lax.broadcasted_iota(jnp.int32, y.shape, 1)
        lo = lane < QK_DIFF
        s_lo = jnp.sum(jnp.where(lo, y2, 0.0), axis=-1, keepdims=True)
        s_hi = jnp.sum(jnp.where(lo, 0.0, y2), axis=-1, keepdims=True)
        r = jnp.where(lo, lax.rsqrt(s_lo * (1.0 / QK_DIFF) + EPS),
                      lax.rsqrt(s_hi * (1.0 / QK_DIFF) + EPS))
    return (y * r) * gain_row


PROJ_SUB = 256


def _proj_subtiles(a_ref, w_ref, wb_ref, first_step, epilogue, w_transposed=False):
    @pl.when(first_step)
    def _():
        w = w_ref[...].T if w_transposed else w_ref[...]
        wb_ref[...] = w.astype(wb_ref.dtype)

    for sub in range(a_ref.shape[0] // PROJ_SUB):
        rows = slice(sub * PROJ_SUB, (sub + 1) * PROJ_SUB)
        epilogue(rows, _dot(a_ref[rows, :], wb_ref[...]))


def _heads(acc):
    return [acc[:, hh * HEAD_DIM:(hh + 1) * HEAD_DIM] for hh in range(N_HEADS)]


def _proj_qdiff_kernel(a_ref, w_ref, g_ref, q1_ref, q2_ref, wb_ref):
    def epilogue(rows, acc):
        lo = lax.broadcasted_iota(jnp.int32, (PROJ_SUB, HEAD_DIM), 1) < QK_DIFF
        for hh, y in enumerate(_heads(acc)):
            yn = _head_rmsnorm(y, g_ref[...], 2) * (QK_DIFF ** -0.5 * LOG2E)
            q1_ref[hh, :, rows] = jnp.where(lo, yn, 0.0).T.astype(q1_ref.dtype)
            q2_ref[hh, :, rows] = jnp.where(lo, 0.0, yn).T.astype(q2_ref.dtype)

    _proj_subtiles(a_ref, w_ref, wb_ref, pl.program_id(0) == 0, epilogue, w_transposed=True)


def _proj_qfox_kernel(a_ref, w_ref, g_ref, q_ref, wb_ref):
    def epilogue(rows, acc):
        for hh, y in enumerate(_heads(acc)):
            yn = _head_rmsnorm(y, g_ref[...], 1) * (HEAD_DIM ** -0.5 * LOG2E)
            q_ref[hh, :, rows] = yn.T.astype(q_ref.dtype)

    _proj_subtiles(a_ref, w_ref, wb_ref, pl.program_id(0) == 0, epilogue, w_transposed=True)


def _proj_k_kernel(a_ref, w_ref, g_ref, k_ref, wb_ref, *, groups):
    def epilogue(rows, acc):
        for hh, y in enumerate(_heads(acc)):
            k_ref[hh, rows, :] = _head_rmsnorm(y, g_ref[...], groups).astype(k_ref.dtype)

    _proj_subtiles(a_ref, w_ref, wb_ref, pl.program_id(0) == 0, epilogue, w_transposed=True)


def _proj_vt_kernel(a_ref, w_ref, vt_ref, wb_ref):
    row = lax.broadcasted_iota(jnp.int32, (V_ROWS - HEAD_DIM, TKM), 0)
    ones_rows = jnp.where(row == 0, 1.0, 0.0).astype(vt_ref.dtype)
    for hh in range(N_HEADS):
        vt_ref[hh, 0, HEAD_DIM:, :] = ones_rows

    def epilogue(rows, acc):
        for hh, y in enumerate(_heads(acc)):
            vt_ref[hh, 0, :HEAD_DIM, rows] = y.T.astype(vt_ref.dtype)

    _proj_subtiles(a_ref, w_ref, wb_ref, pl.program_id(0) == 0, epilogue, w_transposed=True)


def _proj_gates_kernel(a_ref, w_ref, b_ref, o_ref, wb_ref):
    def epilogue(rows, acc):
        o_ref[rows, :] = (1.0 / (1.0 + jnp.exp(-(acc + b_ref[...])))).astype(o_ref.dtype)

    _proj_subtiles(a_ref, w_ref, wb_ref, pl.program_id(1) == 0, epilogue)


def _proj_call(kernel, h, w, col_block, extras, out_shapes, out_specs, name, tm=TKM):
    s, d = h.shape
    in_specs = [pl.BlockSpec((tm, d), lambda i: (i, 0)),
                pl.BlockSpec((None, WIDTH, d), lambda i: (0, col_block, 0), pipeline_mode=pl.Buffered(1))]
    in_specs += [pl.BlockSpec(e.shape, lambda i: (0, 0)) for e in extras]
    return pl.pallas_call(
        kernel,
        grid=(s // tm,),
        in_specs=in_specs,
        out_specs=out_specs,
        out_shape=out_shapes,
        scratch_shapes=[pltpu.VMEM((d, WIDTH), _bf16)],
        compiler_params=_cparams(("arbitrary",)),
        name=name,
    )(h, w, *extras)


def _t_spec(tm):
    return pl.BlockSpec((N_HEADS, HEAD_DIM, tm), lambda i: (0, 0, i))


def _k_spec(tm):
    return pl.BlockSpec((N_HEADS, tm, HEAD_DIM), lambda i: (0, i, 0))


def _gates(h, w, b, tm=1024, tn=1024):
    s, d = h.shape
    n = w.shape[2]
    return pl.pallas_call(
        _proj_gates_kernel,
        grid=(n // tn, s // tm),
        in_specs=[pl.BlockSpec((tm, d), lambda j, i: (i, 0)),
                  pl.BlockSpec((None, d, tn), lambda j, i: (0, 0, j)),
                  pl.BlockSpec((1, tn), lambda j, i: (0, j))],
        out_specs=pl.BlockSpec((tm, tn), lambda j, i: (i, j)),
        out_shape=jax.ShapeDtypeStruct((s, n), _bf16),
        scratch_shapes=[pltpu.VMEM((d, tn), _bf16)],
        compiler_params=_cparams(("arbitrary", "arbitrary")),
        name="proj_gates",
    )(h, w, b.reshape(1, n))


F_ROWS = 16
F_CHUNK = 256


def _split3(x):
    hi = x.astype(_bf16)
    r1 = x - hi.astype(_f32)
    mid = r1.astype(_bf16)
    lo = (r1 - mid.astype(_f32)).astype(_bf16)
    return hi, mid, lo


def _select_rows_or_lanes(index, values):
    out = jnp.zeros(index.shape, _f32)
    for i in reversed(range(len(values))):
        out = jnp.where(index == i, values[i].astype(_f32), out)
    return out


def _forget_kernel(wt_ref, x_ref, g_ref, b_ref, h_ref, ft_ref, kx_ref, carry_ref):
    i = pl.program_id(0)

    @pl.when(i == 0)
    def _():
        carry_ref[...] = jnp.zeros_like(carry_ref)

    x = x_ref[...]
    r = lax.rsqrt(jnp.mean(x * x, axis=-1, keepdims=True) + EPS)
    h = ((x * r) * g_ref[...]).astype(h_ref.dtype)
    h_ref[...] = h
    wt8 = wt_ref[...]
    wt = jnp.concatenate([wt8, jnp.zeros_like(wt8)], axis=0).astype(_bf16)
    z = lax.dot_general(wt, h, (((1,), (1,)), ((), ())),
                        preferred_element_type=_f32) + b_ref[...]
    logf = (jnp.minimum(z, 0.0) - jnp.log(1.0 + jnp.exp(-jnp.abs(z)))) * LOG2E
    tm = logf.shape[1]
    r = lax.broadcasted_iota(jnp.int32, (F_CHUNK, F_CHUNK), 0)
    c = lax.broadcasted_iota(jnp.int32, (F_CHUNK, F_CHUNK), 1)
    upper = jnp.where(r <= c, 1.0, 0.0).astype(_bf16)
    carry = carry_ref[...]
    lane = lax.broadcasted_iota(jnp.int32, (HEAD_DIM, HEAD_DIM), 1)
    for ch in range(tm // F_CHUNK):
        x = logf[:, ch * F_CHUNK:(ch + 1) * F_CHUNK]
        hi, mid, lo = _split3(x)
        pre = _dot(hi, upper) + _dot(mid, upper) + _dot(lo, upper) + carry
        ft_ref[:, ch * F_CHUNK:(ch + 1) * F_CHUNK] = pre
        carry = pre[:, F_CHUNK - 1:F_CHUNK]
        for hh in range(N_HEADS):
            for sub in range(F_CHUNK // HEAD_DIM):
                row = pre[hh:hh + 1, sub * HEAD_DIM:(sub + 1) * HEAD_DIM]
                neg = -jnp.broadcast_to(row, (HEAD_DIM, HEAD_DIM)).T
                base = ch * F_CHUNK + sub * HEAD_DIM
                kx_ref[hh, base:base + HEAD_DIM, :] = _select_rows_or_lanes(
                    lane, _split3(neg)).astype(kx_ref.dtype)
    carry_ref[...] = carry


def _norm_and_forget(x, g, w_in_t, b_col, tm=1024):
    s, d = x.shape
    f_block = 6 * WIDTH // N_HEADS
    return pl.pallas_call(
        _forget_kernel,
        grid=(s // tm,),
        in_specs=[pl.BlockSpec((None, N_HEADS, d), lambda i: (0, f_block, 0)),
                  pl.BlockSpec((tm, d), lambda i: (i, 0)),
                  pl.BlockSpec((1, d), lambda i: (0, 0)),
                  pl.BlockSpec((F_ROWS, 1), lambda i: (0, 0))],
        out_specs=[pl.BlockSpec((tm, d), lambda i: (i, 0)),
                   pl.BlockSpec((F_ROWS, tm), lambda i: (0, i)),
                   pl.BlockSpec((N_HEADS, tm, HEAD_DIM), lambda i: (0, i, 0))],
        out_shape=[jax.ShapeDtypeStruct((s, d), _bf16),
                   jax.ShapeDtypeStruct((F_ROWS, s), _f32),
                   jax.ShapeDtypeStruct((N_HEADS, s, HEAD_DIM), _bf16)],
        scratch_shapes=[pltpu.VMEM((F_ROWS, 1), _f32)],
        compiler_params=_cparams(("arbitrary",)),
        name="norm_forget_gate",
    )(w_in_t, x, g.reshape(1, d), b_col)


def _block_partial(z, vt):
    mloc = jnp.max(z, axis=0, keepdims=True)
    o = _dot(vt, jnp.exp2(z - mloc).astype(_bf16))
    return mloc, o[HEAD_DIM:HEAD_DIM + 1], o[:HEAD_DIM]


def _merge_partials(m_ref, l_ref, acc_ref, idx, c, parts):
    cs = slice(c * TQS, (c + 1) * TQS)
    m_old = m_ref[idx, :, cs]
    m_new = m_old
    for mt, _, _ in parts:
        m_new = jnp.maximum(m_new, mt)
    a = jnp.exp2(m_old - m_new)
    l_new = a * l_ref[idx, :, cs]
    acc_new = a * acc_ref[idx, :, cs]
    for mt, l, o in parts:
        b = jnp.exp2(mt - m_new)
        l_new = l_new + b * l
        acc_new = acc_new + b * o
    m_ref[idx, :, cs] = m_new
    l_ref[idx, :, cs] = l_new
    acc_ref[idx, :, cs] = acc_new


def _init_state(m_ref, l_ref, acc_ref):
    m_ref[...] = jnp.full_like(m_ref, NEG_BIG)
    l_ref[...] = jnp.zeros_like(l_ref)
    acc_ref[...] = jnp.zeros_like(acc_ref)


def _pipelined_key_sweep(qi, scores_cols, main_chain, band_block, sa_ref, sb_ref):
    def fill_cols(refs, block, c):
        for ref, val in zip(refs, scores_cols(block, c)):
            ref[:, c * TQS:(c + 1) * TQS] = val

    def overlapped(cur_block, cur_refs, nxt_block, nxt_refs):
        for c in range(NSUB):
            for idx in range(len(cur_refs)):
                nxt_refs[idx][:, c * TQS:(c + 1) * TQS] = scores_cols(nxt_block, c)[idx]
                main_chain(cur_block, cur_refs, c, idx)

    for c in range(NSUB):
        fill_cols(sa_ref, 0, c)

    def pair(t, carry):
        overlapped(2 * t, sa_ref, 2 * t + 1, sb_ref)
        overlapped(2 * t + 1, sb_ref, 2 * t + 2, sa_ref)
        return carry

    lax.fori_loop(0, lax.shift_right_logical(qi, 1), pair, 0)
    odd = lax.rem(qi, 2) == 1

    @pl.when(odd)
    def _():
        overlapped(qi - 1, sa_ref, qi, sb_ref)
        band_block(sb_ref)

    @pl.when(jnp.logical_not(odd))
    def _():
        band_block(sa_ref)


def _local_iotas():
    lane = lax.broadcasted_iota(jnp.int32, (TK, TQS), 1)
    sub = lax.broadcasted_iota(jnp.int32, (TK, TQS), 0)
    return lane, sub


def _diff_attn_kernel(slope_ref, q1_ref, q2_ref, k_ref, vt_ref, lq1_ref, lk1_ref, lq2_ref, lk2_ref,
                      gsub_ref, o_ref, m_ref, l_ref, acc_ref, sa0_ref, sa1_ref, sb0_ref, sb1_ref):
    h = pl.program_id(0)
    qi = pl.program_id(1)
    neg_slope = -slope_ref[h] * LOG2E
    _init_state(m_ref, l_ref, acc_ref)
    qs = (q1_ref[0], q2_ref[0])

    slope2 = -neg_slope
    krow = lax.broadcasted_iota(jnp.int32, (TKM, HEAD_DIM), 0)
    klane = lax.broadcasted_iota(jnp.int32, (TKM, HEAD_DIM), 1)
    kx = jnp.where(klane < 3, (krow // HEAD_DIM).astype(_f32),
                   jnp.where(klane < 6, (krow % HEAD_DIM).astype(_f32), 0.0)).astype(_bf16)
    qrow = lax.broadcasted_iota(jnp.int32, (HEAD_DIM, TQ), 0)
    ones = jnp.ones((HEAD_DIM, TQ), _f32)
    qx = _select_rows_or_lanes(qrow, _split3(ones * (slope2 * HEAD_DIM)) + _split3(ones * slope2)).astype(_bf16)
    qs_aug = [jnp.concatenate([q, qx], axis=0) for q in qs]
    t_loc = lax.broadcasted_iota(jnp.int32, (1, TQS), 1).astype(_f32)

    def scores_cols(block, c):
        start = pl.multiple_of(block * TKM, TKM)
        k_aug = jnp.concatenate([k_ref[0, pl.ds(start, TKM), :], kx], axis=1)
        return tuple(_dot(k_aug, q[:, c * TQS:(c + 1) * TQS]) for q in qs_aug)

    def query_shift(block, c):
        return (((qi - block) * TQ + c * TQS).astype(_f32) + t_loc) * neg_slope

    def main_chain(block, s_refs, c, idx):
        vt = vt_ref[0, block]
        mloc, l, o = _block_partial(s_refs[idx][:, c * TQS:(c + 1) * TQS], vt)
        _merge_partials(m_ref, l_ref, acc_ref, idx, c, [(mloc + query_shift(block, c), l, o)])

    def band_block(s_refs):
        lane, sub = _local_iotas()
        delta = (lane - sub).astype(_f32)
        fix_diag = jnp.where((sub // 64) <= (lane // 64),
                             (sub.astype(_f32) + jnp.abs(delta)) * neg_slope, NEG_BIG)
        parts = [[[] for _ in range(NSUB)] for _ in range(2)]
        vt_band = vt_ref[0, qi]
        for j in range(NSUB):
            vt = vt_band[:, j * TK:(j + 1) * TK]
            for idx in range(2):
                for c in range(j, NSUB):
                    z = s_refs[idx][j * TK:(j + 1) * TK, c * TQS:(c + 1) * TQS]
                    if c == j:
                        mloc, l, o = _block_partial(z + fix_diag, vt)
                        parts[idx][c].append((mloc + (j * TK) * neg_slope, l, o))
                    else:
                        mloc, l, o = _block_partial(z, vt)
                        parts[idx][c].append((mloc + query_shift(qi, c), l, o))
        for idx in range(2):
            for c in range(NSUB):
                _merge_partials(m_ref, l_ref, acc_ref, idx, c, parts[idx][c])

    _pipelined_key_sweep(qi, scores_cols, main_chain, band_block, (sa0_ref, sa1_ref), (sb0_ref, sb1_ref))

    lam =(jnp.exp(jnp.sum(lq1_ref[...] * lk1_ref[...], axis=-1, keepdims=True))
           - jnp.exp(jnp.sum(lq2_ref[...] * lk2_ref[...], axis=-1, keepdims=True))
           + LAMBDA_INIT)
    o = acc_ref[0] / l_ref[0] - lam * (acc_ref[1] / l_ref[1])
    r = lax.rsqrt(jnp.mean(o * o, axis=0, keepdims=True) + EPS)
    y = (o * r) * gsub_ref[...] * (1.0 - LAMBDA_INIT)
    o_ref[...] = y.T.astype(o_ref.dtype)


def _diff_attention(slopes, q1t, q2t, k, vt, lq1, lk1, lq2, lk2, gsub_col):
    nq = SEQ // TQ
    nkv = SEQ // TKM
    vec = lambda: pl.BlockSpec((1, QK_DIFF), lambda h, q: (0, 0))
    return pl.pallas_call(
        _diff_attn_kernel,
        grid=(N_HEADS, nq),
        in_specs=[pl.BlockSpec(memory_space=pltpu.SMEM),
                  pl.BlockSpec((1, HEAD_DIM, TQ), lambda h, q: (h, 0, q)),
                  pl.BlockSpec((1, HEAD_DIM, TQ), lambda h, q: (h, 0, q)),
                  pl.BlockSpec((1, SEQ, HEAD_DIM), lambda h, q: (h, 0, 0)),
                  pl.BlockSpec((1, nkv, V_ROWS, TKM), lambda h, q: (h, 0, 0, 0)),
                  vec(), vec(), vec(), vec(),
                  pl.BlockSpec((HEAD_DIM, 1), lambda h, q: (0, 0))],
        out_specs=pl.BlockSpec((TQ, HEAD_DIM), lambda h, q: (q, h)),
        out_shape=jax.ShapeDtypeStruct((SEQ, WIDTH), _bf16),
        scratch_shapes=[pltpu.VMEM((2, 1, TQ), _f32),
                        pltpu.VMEM((2, 1, TQ), _f32),
                        pltpu.VMEM((2, HEAD_DIM, TQ), _f32)] + [pltpu.VMEM((TKM, TQ), _f32)] * 4,
        compiler_params=_cparams(("parallel", "parallel")),
        name="diff_attention",
    )(slopes, q1t, q2t, k, vt, lq1, lk1, lq2, lk2, gsub_col)


def _fox_attn_kernel(q_ref, k_ref, vt_ref, ft_ref, kx_ref, *refs, n_cast):
    cast_in, o_ref, cast_out = refs[:n_cast], refs[n_cast], refs[n_cast + 1:2 * n_cast + 1]
    m_ref, l_ref, acc_ref, sa_ref, sb_ref = refs[2 * n_cast + 1:]
    qi = pl.program_id(1)
    _cast_slabs(cast_in, cast_out)
    _init_state(m_ref, l_ref, acc_ref)
    q = q_ref[0]
    ft = ft_ref[0]
    qrow = lax.broadcasted_iota(jnp.int32, (HEAD_DIM, TQ), 0)
    q_aug = jnp.concatenate([q, jnp.where(qrow < 3, 1.0, 0.0).astype(_bf16)], axis=0)

    def scores_cols(block, c):
        start = pl.multiple_of(block * TKM, TKM)
        k_aug = jnp.concatenate([k_ref[0, pl.ds(start, TKM), :], kx_ref[0, pl.ds(start, TKM), :]], axis=1)
        return (_dot(k_aug, q_aug[:, c * TQS:(c + 1) * TQS]),)

    def main_chain(block, s_ref, c, idx):
        cs = slice(c * TQS, (c + 1) * TQS)
        mloc, l, o = _block_partial(s_ref[0][:, cs], vt_ref[0, block])
        _merge_partials(m_ref, l_ref, acc_ref, 0, c, [(mloc + ft[:, cs], l, o)])

    def band_block(s_ref):
        lane, sub = _local_iotas()
        causal = sub <= lane
        parts = [[] for _ in range(NSUB)]
        vt_band = vt_ref[0, qi]
        for j in range(NSUB):
            vt = vt_band[:, j * TK:(j + 1) * TK]
            for c in range(j, NSUB):
                z = s_ref[0][j * TK:(j + 1) * TK, c * TQS:(c + 1) * TQS]
                if c == j:
                    z = jnp.where(causal, z, NEG_BIG)
                mloc, l, o = _block_partial(z, vt)
                parts[c].append((mloc + ft[:, c * TQS:(c + 1) * TQS], l, o))
        for c in range(NSUB):
            _merge_partials(m_ref, l_ref, acc_ref, 0, c, parts[c])

    _pipelined_key_sweep(qi, scores_cols, main_chain, band_block, (sa_ref,), (sb_ref,))

    o = acc_ref[0] / l_ref[0]
    o_ref[...] = o.T.astype(o_ref.dtype)


def _fox_attention(qt, k, vt, ft3, kx, cast_weights):
    nq = SEQ // TQ
    nkv = SEQ // TKM
    cast_in, cast_out, cast_shapes = _cast_slab_specs(cast_weights, N_HEADS * nq, lambda h, q: h * nq + q)
    outs = pl.pallas_call(
        functools.partial(_fox_attn_kernel, n_cast=len(cast_weights)),
        grid=(N_HEADS, nq),
        in_specs=[pl.BlockSpec((1, HEAD_DIM, TQ), lambda h, q: (h, 0, q)),
                  pl.BlockSpec((1, SEQ, HEAD_DIM), lambda h, q: (h, 0, 0)),
                  pl.BlockSpec((1, nkv, V_ROWS, TKM), lambda h, q: (h, 0, 0, 0)),
                  pl.BlockSpec((1, 1, TQ), lambda h, q: (h, 0, q)),
                  pl.BlockSpec((1, SEQ, HEAD_DIM), lambda h, q: (h, 0, 0))] + cast_in,
        out_specs=[pl.BlockSpec((TQ, HEAD_DIM), lambda h, q: (q, h))] + cast_out,
        out_shape=[jax.ShapeDtypeStruct((SEQ, WIDTH), _bf16)] + cast_shapes,
        scratch_shapes=[pltpu.VMEM((1, 1, TQ), _f32),
                        pltpu.VMEM((1, 1, TQ), _f32),
                        pltpu.VMEM((1, HEAD_DIM, TQ), _f32),
                        pltpu.VMEM((TKM, TQ), _f32),
                        pltpu.VMEM((TKM, TQ), _f32)],
        compiler_params=_cparams(("parallel", "parallel")),
        name="fox_attention",
    )(qt, k, vt, ft3, kx, *cast_weights)
    return outs[0], outs[1:]


def _merge_kernel(oa_ref, ob_ref, g_ref, x_ref, wbd_ref, wbf_ref, wout_ref, gm_ref, x1_ref, h2_ref):
    a = _dot(oa_ref[...], wbd_ref[...])
    b = _dot(ob_ref[...], wbf_ref[...])
    g = g_ref[...].astype(_f32)
    merged = g[:, :D_MODEL] * a + g[:, D_MODEL:] * b
    x1 = x_ref[...] + _dot(merged.astype(_bf16), wout_ref[...])
    x1_ref[...] = x1
    r = lax.rsqrt(jnp.mean(x1 * x1, axis=-1, keepdims=True) + EPS)
    h2_ref[...] = ((x1 * r) * gm_ref[...]).astype(h2_ref.dtype)


def _merge(oa, ob, gates, x, wbd, wbf, wout, gm, tm=256):
    s, d = x.shape
    const = lambda shape: pl.BlockSpec(shape, lambda i: (0, 0), pipeline_mode=pl.Buffered(1))
    return pl.pallas_call(
        _merge_kernel,
        grid=(s // tm,),
        in_specs=[pl.BlockSpec((tm, WIDTH), lambda i: (i, 0)),
                  pl.BlockSpec((tm, WIDTH), lambda i: (i, 0)),
                  pl.BlockSpec((tm, 2 * d), lambda i: (i, 0)),
                  pl.BlockSpec((tm, d), lambda i: (i, 0)),
                  const((WIDTH, d)), const((WIDTH, d)), const((d, d)),
                  pl.BlockSpec((1, d), lambda i: (0, 0))],
        out_specs=[pl.BlockSpec((tm, d), lambda i: (i, 0)),
                   pl.BlockSpec((tm, d), lambda i: (i, 0))],
        out_shape=[jax.ShapeDtypeStruct((s, d), _f32),
                   jax.ShapeDtypeStruct((s, d), _bf16)],
        compiler_params=_cparams(("parallel",)),
        name="merge_out_proj",
    )(oa, ob, gates, x, wbd, wbf, wout, gm.reshape(1, d))


def _mlp_kernel(h_ref, wu_ref, wd_ref, x_ref, o_ref):
    k = pl.program_id(1)

    @pl.when(k == 0)
    def _():
        o_ref[...] = x_ref[...]

    u = jnp.maximum(_dot(h_ref[...], wu_ref[...]), 0.0)
    o_ref[...] += _dot((u * u).astype(_bf16), wd_ref[...])


def _mlp(h2, wu, wd, x1, tm=512, tf=1024):
    s, d = x1.shape
    f = wu.shape[1]
    return pl.pallas_call(
        _mlp_kernel,
        grid=(s // tm, f // tf),
        in_specs=[pl.BlockSpec((tm, d), lambda i, k: (i, 0)),
                  pl.BlockSpec((d, tf), lambda i, k: (0, k)),
                  pl.BlockSpec((tf, d), lambda i, k: (k, 0)),
                  pl.BlockSpec((tm, d), lambda i, k: (i, 0))],
        out_specs=pl.BlockSpec((tm, d), lambda i, k: (i, 0)),
        out_shape=jax.ShapeDtypeStruct((s, d), _f32),
        compiler_params=_cparams(("parallel", "arbitrary")),
        name="mlp_relu2",
    )(h2, wu, wd, x1)


def kernel(x, norm_mix, w_in, b_forget, qnorm_diff, knorm_diff, lambda_q1, lambda_k1, lambda_q2, lambda_k2,
           subln_diff, qnorm_fox, knorm_fox, w_branch_diff, w_branch_fox, w_gate, b_gate, w_out, norm_mlp,
           w_mlp_up, w_mlp_down):
    assert x.shape == (1, SEQ, D_MODEL)
    x2 = x[0]
    w_in_t = jnp.swapaxes(w_in, 1, 2)

    bf_col = jnp.zeros((F_ROWS, 1), _f32).at[:N_HEADS, 0].set(b_forget[0])
    h, ft, kx_fox = _norm_and_forget(x2, norm_mix[0], w_in_t, bf_col)
    ft3 = ft.reshape(F_ROWS, 1, SEQ)

    g_qd = jnp.tile(qnorm_diff[0], 2).reshape(1, HEAD_DIM)
    g_kd = jnp.tile(knorm_diff[0], 2).reshape(1, HEAD_DIM)
    tm = TKM
    t_shape = jax.ShapeDtypeStruct((N_HEADS, HEAD_DIM, SEQ), _bf16)
    k_shape = jax.ShapeDtypeStruct((N_HEADS, SEQ, HEAD_DIM), _bf16)
    vt_shape = jax.ShapeDtypeStruct((N_HEADS, SEQ // TKM, V_ROWS, TKM), _bf16)
    vt_spec = pl.BlockSpec((N_HEADS, 1, V_ROWS, TKM), lambda i: (0, i, 0, 0))

    q1t, q2t = _proj_call(_proj_qdiff_kernel, h, w_in_t, 0, [g_qd], [t_shape, t_shape],
                          [_t_spec(tm), _t_spec(tm)], "proj_q_diff")
    ka = _proj_call(functools.partial(_proj_k_kernel, groups=2), h, w_in_t, 1, [g_kd], k_shape,
                    _k_spec(tm), "proj_k_diff")
    vat = _proj_call(_proj_vt_kernel, h, w_in_t, 2, [], vt_shape, vt_spec, "proj_v_diff")
    qbt = _proj_call(_proj_qfox_kernel, h, w_in_t, 3, [qnorm_fox[0].reshape(1, HEAD_DIM)], t_shape,
                     _t_spec(tm), "proj_q_fox")
    kb = _proj_call(functools.partial(_proj_k_kernel, groups=1), h, w_in_t, 4,
                    [knorm_fox[0].reshape(1, HEAD_DIM)], k_shape, _k_spec(tm), "proj_k_fox")
    vbt = _proj_call(_proj_vt_kernel, h, w_in_t, 5, [], vt_shape, vt_spec, "proj_v_fox")

    gates = _gates(h, w_gate, b_gate[0])

    slopes = 2.0 ** (-8.0 * jnp.arange(1, N_HEADS + 1, dtype=_f32) / N_HEADS)
    row = lambda v: v[0].reshape(1, QK_DIFF)
    oa = _diff_attention(slopes, q1t, q2t, ka, vat, row(lambda_q1), row(lambda_k1), row(lambda_q2),
                         row(lambda_k2), subln_diff[0].reshape(HEAD_DIM, 1))
    ob, (wbd, wbf, wout, wup, wdown) = _fox_attention(
        qbt, kb, vbt, ft3, kx_fox, [w_branch_diff, w_branch_fox, w_out, w_mlp_up, w_mlp_down])

    x1, h2 = _merge(oa, ob, gates, x2, wbd, wbf, wout, norm_mlp[0])
    out = _mlp(h2, wup, wdown, x1)
    return out[None]
```

```python
import functools

import jax
import jax.numpy as jnp
from jax import lax
from jax.experimental import pallas as pl
from jax.experimental.pallas import tpu as pltpu

D_MODEL = 2048
SEQ = 8192
HEAD_DIM = 128
N_HEADS = 8
QK_DIFF = 64
WIDTH = N_HEADS * HEAD_DIM
D_FF = 4 * D_MODEL
EPS = 1e-6
LAMBDA_INIT = 0.8 - 0.6 * 1.0
NEG_BIG = -1e30

TQ = 1024
TQS = 256
TK = 256
TKM = TQ
NSUB = TQ // TQS
LOG2E = 1.4426950408889634
V_ROWS = HEAD_DIM + 16
VMEM_LIMIT = 56 * 1024 * 1024

_f32 = jnp.float32
_bf16 = jnp.bfloat16


def _cparams(sem):
    return pltpu.CompilerParams(dimension_semantics=sem, vmem_limit_bytes=VMEM_LIMIT)


def _dot(a, b):
    return jnp.dot(a, b, preferred_element_type=_f32)


def _cast_slab_specs(weights, nsteps, step_of):
    in_specs, out_specs, out_shapes = [], [], []
    for w in weights:
        _, rows, cols = w.shape
        slab = rows // nsteps
        assert slab * nsteps == rows and slab % 16 == 0
        in_specs.append(pl.BlockSpec((None, slab, cols), lambda *ids: (0, step_of(*ids), 0)))
        out_specs.append(pl.BlockSpec((slab, cols), lambda *ids: (step_of(*ids), 0)))
        out_shapes.append(jax.ShapeDtypeStruct((rows, cols), _bf16))
    return in_specs, out_specs, out_shapes


def _cast_slabs(src_refs, dst_refs):
    for src, dst in zip(src_refs, dst_refs):
        dst[...] = src[...].astype(dst.dtype)


def _head_rmsnorm(y, gain_row, groups):
    y2 = y * y
    if groups == 1:
        r = lax.rsqrt(jnp.mean(y2, axis=-1, keepdims=True) + EPS)
    else:
        lane = lax.broadcasted_iota(jnp.int32, y.shape, 1)
        lo = lane < QK_DIFF
        s_lo = jnp.sum(jnp.where(lo, y2, 0.0), axis=-1, keepdims=True)
        s_hi = jnp.sum(jnp.where(lo, 0.0, y2), axis=-1, keepdims=True)
        r = jnp.where(lo, lax.rsqrt(s_lo * (1.0 / QK_DIFF) + EPS),
                      lax.rsqrt(s_hi * (1.0 / QK_DIFF) + EPS))
    return (y * r) * gain_row


PROJ_SUB = 256


def _proj_subtiles(a_ref, w_ref, wb_ref, first_step, epilogue, w_transposed=False):
    @pl.when(first_step)
    def _():
        w = w_ref[...].T if w_transposed else w_ref[...]
        wb_ref[...] = w.astype(wb_ref.dtype)

    for sub in range(a_ref.shape[0] // PROJ_SUB):
        rows = slice(sub * PROJ_SUB, (sub + 1) * PROJ_SUB)
        epilogue(rows, _dot(a_ref[rows, :], wb_ref[...]))


def _heads(acc):
    return [acc[:, hh * HEAD_DIM:(hh + 1) * HEAD_DIM] for hh in range(N_HEADS)]


def _proj_qdiff_kernel(a_ref, w_ref, g_ref, q1_ref, q2_ref, wb_ref):
    def epilogue(rows, acc):
        lo = lax.broadcasted_iota(jnp.int32, (PROJ_SUB, HEAD_DIM), 1) < QK_DIFF
        for hh, y in enumerate(_heads(acc)):
            yn = _head_rmsnorm(y, g_ref[...], 2) * (QK_DIFF ** -0.5 * LOG2E)
            q1_ref[hh, 0, :, rows] = jnp.where(lo, yn, 0.0).T.astype(q1_ref.dtype)
            q2_ref[hh, 0, :, rows] = jnp.where(lo, 0.0, yn).T.astype(q2_ref.dtype)

    _proj_subtiles(a_ref, w_ref, wb_ref, pl.program_id(0) == 0, epilogue, w_transposed=True)


def _proj_qfox_kernel(a_ref, w_ref, g_ref, q_ref, wb_ref):
    def epilogue(rows, acc):
        for hh, y in enumerate(_heads(acc)):
            yn = _head_rmsnorm(y, g_ref[...], 1) * (HEAD_DIM ** -0.5 * LOG2E)
            q_ref[hh, 0, :, rows] = yn.T.astype(q_ref.dtype)

    _proj_subtiles(a_ref, w_ref, wb_ref, pl.program_id(0) == 0, epilogue, w_transposed=True)


def _proj_k_kernel(a_ref, w_ref, g_ref, k_ref, wb_ref, *, groups):
    def epilogue(rows, acc):
        for hh, y in enumerate(_heads(acc)):
            k_ref[hh, rows, :] = _head_rmsnorm(y, g_ref[...], groups).astype(k_ref.dtype)

    _proj_subtiles(a_ref, w_ref, wb_ref, pl.program_id(0) == 0, epilogue, w_transposed=True)


def _proj_vt_kernel(a_ref, w_ref, vt_ref, wb_ref):
    row = lax.broadcasted_iota(jnp.int32, (V_ROWS - HEAD_DIM, TKM), 0)
    ones_rows = jnp.where(row == 0, 1.0, 0.0).astype(vt_ref.dtype)
    for hh in range(N_HEADS):
        vt_ref[hh, 0, HEAD_DIM:, :] = ones_rows

    def epilogue(rows, acc):
        for hh, y in enumerate(_heads(acc)):
            vt_ref[hh, 0, :HEAD_DIM, rows] = y.T.astype(vt_ref.dtype)

    _proj_subtiles(a_ref, w_ref, wb_ref, pl.program_id(0) == 0, epilogue, w_transposed=True)


def _proj_gates_kernel(a_ref, w_ref, b_ref, *refs, n_cast):
    cast_in, o_ref, cast_out, wb_ref = refs[:n_cast], refs[n_cast], refs[n_cast + 1:-1], refs[-1]
    _cast_slabs(cast_in, cast_out)

    def epilogue(rows, acc):
        o_ref[rows, :] = (1.0 / (1.0 + jnp.exp(-(acc + b_ref[...])))).astype(o_ref.dtype)

    _proj_subtiles(a_ref, w_ref, wb_ref, pl.program_id(1) == 0, epilogue)


def _proj_call(kernel, h, w, col_block, extras, out_shapes, out_specs, name, tm=TKM):
    s, d = h.shape
    in_specs = [pl.BlockSpec((tm, d), lambda i: (i, 0)),
                pl.BlockSpec((None, WIDTH, d), lambda i: (0, col_block, 0), pipeline_mode=pl.Buffered(1))]
    in_specs += [pl.BlockSpec(e.shape, lambda i: (0, 0)) for e in extras]
    return pl.pallas_call(
        kernel,
        grid=(s // tm,),
        in_specs=in_specs,
        out_specs=out_specs,
        out_shape=out_shapes,
        scratch_shapes=[pltpu.VMEM((d, WIDTH), _bf16)],
        compiler_params=_cparams(("arbitrary",)),
        name=name,
    )(h, w, *extras)


def _t_spec(tm):
    assert tm == TQ
    return pl.BlockSpec((N_HEADS, 1, HEAD_DIM, tm), lambda i: (0, i, 0, 0))


def _k_spec(tm):
    return pl.BlockSpec((N_HEADS, tm, HEAD_DIM), lambda i: (0, i, 0))


def _gates(h, w, b, cast_weights, tm=1024, tn=1024):
    s, d = h.shape
    n = w.shape[2]
    ni = s // tm
    cast_in, cast_out, cast_shapes = _cast_slab_specs(cast_weights, (n // tn) * ni, lambda j, i: j * ni + i)
    outs = pl.pallas_call(
        functools.partial(_proj_gates_kernel, n_cast=len(cast_weights)),
        grid=(n // tn, ni),
        in_specs=[pl.BlockSpec((tm, d), lambda j, i: (i, 0)),
                  pl.BlockSpec((None, d, tn), lambda j, i: (0, 0, j)),
                  pl.BlockSpec((1, tn), lambda j, i: (0, j))] + cast_in,
        out_specs=[pl.BlockSpec((tm, tn), lambda j, i: (i, j))] + cast_out,
        out_shape=[jax.ShapeDtypeStruct((s, n), _bf16)] + cast_shapes,
        scratch_shapes=[pltpu.VMEM((d, tn), _bf16)],
        compiler_params=_cparams(("arbitrary", "arbitrary")),
        name="proj_gates",
    )(h, w, b.reshape(1, n), *cast_weights)
    return outs[0], outs[1:]


F_ROWS = 16
F_CHUNK = 256


def _split3(x):
    hi = x.astype(_bf16)
    r1 = x - hi.astype(_f32)
    mid = r1.astype(_bf16)
    lo = (r1 - mid.astype(_f32)).astype(_bf16)
    return hi, mid, lo


def _select_rows_or_lanes(index, values):
    out = jnp.zeros(index.shape, _f32)
    for i in reversed(range(len(values))):
        out = jnp.where(index == i, values[i].astype(_f32), out)
    return out


def _forget_kernel(wt_ref, x_ref, g_ref, b_ref, h_ref, ft_ref, kx_ref, carry_ref):
    i = pl.program_id(0)

    @pl.when(i == 0)
    def _():
        carry_ref[...] = jnp.zeros_like(carry_ref)

    x = x_ref[...]
    r = lax.rsqrt(jnp.mean(x * x, axis=-1, keepdims=True) + EPS)
    h = ((x * r) * g_ref[...]).astype(h_ref.dtype)
    h_ref[...] = h
    wt8 = wt_ref[...]
    wt = jnp.concatenate([wt8, jnp.zeros_like(wt8)], axis=0).astype(_bf16)
    z = lax.dot_general(wt, h, (((1,), (1,)), ((), ())),
                        preferred_element_type=_f32) + b_ref[...]
    logf = (jnp.minimum(z, 0.0) - jnp.log(1.0 + jnp.exp(-jnp.abs(z)))) * LOG2E
    tm = logf.shape[1]
    r = lax.broadcasted_iota(jnp.int32, (F_CHUNK, F_CHUNK), 0)
    c = lax.broadcasted_iota(jnp.int32, (F_CHUNK, F_CHUNK), 1)
    upper = jnp.where(r <= c, 1.0, 0.0).astype(_bf16)
    carry = carry_ref[...]
    lane = lax.broadcasted_iota(jnp.int32, (HEAD_DIM, HEAD_DIM), 1)
    for ch in range(tm // F_CHUNK):
        x = logf[:, ch * F_CHUNK:(ch + 1) * F_CHUNK]
        hi, mid, lo = _split3(x)
        pre = _dot(hi, upper) + _dot(mid, upper) + _dot(lo, upper) + carry
        ft_ref[:, ch * F_CHUNK:(ch + 1) * F_CHUNK] = pre
        carry = pre[:, F_CHUNK - 1:F_CHUNK]
        for hh in range(N_HEADS):
            for sub in range(F_CHUNK // HEAD_DIM):
                row = pre[hh:hh + 1, sub * HEAD_DIM:(sub + 1) * HEAD_DIM]
                neg = -jnp.broadcast_to(row, (HEAD_DIM, HEAD_DIM)).T
                base = ch * F_CHUNK + sub * HEAD_DIM
                kx_ref[hh, base:base + HEAD_DIM, :] = _select_rows_or_lanes(
                    lane, _split3(neg)).astype(kx_ref.dtype)
    carry_ref[...] = carry


def _norm_and_forget(x, g, w_in_t, b_col, tm=1024):
    s, d = x.shape
    f_block = 6 * WIDTH // N_HEADS
    return pl.pallas_call(
        _forget_kernel,
        grid=(s // tm,),
        in_specs=[pl.BlockSpec((None, N_HEADS, d), lambda i: (0, f_block, 0)),
                  pl.BlockSpec((tm, d), lambda i: (i, 0)),
                  pl.BlockSpec((1, d), lambda i: (0, 0)),
                  pl.BlockSpec((F_ROWS, 1), lambda i: (0, 0))],
        out_specs=[pl.BlockSpec((tm, d), lambda i: (i, 0)),
                   pl.BlockSpec((F_ROWS, tm), lambda i: (0, i)),
                   pl.BlockSpec((N_HEADS, tm, HEAD_DIM), lambda i: (0, i, 0))],
        out_shape=[jax.ShapeDtypeStruct((s, d), _bf16),
                   jax.ShapeDtypeStruct((F_ROWS, s), _f32),
                   jax.ShapeDtypeStruct((N_HEADS, s, HEAD_DIM), _bf16)],
        scratch_shapes=[pltpu.VMEM((F_ROWS, 1), _f32)],
        compiler_params=_cparams(("arbitrary",)),
        name="norm_forget_gate",
    )(w_in_t, x, g.reshape(1, d), b_col)


def _block_partial(z, vt):
    mloc = jnp.max(z, axis=0, keepdims=True)
    o = _dot(vt, jnp.exp2(z - mloc).astype(_bf16))
    return mloc, o[HEAD_DIM:HEAD_DIM + 1], o[:HEAD_DIM]


def _merge_partials(m_ref, l_ref, acc_ref, idx, c, parts):
    cs = slice(c * TQS, (c + 1) * TQS)
    m_old = m_ref[idx, :, cs]
    m_new = m_old
    for mt, _, _ in parts:
        m_new = jnp.maximum(m_new, mt)
    a = jnp.exp2(m_old - m_new)
    l_new = a * l_ref[idx, :, cs]
    acc_new = a * acc_ref[idx, :, cs]
    for mt, l, o in parts:
        b = jnp.exp2(mt - m_new)
        l_new = l_new + b * l
        acc_new = acc_new + b * o
    m_ref[idx, :, cs] = m_new
    l_ref[idx, :, cs] = l_new
    acc_ref[idx, :, cs] = acc_new


def _init_state(m_ref, l_ref, acc_ref):
    m_ref[...] = jnp.full_like(m_ref, NEG_BIG)
    l_ref[...] = jnp.zeros_like(l_ref)
    acc_ref[...] = jnp.zeros_like(acc_ref)


def _head_sweep(scores_cols, main_chain, band_row, start_tile, finish_tile, sa_ref, sb_ref, sc_ref):
    n_maps = len(sa_ref)
    n_tiles = SEQ // TQ

    def fill_cols(refs, tile, block, c):
        for ref, val in zip(refs, scores_cols(tile, block, c)):
            ref[:, c * TQS:(c + 1) * TQS] = val

    def overlapped(tile, cur_block, cur_refs, nxt_block, nxt_refs):
        for c in range(NSUB):
            for idx in range(n_maps):
                nxt_refs[idx][:, c * TQS:(c + 1) * TQS] = scores_cols(tile, nxt_block, c)[idx]
                main_chain(tile, cur_block, cur_refs, c, idx)

    for c in range(NSUB):
        fill_cols(sc_ref, 0, 0, c)
    start_tile()

    def tile_body(qi, carry):
        def pair(t, carry2):
            overlapped(qi, 2 * t, sa_ref, 2 * t + 1, sb_ref)
            overlapped(qi, 2 * t + 1, sb_ref, 2 * t + 2, sa_ref)
            return carry2

        n_pairs = lax.shift_right_logical(jnp.maximum(qi - 1, 0), 1)
        lax.fori_loop(0, n_pairs, pair, 0)
        last = 2 * n_pairs

        @pl.when(qi - last == 2)
        def _():
            overlapped(qi, last, sa_ref, last + 1, sb_ref)
            overlapped(qi, last + 1, sb_ref, qi, sc_ref)

        @pl.when(qi - last == 1)
        def _():
            overlapped(qi, last, sa_ref, qi, sc_ref)

        nxt = jnp.minimum(qi + 1, n_tiles - 1)
        for j in range(NSUB):
            fill_cols(sa_ref, nxt, 0, j)
            band_row(qi, j, sc_ref)
        finish_tile(qi)
        start_tile()
        return carry

    lax.fori_loop(0, n_tiles, tile_body, 0)


def _local_iotas():
    lane = lax.broadcasted_iota(jnp.int32, (TK, TQS), 1)
    sub = lax.broadcasted_iota(jnp.int32, (TK, TQS), 0)
    return lane, sub


def _diff_attn_kernel(slope_ref, q1_ref, q2_ref, k_ref, vt_ref, lq1_ref, lk1_ref, lq2_ref, lk2_ref,
                      gsub_ref, o_ref, m_ref, l_ref, acc_ref, *s_refs):
    h = pl.program_id(0)
    neg_slope = -slope_ref[h] * LOG2E
    q_refs = (q1_ref, q2_ref)

    slope2 = -neg_slope
    krow = lax.broadcasted_iota(jnp.int32, (TKM, HEAD_DIM), 0)
    klane = lax.broadcasted_iota(jnp.int32, (TKM, HEAD_DIM), 1)
    kx = jnp.where(klane < 3, (krow // HEAD_DIM).astype(_f32),
                   jnp.where(klane < 6, (krow % HEAD_DIM).astype(_f32), 0.0)).astype(_bf16)
    qrow = lax.broadcasted_iota(jnp.int32, (HEAD_DIM, TQS), 0)
    ones = jnp.ones((HEAD_DIM, TQS), _f32)
    qx = _select_rows_or_lanes(qrow, _split3(ones * (slope2 * HEAD_DIM)) + _split3(ones * slope2)).astype(_bf16)
    t_loc = lax.broadcasted_iota(jnp.int32, (1, TQS), 1).astype(_f32)
    lane, sub = _local_iotas()
    fix_diag = jnp.where((sub // 64) <= (lane // 64),
                         (sub.astype(_f32) + jnp.abs((lane - sub).astype(_f32))) * neg_slope, NEG_BIG)
    lam = (jnp.exp(jnp.sum(lq1_ref[...] * lk1_ref[...], axis=-1, keepdims=True))
           - jnp.exp(jnp.sum(lq2_ref[...] * lk2_ref[...], axis=-1, keepdims=True))
           + LAMBDA_INIT)

    def scores_cols(tile, block, c):
        start = pl.multiple_of(block * TKM, TKM)
        k_aug = jnp.concatenate([k_ref[0, pl.ds(start, TKM), :], kx], axis=1)
        cs = slice(c * TQS, (c + 1) * TQS)
        return tuple(_dot(k_aug, jnp.concatenate([q_ref[0, tile, :, cs], qx], axis=0)) for q_ref in q_refs)

    def query_shift(tile, block, c):
        return (((tile - block) * TQ + c * TQS).astype(_f32) + t_loc) * neg_slope

    def main_chain(tile, block, s_refs, c, idx):
        vt = vt_ref[0, block]
        mloc, l, o = _block_partial(s_refs[idx][:, c * TQS:(c + 1) * TQS], vt)
        _merge_partials(m_ref, l_ref, acc_ref, idx, c, [(mloc + query_shift(tile, block, c), l, o)])

    def band_row(tile, j, s_refs):
        vt = vt_ref[0, tile, :, j * TK:(j + 1) * TK]
        for idx in range(2):
            for c in range(j, NSUB):
                z = s_refs[idx][j * TK:(j + 1) * TK, c * TQS:(c + 1) * TQS]
                if c == j:
                    mloc, l, o = _block_partial(z + fix_diag, vt)
                    part = (mloc + (j * TK) * neg_slope, l, o)
                else:
                    mloc, l, o = _block_partial(z, vt)
                    part = (mloc + query_shift(tile, tile, c), l, o)
                _merge_partials(m_ref, l_ref, acc_ref, idx, c, [part])

    def finish_tile(tile):
        o = acc_ref[0] / l_ref[0] - lam * (acc_ref[1] / l_ref[1])
        r = lax.rsqrt(jnp.mean(o * o, axis=0, keepdims=True) + EPS)
        y = (o * r) * gsub_ref[...] * (1.0 - LAMBDA_INIT)
        o_ref[pl.ds(pl.multiple_of(tile * TQ, TQ), TQ), :] = y.T.astype(o_ref.dtype)

    n = len(s_refs) // 3
    _head_sweep(scores_cols, main_chain, band_row, lambda: _init_state(m_ref, l_ref, acc_ref), finish_tile,
                s_refs[:n], s_refs[n:2 * n], s_refs[2 * n:])


def _head_specs(n_q):
    nq = SEQ // TQ
    q_spec = pl.BlockSpec((1, nq, HEAD_DIM, TQ), lambda h: (h, 0, 0, 0))
    return [q_spec] * n_q + [pl.BlockSpec((1, SEQ, HEAD_DIM), lambda h: (h, 0, 0)),
                             pl.BlockSpec((1, SEQ // TKM, V_ROWS, TKM), lambda h: (h, 0, 0, 0))]


def _head_scratch(n_maps):
    return [pltpu.VMEM((n_maps, 1, TQ), _f32),
            pltpu.VMEM((n_maps, 1, TQ), _f32),
            pltpu.VMEM((n_maps, HEAD_DIM, TQ), _f32)] + [pltpu.VMEM((TKM, TQ), _f32)] * (3 * n_maps)


def _diff_attention(slopes, q1t, q2t, k, vt, lq1, lk1, lq2, lk2, gsub_col):
    vec = lambda: pl.BlockSpec((1, QK_DIFF), lambda h: (0, 0))
    q_specs = _head_specs(2)
    return pl.pallas_call(
        _diff_attn_kernel,
        grid=(N_HEADS,),
        in_specs=[pl.BlockSpec(memory_space=pltpu.SMEM)] + q_specs + [
            vec(), vec(), vec(), vec(), pl.BlockSpec((HEAD_DIM, 1), lambda h: (0, 0))],
        out_specs=pl.BlockSpec((SEQ, HEAD_DIM), lambda h: (0, h)),
        out_shape=jax.ShapeDtypeStruct((SEQ, WIDTH), _bf16),
        scratch_shapes=_head_scratch(2),
        compiler_params=_cparams(("parallel",)),
        name="diff_attention",
    )(slopes, q1t, q2t, k, vt, lq1, lk1, lq2, lk2, gsub_col)


def _fox_attn_kernel(q_ref, k_ref, vt_ref, ft_ref, kx_ref, o_ref, m_ref, l_ref, acc_ref, *s_refs):
    qrow = lax.broadcasted_iota(jnp.int32, (HEAD_DIM, TQS), 0)
    ones_rows = jnp.where(qrow < 3, 1.0, 0.0).astype(_bf16)
    lane, sub = _local_iotas()
    causal = sub <= lane

    def scores_cols(tile, block, c):
        start = pl.multiple_of(block * TKM, TKM)
        k_aug = jnp.concatenate([k_ref[0, pl.ds(start, TKM), :], kx_ref[0, pl.ds(start, TKM), :]], axis=1)
        q_aug = jnp.concatenate([q_ref[0, tile, :, c * TQS:(c + 1) * TQS], ones_rows], axis=0)
        return (_dot(k_aug, q_aug),)

    def main_chain(tile, block, s_ref, c, idx):
        cs = slice(c * TQS, (c + 1) * TQS)
        mloc, l, o = _block_partial(s_ref[0][:, cs], vt_ref[0, block])
        _merge_partials(m_ref, l_ref, acc_ref, 0, c, [(mloc + ft_ref[0, tile, :, cs], l, o)])

    def band_row(tile, j, s_ref):
        vt = vt_ref[0, tile, :, j * TK:(j + 1) * TK]
        for c in range(j, NSUB):
            cs = slice(c * TQS, (c + 1) * TQS)
            z = s_ref[0][j * TK:(j + 1) * TK, cs]
            if c == j:
                z = jnp.where(causal, z, NEG_BIG)
            mloc, l, o = _block_partial(z, vt)
            _merge_partials(m_ref, l_ref, acc_ref, 0, c, [(mloc + ft_ref[0, tile, :, cs], l, o)])

    def finish_tile(tile):
        o = acc_ref[0] / l_ref[0]
        o_ref[pl.ds(pl.multiple_of(tile * TQ, TQ), TQ), :] = o.T.astype(o_ref.dtype)

    _head_sweep(scores_cols, main_chain, band_row, lambda: _init_state(m_ref, l_ref, acc_ref), finish_tile,
                s_refs[:1], s_refs[1:2], s_refs[2:])


def _fox_attention(qt, k, vt, ft4, kx):
    nq = SEQ // TQ
    return pl.pallas_call(
        _fox_attn_kernel,
        grid=(N_HEADS,),
        in_specs=_head_specs(1) + [pl.BlockSpec((1, nq, 1, TQ), lambda h: (h, 0, 0, 0)),
                                   pl.BlockSpec((1, SEQ, HEAD_DIM), lambda h: (h, 0, 0))],
        out_specs=pl.BlockSpec((SEQ, HEAD_DIM), lambda h: (0, h)),
        out_shape=jax.ShapeDtypeStruct((SEQ, WIDTH), _bf16),
        scratch_shapes=_head_scratch(1),
        compiler_params=_cparams(("parallel",)),
        name="fox_attention",
    )(qt, k, vt, ft4, kx)


def _merge_kernel(oa_ref, ob_ref, g_ref, x_ref, wbd_ref, wbf_ref, wout_ref, gm_ref, x1_ref, h2_ref):
    a = _dot(oa_ref[...], wbd_ref[...])
    b = _dot(ob_ref[...], wbf_ref[...])
    g = g_ref[...].astype(_f32)
    merged = g[:, :D_MODEL] * a + g[:, D_MODEL:] * b
    x1 = x_ref[...] + _dot(merged.astype(_bf16), wout_ref[...])
    x1_ref[...] = x1
    r = lax.rsqrt(jnp.mean(x1 * x1, axis=-1, keepdims=True) + EPS)
    h2_ref[...] = ((x1 * r) * gm_ref[...]).astype(h2_ref.dtype)


def _merge(oa, ob, gates, x, wbd, wbf, wout, gm, tm=256):
    s, d = x.shape
    const = lambda shape: pl.BlockSpec(shape, lambda i: (0, 0), pipeline_mode=pl.Buffered(1))
    return pl.pallas_call(
        _merge_kernel,
        grid=(s // tm,),
        in_specs=[pl.BlockSpec((tm, WIDTH), lambda i: (i, 0)),
                  pl.BlockSpec((tm, WIDTH), lambda i: (i, 0)),
                  pl.BlockSpec((tm, 2 * d), lambda i: (i, 0)),
                  pl.BlockSpec((tm, d), lambda i: (i, 0)),
                  const((WIDTH, d)), const((WIDTH, d)), const((d, d)),
                  pl.BlockSpec((1, d), lambda i: (0, 0))],
        out_specs=[pl.BlockSpec((tm, d), lambda i: (i, 0)),
                   pl.BlockSpec((tm, d), lambda i: (i, 0))],
        out_shape=[jax.ShapeDtypeStruct((s, d), _f32),
                   jax.ShapeDtypeStruct((s, d), _bf16)],
        compiler_params=_cparams(("parallel",)),
        name="merge_out_proj",
    )(oa, ob, gates, x, wbd, wbf, wout, gm.reshape(1, d))


def _mlp_kernel(h_ref, wu_ref, wd_ref, x_ref, o_ref):
    k = pl.program_id(1)

    @pl.when(k == 0)
    def _():
        o_ref[...] = x_ref[...]

    u = jnp.maximum(_dot(h_ref[...], wu_ref[...]), 0.0)
    o_ref[...] += _dot((u * u).astype(_bf16), wd_ref[...])


def _mlp(h2, wu, wd, x1, tm=512, tf=1024):
    s, d = x1.shape
    f = wu.shape[1]
    return pl.pallas_call(
        _mlp_kernel,
        grid=(s // tm, f // tf),
        in_specs=[pl.BlockSpec((tm, d), lambda i, k: (i, 0)),
                  pl.BlockSpec((d, tf), lambda i, k: (0, k)),
                  pl.BlockSpec((tf, d), lambda i, k: (k, 0)),
                  pl.BlockSpec((tm, d), lambda i, k: (i, 0))],
        out_specs=pl.BlockSpec((tm, d), lambda i, k: (i, 0)),
        out_shape=jax.ShapeDtypeStruct((s, d), _f32),
        compiler_params=_cparams(("parallel", "arbitrary")),
        name="mlp_relu2",
    )(h2, wu, wd, x1)


def kernel(x, norm_mix, w_in, b_forget, qnorm_diff, knorm_diff, lambda_q1, lambda_k1, lambda_q2, lambda_k2,
           subln_diff, qnorm_fox, knorm_fox, w_branch_diff, w_branch_fox, w_gate, b_gate, w_out, norm_mlp,
           w_mlp_up, w_mlp_down):
    assert x.shape == (1, SEQ, D_MODEL)
    x2 = x[0]
    w_in_t = jnp.swapaxes(w_in, 1, 2)

    bf_col = jnp.zeros((F_ROWS, 1), _f32).at[:N_HEADS, 0].set(b_forget[0])
    h, ft, kx_fox = _norm_and_forget(x2, norm_mix[0], w_in_t, bf_col)
    ft4 = ft.reshape(F_ROWS, SEQ // TQ, 1, TQ)

    g_qd = jnp.tile(qnorm_diff[0], 2).reshape(1, HEAD_DIM)
    g_kd = jnp.tile(knorm_diff[0], 2).reshape(1, HEAD_DIM)
    tm = TKM
    t_shape = jax.ShapeDtypeStruct((N_HEADS, SEQ // TQ, HEAD_DIM, TQ), _bf16)
    k_shape = jax.ShapeDtypeStruct((N_HEADS, SEQ, HEAD_DIM), _bf16)
    vt_shape = jax.ShapeDtypeStruct((N_HEADS, SEQ // TKM, V_ROWS, TKM), _bf16)
    vt_spec = pl.BlockSpec((N_HEADS, 1, V_ROWS, TKM), lambda i: (0, i, 0, 0))

    q1t, q2t = _proj_call(_proj_qdiff_kernel, h, w_in_t, 0, [g_qd], [t_shape, t_shape],
                          [_t_spec(tm), _t_spec(tm)], "proj_q_diff")
    ka = _proj_call(functools.partial(_proj_k_kernel, groups=2), h, w_in_t, 1, [g_kd], k_shape,
                    _k_spec(tm), "proj_k_diff")
    vat = _proj_call(_proj_vt_kernel, h, w_in_t, 2, [], vt_shape, vt_spec, "proj_v_diff")
    qbt = _proj_call(_proj_qfox_kernel, h, w_in_t, 3, [qnorm_fox[0].reshape(1, HEAD_DIM)], t_shape,
                     _t_spec(tm), "proj_q_fox")
    kb = _proj_call(functools.partial(_proj_k_kernel, groups=1), h, w_in_t, 4,
                    [knorm_fox[0].reshape(1, HEAD_DIM)], k_shape, _k_spec(tm), "proj_k_fox")
    vbt = _proj_call(_proj_vt_kernel, h, w_in_t, 5, [], vt_shape, vt_spec, "proj_v_fox")

    gates, (wbd, wbf, wout, wup, wdown) = _gates(
        h, w_gate, b_gate[0], [w_branch_diff, w_branch_fox, w_out, w_mlp_up, w_mlp_down])

    slopes = 2.0 ** (-8.0 * jnp.arange(1, N_HEADS + 1, dtype=_f32) / N_HEADS)
    row = lambda v: v[0].reshape(1, QK_DIFF)
    oa = _diff_attention(slopes, q1t, q2t, ka, vat, row(lambda_q1), row(lambda_k1), row(lambda_q2),
                         row(lambda_k2), subln_diff[0].reshape(HEAD_DIM, 1))
    ob = _fox_attention(qbt, kb, vbt, ft4, kx_fox)

    x1, h2 = _merge(oa, ob, gates, x2, wbd, wbf, wout, norm_mlp[0])
    out = _mlp(h2, wup, wdown, x1)
    return out[None]
```

```python
import functools

import jax
import jax.numpy as jnp
from jax import lax
from jax.experimental import pallas as pl
from jax.experimental.pallas import tpu as pltpu

D_MODEL = 2048
SEQ = 8192
HEAD_DIM = 128
N_HEADS = 8
QK_DIFF = 64
WIDTH = N_HEADS * HEAD_DIM
D_FF = 4 * D_MODEL
EPS = 1e-6
LAMBDA_INIT = 0.8 - 0.6 * 1.0
NEG_BIG = -1e30

TQ = 1024
TQS = 256
TK = 256
TKM = TQ
NSUB = TQ // TQS
LOG2E = 1.4426950408889634
V_ROWS = HEAD_DIM + 16
VMEM_LIMIT = 56 * 1024 * 1024

_f32 = jnp.float32
_bf16 = jnp.bfloat16


def _cparams(sem):
    return pltpu.CompilerParams(dimension_semantics=sem, vmem_limit_bytes=VMEM_LIMIT)


def _dot(a, b):
    return jnp.dot(a, b, preferred_element_type=_f32)


def _cast_slab_specs(weights, nsteps, step_of):
    in_specs, out_specs, out_shapes = [], [], []
    for w in weights:
        _, rows, cols = w.shape
        slab = rows // nsteps
        assert slab * nsteps == rows and slab % 16 == 0
        in_specs.append(pl.BlockSpec((None, slab, cols), lambda *ids: (0, step_of(*ids), 0)))
        out_specs.append(pl.BlockSpec((slab, cols), lambda *ids: (step_of(*ids), 0)))
        out_shapes.append(jax.ShapeDtypeStruct((rows, cols), _bf16))
    return in_specs, out_specs, out_shapes


def _cast_slabs(src_refs, dst_refs):
    for src, dst in zip(src_refs, dst_refs):
        dst[...] = src[...].astype(dst.dtype)


def _head_rmsnorm(y, gain_row, groups):
    y2 = y * y
    if groups == 1:
        r = lax.rsqrt(jnp.mean(y2, axis=-1, keepdims=True) + EPS)
    else:
        lane = lax.broadcasted_iota(jnp.int32, y.shape, 1)
        lo = lane < QK_DIFF
        s_lo = jnp.sum(jnp.where(lo, y2, 0.0), axis=-1, keepdims=True)
        s_hi = jnp.sum(jnp.where(lo, 0.0, y2), axis=-1, keepdims=True)
        r = jnp.where(lo, lax.rsqrt(s_lo * (1.0 / QK_DIFF) + EPS),
                      lax.rsqrt(s_hi * (1.0 / QK_DIFF) + EPS))
    return (y * r) * gain_row


PROJ_SUB = 256


def _proj_subtiles(a_ref, w_ref, wb_ref, first_step, epilogue, w_transposed=False, side_work=None):
    @pl.when(first_step)
    def _():
        w = w_ref[...].T if w_transposed else w_ref[...]
        wb_ref[...] = w.astype(wb_ref.dtype)

    if side_work is not None:
        side_work()
    for sub in range(a_ref.shape[0] // PROJ_SUB):
        rows = slice(sub * PROJ_SUB, (sub + 1) * PROJ_SUB)
        epilogue(rows, _dot(a_ref[rows, :], wb_ref[...]))


def _heads(acc):
    return [acc[:, hh * HEAD_DIM:(hh + 1) * HEAD_DIM] for hh in range(N_HEADS)]


def _proj_qdiff_kernel(a_ref, w_ref, g_ref, q1_ref, q2_ref, wb_ref):
    def epilogue(rows, acc):
        lo = lax.broadcasted_iota(jnp.int32, (PROJ_SUB, HEAD_DIM), 1) < QK_DIFF
        for hh, y in enumerate(_heads(acc)):
            yn = _head_rmsnorm(y, g_ref[...], 2) * (QK_DIFF ** -0.5 * LOG2E)
            q1_ref[hh, 0, :, rows] = jnp.where(lo, yn, 0.0).T.astype(q1_ref.dtype)
            q2_ref[hh, 0, :, rows] = jnp.where(lo, 0.0, yn).T.astype(q2_ref.dtype)

    _proj_subtiles(a_ref, w_ref, wb_ref, pl.program_id(0) == 0, epilogue, w_transposed=True)


def _proj_qfox_kernel(a_ref, w_ref, g_ref, q_ref, wb_ref):
    def epilogue(rows, acc):
        for hh, y in enumerate(_heads(acc)):
            yn = _head_rmsnorm(y, g_ref[...], 1) * (HEAD_DIM ** -0.5 * LOG2E)
            q_ref[hh, 0, :, rows] = yn.T.astype(q_ref.dtype)

    _proj_subtiles(a_ref, w_ref, wb_ref, pl.program_id(0) == 0, epilogue, w_transposed=True)


def _proj_k_kernel(a_ref, w_ref, g_ref, k_ref, wb_ref, *, groups):
    def epilogue(rows, acc):
        for hh, y in enumerate(_heads(acc)):
            k_ref[hh, rows, :] = _head_rmsnorm(y, g_ref[...], groups).astype(k_ref.dtype)

    _proj_subtiles(a_ref, w_ref, wb_ref, pl.program_id(0) == 0, epilogue, w_transposed=True)


def _proj_vt_kernel(a_ref, w_ref, vt_ref, wb_ref):
    row = lax.broadcasted_iota(jnp.int32, (V_ROWS - HEAD_DIM, TKM), 0)
    ones_rows = jnp.where(row == 0, 1.0, 0.0).astype(vt_ref.dtype)
    for hh in range(N_HEADS):
        vt_ref[hh, 0, HEAD_DIM:, :] = ones_rows

    def epilogue(rows, acc):
        for hh, y in enumerate(_heads(acc)):
            vt_ref[hh, 0, :HEAD_DIM, rows] = y.T.astype(vt_ref.dtype)

    _proj_subtiles(a_ref, w_ref, wb_ref, pl.program_id(0) == 0, epilogue, w_transposed=True)


def _proj_gates_kernel(a_ref, w_ref, b_ref, *refs, n_cast):
    cast_in, o_ref, cast_out, wb_ref = refs[:n_cast], refs[n_cast], refs[n_cast + 1:-1], refs[-1]

    def epilogue(rows, acc):
        o_ref[rows, :] = (1.0 / (1.0 + jnp.exp(-(acc + b_ref[...])))).astype(o_ref.dtype)

    _proj_subtiles(a_ref, w_ref, wb_ref, pl.program_id(1) == 0, epilogue,
                   side_work=lambda: _cast_slabs(cast_in, cast_out))


def _proj_call(kernel, h, w, col_block, extras, out_shapes, out_specs, name, tm=TKM):
    s, d = h.shape
    in_specs = [pl.BlockSpec((tm, d), lambda i: (i, 0)),
                pl.BlockSpec((None, WIDTH, d), lambda i: (0, col_block, 0), pipeline_mode=pl.Buffered(1))]
    in_specs += [pl.BlockSpec(e.shape, lambda i: (0, 0)) for e in extras]
    return pl.pallas_call(
        kernel,
        grid=(s // tm,),
        in_specs=in_specs,
        out_specs=out_specs,
        out_shape=out_shapes,
        scratch_shapes=[pltpu.VMEM((d, WIDTH), _bf16)],
        compiler_params=_cparams(("arbitrary",)),
        name=name,
    )(h, w, *extras)


def _t_spec(tm):
    assert tm == TQ
    return pl.BlockSpec((N_HEADS, 1, HEAD_DIM, tm), lambda i: (0, i, 0, 0))


def _k_spec(tm):
    return pl.BlockSpec((N_HEADS, tm, HEAD_DIM), lambda i: (0, i, 0))


def _gates(h, w, b, cast_weights, tm=1024, tn=1024):
    s, d = h.shape
    n = w.shape[2]
    ni = s // tm
    cast_in, cast_out, cast_shapes = _cast_slab_specs(cast_weights, (n // tn) * ni, lambda j, i: j * ni + i)
    outs = pl.pallas_call(
        functools.partial(_proj_gates_kernel, n_cast=len(cast_weights)),
        grid=(n // tn, ni),
        in_specs=[pl.BlockSpec((tm, d), lambda j, i: (i, 0)),
                  pl.BlockSpec((None, d, tn), lambda j, i: (0, 0, j)),
                  pl.BlockSpec((1, tn), lambda j, i: (0, j))] + cast_in,
        out_specs=[pl.BlockSpec((tm, tn), lambda j, i: (i, j))] + cast_out,
        out_shape=[jax.ShapeDtypeStruct((s, n), _bf16)] + cast_shapes,
        scratch_shapes=[pltpu.VMEM((d, tn), _bf16)],
        compiler_params=_cparams(("arbitrary", "arbitrary")),
        name="proj_gates",
    )(h, w, b.reshape(1, n), *cast_weights)
    return outs[0], outs[1:]


F_ROWS = 16
F_CHUNK = 256


def _split3(x):
    hi = x.astype(_bf16)
    r1 = x - hi.astype(_f32)
    mid = r1.astype(_bf16)
    lo = (r1 - mid.astype(_f32)).astype(_bf16)
    return hi, mid, lo


def _select_rows_or_lanes(index, values):
    out = jnp.zeros(index.shape, _f32)
    for i in reversed(range(len(values))):
        out = jnp.where(index == i, values[i].astype(_f32), out)
    return out


def _forget_kernel(wt_ref, x_ref, g_ref, b_ref, h_ref, ft_ref, kx_ref, carry_ref):
    i = pl.program_id(0)

    @pl.when(i == 0)
    def _():
        carry_ref[...] = jnp.zeros_like(carry_ref)

    x = x_ref[...]
    r = lax.rsqrt(jnp.mean(x * x, axis=-1, keepdims=True) + EPS)
    h = ((x * r) * g_ref[...]).astype(h_ref.dtype)
    h_ref[...] = h
    wt8 = wt_ref[...]
    wt = jnp.concatenate([wt8, jnp.zeros_like(wt8)], axis=0).astype(_bf16)
    z = lax.dot_general(wt, h, (((1,), (1,)), ((), ())),
                        preferred_element_type=_f32) + b_ref[...]
    logf = (jnp.minimum(z, 0.0) - jnp.log(1.0 + jnp.exp(-jnp.abs(z)))) * LOG2E
    tm = logf.shape[1]
    r = lax.broadcasted_iota(jnp.int32, (F_CHUNK, F_CHUNK), 0)
    c = lax.broadcasted_iota(jnp.int32, (F_CHUNK, F_CHUNK), 1)
    upper = jnp.where(r <= c, 1.0, 0.0).astype(_bf16)
    carry = carry_ref[...]
    lane = lax.broadcasted_iota(jnp.int32, (HEAD_DIM, HEAD_DIM), 1)
    for ch in range(tm // F_CHUNK):
        x = logf[:, ch * F_CHUNK:(ch + 1) * F_CHUNK]
        hi, mid, lo = _split3(x)
        pre = _dot(hi, upper) + _dot(mid, upper) + _dot(lo, upper) + carry
        ft_ref[:, ch * F_CHUNK:(ch + 1) * F_CHUNK] = pre
        carry = pre[:, F_CHUNK - 1:F_CHUNK]
        for hh in range(N_HEADS):
            for sub in range(F_CHUNK // HEAD_DIM):
                row = pre[hh:hh + 1, sub * HEAD_DIM:(sub + 1) * HEAD_DIM]
                neg = -jnp.broadcast_to(row, (HEAD_DIM, HEAD_DIM)).T
                base = ch * F_CHUNK + sub * HEAD_DIM
                kx_ref[hh, base:base + HEAD_DIM, :] = _select_rows_or_lanes(
                    lane, _split3(neg)).astype(kx_ref.dtype)
    carry_ref[...] = carry


def _norm_and_forget(x, g, w_in_t, b_col, tm=1024):
    s, d = x.shape
    f_block = 6 * WIDTH // N_HEADS
    return pl.pallas_call(
        _forget_kernel,
        grid=(s // tm,),
        in_specs=[pl.BlockSpec((None, N_HEADS, d), lambda i: (0, f_block, 0)),
                  pl.BlockSpec((tm, d), lambda i: (i, 0)),
                  pl.BlockSpec((1, d), lambda i: (0, 0)),
                  pl.BlockSpec((F_ROWS, 1), lambda i: (0, 0))],
        out_specs=[pl.BlockSpec((tm, d), lambda i: (i, 0)),
                   pl.BlockSpec((F_ROWS, tm), lambda i: (0, i)),
                   pl.BlockSpec((N_HEADS, tm, HEAD_DIM), lambda i: (0, i, 0))],
        out_shape=[jax.ShapeDtypeStruct((s, d), _bf16),
                   jax.ShapeDtypeStruct((F_ROWS, s), _f32),
                   jax.ShapeDtypeStruct((N_HEADS, s, HEAD_DIM), _bf16)],
        scratch_shapes=[pltpu.VMEM((F_ROWS, 1), _f32)],
        compiler_params=_cparams(("arbitrary",)),
        name="norm_forget_gate",
    )(w_in_t, x, g.reshape(1, d), b_col)


def _block_partial(z, vt):
    mloc = jnp.max(z, axis=0, keepdims=True)
    o = _dot(vt, jnp.exp2(z - mloc).astype(_bf16))
    return mloc, o[HEAD_DIM:HEAD_DIM + 1], o[:HEAD_DIM]


def _merge_partials(m_ref, l_ref, acc_ref, idx, c, parts):
    cs = slice(c * TQS, (c + 1) * TQS)
    m_old = m_ref[idx, :, cs]
    m_new = m_old
    for mt, _, _ in parts:
        m_new = jnp.maximum(m_new, mt)
    a = jnp.exp2(m_old - m_new)
    l_new = a * l_ref[idx, :, cs]
    acc_new = a * acc_ref[idx, :, cs]
    for mt, l, o in parts:
        b = jnp.exp2(mt - m_new)
        l_new = l_new + b * l
        acc_new = acc_new + b * o
    m_ref[idx, :, cs] = m_new
    l_ref[idx, :, cs] = l_new
    acc_ref[idx, :, cs] = acc_new


def _init_state(m_ref, l_ref, acc_ref):
    m_ref[...] = jnp.full_like(m_ref, NEG_BIG)
    l_ref[...] = jnp.zeros_like(l_ref)
    acc_ref[...] = jnp.zeros_like(acc_ref)


def _head_sweep(scores_cols, main_chain, band_row, start_tile, finish_tile, sa_ref, sb_ref, sc_ref):
    n_maps = len(sa_ref)
    n_tiles = SEQ // TQ

    def fill_cols(refs, tile, block, c):
        for ref, val in zip(refs, scores_cols(tile, block, c)):
            ref[:, c * TQS:(c + 1) * TQS] = val

    def overlapped(tile, cur_block, cur_refs, nxt_block, nxt_refs):
        for c in range(NSUB):
            for idx in range(n_maps):
                nxt_refs[idx][:, c * TQS:(c + 1) * TQS] = scores_cols(tile, nxt_block, c)[idx]
                main_chain(tile, cur_block, cur_refs, c, idx)

    for c in range(NSUB):
        fill_cols(sc_ref, 0, 0, c)
    start_tile()

    def tile_body(qi, carry):
        def pair(t, carry2):
            overlapped(qi, 2 * t, sa_ref, 2 * t + 1, sb_ref)
            overlapped(qi, 2 * t + 1, sb_ref, 2 * t + 2, sa_ref)
            return carry2

        n_pairs = lax.shift_right_logical(jnp.maximum(qi - 1, 0), 1)
        lax.fori_loop(0, n_pairs, pair, 0)
        last = 2 * n_pairs

        @pl.when(qi - last == 2)
        def _():
            overlapped(qi, last, sa_ref, last + 1, sb_ref)
            overlapped(qi, last + 1, sb_ref, qi, sc_ref)

        @pl.when(qi - last == 1)
        def _():
            overlapped(qi, last, sa_ref, qi, sc_ref)

        nxt = jnp.minimum(qi + 1, n_tiles - 1)
        for j in range(NSUB):
            fill_cols(sa_ref, nxt, 0, j)
            band_row(qi, j, sc_ref)
        finish_tile(qi)
        start_tile()
        return carry

    lax.fori_loop(0, n_tiles, tile_body, 0)


def _local_iotas():
    lane = lax.broadcasted_iota(jnp.int32, (TK, TQS), 1)
    sub = lax.broadcasted_iota(jnp.int32, (TK, TQS), 0)
    return lane, sub


def _diff_attn_kernel(slope_ref, q1_ref, q2_ref, k_ref, vt_ref, lq1_ref, lk1_ref, lq2_ref, lk2_ref,
                      gsub_ref, o_ref, m_ref, l_ref, acc_ref, *s_refs):
    h = pl.program_id(0)
    neg_slope = -slope_ref[h] * LOG2E
    q_refs = (q1_ref, q2_ref)

    slope2 = -neg_slope
    krow = lax.broadcasted_iota(jnp.int32, (TKM, HEAD_DIM), 0)
    klane = lax.broadcasted_iota(jnp.int32, (TKM, HEAD_DIM), 1)
    kx = jnp.where(klane < 3, (krow // HEAD_DIM).astype(_f32),
                   jnp.where(klane < 6, (krow % HEAD_DIM).astype(_f32), 0.0)).astype(_bf16)
    qrow = lax.broadcasted_iota(jnp.int32, (HEAD_DIM, TQS), 0)
    ones = jnp.ones((HEAD_DIM, TQS), _f32)
    qx = _select_rows_or_lanes(qrow, _split3(ones * (slope2 * HEAD_DIM)) + _split3(ones * slope2)).astype(_bf16)
    t_loc = lax.broadcasted_iota(jnp.int32, (1, TQS), 1).astype(_f32)
    lane, sub = _local_iotas()
    fix_diag = jnp.where((sub // 64) <= (lane // 64),
                         (sub.astype(_f32) + jnp.abs((lane - sub).astype(_f32))) * neg_slope, NEG_BIG)
    lam = (jnp.exp(jnp.sum(lq1_ref[...] * lk1_ref[...], axis=-1, keepdims=True))
           - jnp.exp(jnp.sum(lq2_ref[...] * lk2_ref[...], axis=-1, keepdims=True))
           + LAMBDA_INIT)

    def scores_cols(tile, block, c):
        start = pl.multiple_of(block * TKM, TKM)
        k_aug = jnp.concatenate([k_ref[0, pl.ds(start, TKM), :], kx], axis=1)
        cs = slice(c * TQS, (c + 1) * TQS)
        return tuple(_dot(k_aug, jnp.concatenate([q_ref[0, tile, :, cs], qx], axis=0)) for q_ref in q_refs)

    def query_shift(tile, block, c):
        return (jnp.asarray((tile - block) * TQ + c * TQS, _f32) + t_loc) * neg_slope

    def main_chain(tile, block, s_refs, c, idx):
        vt = vt_ref[0, block]
        mloc, l, o = _block_partial(s_refs[idx][:, c * TQS:(c + 1) * TQS], vt)
        _merge_partials(m_ref, l_ref, acc_ref, idx, c, [(mloc + query_shift(tile, block, c), l, o)])

    def band_row(tile, j, s_refs):
        vt = vt_ref[0, tile, :, j * TK:(j + 1) * TK]
        for idx in range(2):
            for c in range(j, NSUB):
                z = s_refs[idx][j * TK:(j + 1) * TK, c * TQS:(c + 1) * TQS]
                if c == j:
                    mloc, l, o = _block_partial(z + fix_diag, vt)
                    part = (mloc + (j * TK) * neg_slope, l, o)
                else:
                    mloc, l, o = _block_partial(z, vt)
                    part = (mloc + query_shift(tile, tile, c), l, o)
                _merge_partials(m_ref, l_ref, acc_ref, idx, c, [part])

    def finish_tile(tile):
        o = acc_ref[0] / l_ref[0] - lam * (acc_ref[1] / l_ref[1])
        r = lax.rsqrt(jnp.mean(o * o, axis=0, keepdims=True) + EPS)
        y = (o * r) * gsub_ref[...] * (1.0 - LAMBDA_INIT)
        o_ref[pl.ds(pl.multiple_of(tile * TQ, TQ), TQ), :] = y.T.astype(o_ref.dtype)

    n = len(s_refs) // 3
    _head_sweep(scores_cols, main_chain, band_row, lambda: _init_state(m_ref, l_ref, acc_ref), finish_tile,
                s_refs[:n], s_refs[n:2 * n], s_refs[2 * n:])


def _head_specs(n_q):
    nq = SEQ // TQ
    q_spec = pl.BlockSpec((1, nq, HEAD_DIM, TQ), lambda h: (h, 0, 0, 0))
    return [q_spec] * n_q + [pl.BlockSpec((1, SEQ, HEAD_DIM), lambda h: (h, 0, 0)),
                             pl.BlockSpec((1, SEQ // TKM, V_ROWS, TKM), lambda h: (h, 0, 0, 0))]


def _head_scratch(n_maps):
    return [pltpu.VMEM((n_maps, 1, TQ), _f32),
            pltpu.VMEM((n_maps, 1, TQ), _f32),
            pltpu.VMEM((n_maps, HEAD_DIM, TQ), _f32)] + [pltpu.VMEM((TKM, TQ), _f32)] * (3 * n_maps)


def _diff_attention(slopes, q1t, q2t, k, vt, lq1, lk1, lq2, lk2, gsub_col):
    vec = lambda: pl.BlockSpec((1, QK_DIFF), lambda h: (0, 0))
    q_specs = _head_specs(2)
    return pl.pallas_call(
        _diff_attn_kernel,
        grid=(N_HEADS,),
        in_specs=[pl.BlockSpec(memory_space=pltpu.SMEM)] + q_specs + [
            vec(), vec(), vec(), vec(), pl.BlockSpec((HEAD_DIM, 1), lambda h: (0, 0))],
        out_specs=pl.BlockSpec((SEQ, HEAD_DIM), lambda h: (0, h)),
        out_shape=jax.ShapeDtypeStruct((SEQ, WIDTH), _bf16),
        scratch_shapes=_head_scratch(2),
        compiler_params=_cparams(("parallel",)),
        name="diff_attention",
    )(slopes, q1t, q2t, k, vt, lq1, lk1, lq2, lk2, gsub_col)


def _fox_attn_kernel(q_ref, k_ref, vt_ref, ft_ref, kx_ref, o_ref, m_ref, l_ref, acc_ref, *s_refs):
    qrow = lax.broadcasted_iota(jnp.int32, (HEAD_DIM, TQS), 0)
    ones_rows = jnp.where(qrow < 3, 1.0, 0.0).astype(_bf16)
    lane, sub = _local_iotas()
    causal = sub <= lane

    def scores_cols(tile, block, c):
        start = pl.multiple_of(block * TKM, TKM)
        k_aug = jnp.concatenate([k_ref[0, pl.ds(start, TKM), :], kx_ref[0, pl.ds(start, TKM), :]], axis=1)
        q_aug = jnp.concatenate([q_ref[0, tile, :, c * TQS:(c + 1) * TQS], ones_rows], axis=0)
        return (_dot(k_aug, q_aug),)

    def main_chain(tile, block, s_ref, c, idx):
        cs = slice(c * TQS, (c + 1) * TQS)
        mloc, l, o = _block_partial(s_ref[0][:, cs], vt_ref[0, block])
        _merge_partials(m_ref, l_ref, acc_ref, 0, c, [(mloc + ft_ref[0, tile, :, cs], l, o)])

    def band_row(tile, j, s_ref):
        vt = vt_ref[0, tile, :, j * TK:(j + 1) * TK]
        for c in range(j, NSUB):
            cs = slice(c * TQS, (c + 1) * TQS)
            z = s_ref[0][j * TK:(j + 1) * TK, cs]
            if c == j:
                z = jnp.where(causal, z, NEG_BIG)
            mloc, l, o = _block_partial(z, vt)
            _merge_partials(m_ref, l_ref, acc_ref, 0, c, [(mloc + ft_ref[0, tile, :, cs], l, o)])

    def finish_tile(tile):
        o = acc_ref[0] / l_ref[0]
        o_ref[pl.ds(pl.multiple_of(tile * TQ, TQ), TQ), :] = o.T.astype(o_ref.dtype)

    _head_sweep(scores_cols, main_chain, band_row, lambda: _init_state(m_ref, l_ref, acc_ref), finish_tile,
                s_refs[:1], s_refs[1:2], s_refs[2:])


def _fox_attention(qt, k, vt, ft4, kx):
    nq = SEQ // TQ
    return pl.pallas_call(
        _fox_attn_kernel,
        grid=(N_HEADS,),
        in_specs=_head_specs(1) + [pl.BlockSpec((1, nq, 1, TQ), lambda h: (h, 0, 0, 0)),
                                   pl.BlockSpec((1, SEQ, HEAD_DIM), lambda h: (h, 0, 0))],
        out_specs=pl.BlockSpec((SEQ, HEAD_DIM), lambda h: (0, h)),
        out_shape=jax.ShapeDtypeStruct((SEQ, WIDTH), _bf16),
        scratch_shapes=_head_scratch(1),
        compiler_params=_cparams(("parallel",)),
        name="fox_attention",
    )(qt, k, vt, ft4, kx)


def _merge_kernel(oa_ref, ob_ref, g_ref, x_ref, wbd_ref, wbf_ref, wout_ref, gm_ref, x1_ref, h2_ref):
    a = _dot(oa_ref[...], wbd_ref[...])
    b = _dot(ob_ref[...], wbf_ref[...])
    g = g_ref[...].astype(_f32)
    merged = g[:, :D_MODEL] * a + g[:, D_MODEL:] * b
    x1 = x_ref[...] + _dot(merged.astype(_bf16), wout_ref[...])
    x1_ref[...] = x1
    r = lax.rsqrt(jnp.mean(x1 * x1, axis=-1, keepdims=True) + EPS)
    h2_ref[...] = ((x1 * r) * gm_ref[...]).astype(h2_ref.dtype)


def _merge(oa, ob, gates, x, wbd, wbf, wout, gm, tm=256):
    s, d = x.shape
    const = lambda shape: pl.BlockSpec(shape, lambda i: (0, 0), pipeline_mode=pl.Buffered(1))
    return pl.pallas_call(
        _merge_kernel,
        grid=(s // tm,),
        in_specs=[pl.BlockSpec((tm, WIDTH), lambda i: (i, 0)),
                  pl.BlockSpec((tm, WIDTH), lambda i: (i, 0)),
                  pl.BlockSpec((tm, 2 * d), lambda i: (i, 0)),
                  pl.BlockSpec((tm, d), lambda i: (i, 0)),
                  const((WIDTH, d)), const((WIDTH, d)), const((d, d)),
                  pl.BlockSpec((1, d), lambda i: (0, 0))],
        out_specs=[pl.BlockSpec((tm, d), lambda i: (i, 0)),
                   pl.BlockSpec((tm, d), lambda i: (i, 0))],
        out_shape=[jax.ShapeDtypeStruct((s, d), _f32),
                   jax.ShapeDtypeStruct((s, d), _bf16)],
        compiler_params=_cparams(("parallel",)),
        name="merge_out_proj",
    )(oa, ob, gates, x, wbd, wbf, wout, gm.reshape(1, d))


def _mlp_kernel(h_ref, wu_ref, wd_ref, x_ref, o_ref):
    k = pl.program_id(1)

    @pl.when(k == 0)
    def _():
        o_ref[...] = x_ref[...]

    u = jnp.maximum(_dot(h_ref[...], wu_ref[...]), 0.0)
    o_ref[...] += _dot((u * u).astype(_bf16), wd_ref[...])


def _mlp(h2, wu, wd, x1, tm=512, tf=1024):
    s, d = x1.shape
    f = wu.shape[1]
    return pl.pallas_call(
        _mlp_kernel,
        grid=(s // tm, f // tf),
        in_specs=[pl.BlockSpec((tm, d), lambda i, k: (i, 0)),
                  pl.BlockSpec((d, tf), lambda i, k: (0, k)),
                  pl.BlockSpec((tf, d), lambda i, k: (k, 0)),
                  pl.BlockSpec((tm, d), lambda i, k: (i, 0))],
        out_specs=pl.BlockSpec((tm, d), lambda i, k: (i, 0)),
        out_shape=jax.ShapeDtypeStruct((s, d), _f32),
        compiler_params=_cparams(("parallel", "arbitrary")),
        name="mlp_relu2",
    )(h2, wu, wd, x1)


def kernel(x, norm_mix, w_in, b_forget, qnorm_diff, knorm_diff, lambda_q1, lambda_k1, lambda_q2, lambda_k2,
           subln_diff, qnorm_fox, knorm_fox, w_branch_diff, w_branch_fox, w_gate, b_gate, w_out, norm_mlp,
           w_mlp_up, w_mlp_down):
    assert x.shape == (1, SEQ, D_MODEL)
    x2 = x[0]
    w_in_t = jnp.swapaxes(w_in, 1, 2)

    bf_col = jnp.zeros((F_ROWS, 1), _f32).at[:N_HEADS, 0].set(b_forget[0])
    h, ft, kx_fox = _norm_and_forget(x2, norm_mix[0], w_in_t, bf_col)
    ft4 = ft.reshape(F_ROWS, SEQ // TQ, 1, TQ)

    g_qd = jnp.tile(qnorm_diff[0], 2).reshape(1, HEAD_DIM)
    g_kd = jnp.tile(knorm_diff[0], 2).reshape(1, HEAD_DIM)
    tm = TKM
    t_shape = jax.ShapeDtypeStruct((N_HEADS, SEQ // TQ, HEAD_DIM, TQ), _bf16)
    k_shape = jax.ShapeDtypeStruct((N_HEADS, SEQ, HEAD_DIM), _bf16)
    vt_shape = jax.ShapeDtypeStruct((N_HEADS, SEQ // TKM, V_ROWS, TKM), _bf16)
    vt_spec = pl.BlockSpec((N_HEADS, 1, V_ROWS, TKM), lambda i: (0, i, 0, 0))

    q1t, q2t = _proj_call(_proj_qdiff_kernel, h, w_in_t, 0, [g_qd], [t_shape, t_shape],
                          [_t_spec(tm), _t_spec(tm)], "proj_q_diff")
    ka = _proj_call(functools.partial(_proj_k_kernel, groups=2), h, w_in_t, 1, [g_kd], k_shape,
                    _k_spec(tm), "proj_k_diff")
    vat = _proj_call(_proj_vt_kernel, h, w_in_t, 2, [], vt_shape, vt_spec, "proj_v_diff")
    qbt = _proj_call(_proj_qfox_kernel, h, w_in_t, 3, [qnorm_fox[0].reshape(1, HEAD_DIM)], t_shape,
                     _t_spec(tm), "proj_q_fox")
    kb = _proj_call(functools.partial(_proj_k_kernel, groups=1), h, w_in_t, 4,
                    [knorm_fox[0].reshape(1, HEAD_DIM)], k_shape, _k_spec(tm), "proj_k_fox")
    vbt = _proj_call(_proj_vt_kernel, h, w_in_t, 5, [], vt_shape, vt_spec, "proj_v_fox")

    gates, (wbd, wbf, wout, wup, wdown) = _gates(
        h, w_gate, b_gate[0], [w_branch_diff, w_branch_fox, w_out, w_mlp_up, w_mlp_down])

    slopes = 2.0 ** (-8.0 * jnp.arange(1, N_HEADS + 1, dtype=_f32) / N_HEADS)
    row = lambda v: v[0].reshape(1, QK_DIFF)
    oa = _diff_attention(slopes, q1t, q2t, ka, vat, row(lambda_q1), row(lambda_k1), row(lambda_q2),
                         row(lambda_k2), subln_diff[0].reshape(HEAD_DIM, 1))
    ob = _fox_attention(qbt, kb, vbt, ft4, kx_fox)

    x1, h2 = _merge(oa, ob, gates, x2, wbd, wbf, wout, norm_mlp[0])
    out = _mlp(h2, wup, wdown, x1)
    return out[None]
```

```python
import functools

import jax
import jax.numpy as jnp
from jax import lax
from jax.experimental import pallas as pl
from jax.experimental.pallas import tpu as pltpu

D_MODEL = 2048
SEQ = 8192
HEAD_DIM = 128
N_HEADS = 8
QK_DIFF = 64
WIDTH = N_HEADS * HEAD_DIM
D_FF = 4 * D_MODEL
EPS = 1e-6
LAMBDA_INIT = 0.8 - 0.6 * 1.0
NEG_BIG = -1e30

TQ = 1024
TQS = 256
TK = 256
TKM = TQ
NSUB = TQ // TQS
LOG2E = 1.4426950408889634
V_ROWS = HEAD_DIM + 16
VMEM_LIMIT = 56 * 1024 * 1024

_f32 = jnp.float32
_bf16 = jnp.bfloat16


def _cparams(sem):
    return pltpu.CompilerParams(dimension_semantics=sem, vmem_limit_bytes=VMEM_LIMIT)


def _dot(a, b):
    return jnp.dot(a, b, preferred_element_type=_f32)


def _cast_slab_specs(weights, nsteps, step_of):
    in_specs, out_specs, out_shapes = [], [], []
    for w in weights:
        _, rows, cols = w.shape
        slab = rows // nsteps
        assert slab * nsteps == rows and slab % 16 == 0
        in_specs.append(pl.BlockSpec((None, slab, cols), lambda *ids: (0, step_of(*ids), 0)))
        out_specs.append(pl.BlockSpec((slab, cols), lambda *ids: (step_of(*ids), 0)))
        out_shapes.append(jax.ShapeDtypeStruct((rows, cols), _bf16))
    return in_specs, out_specs, out_shapes


def _cast_slabs(src_refs, dst_refs):
    for src, dst in zip(src_refs, dst_refs):
        dst[...] = src[...].astype(dst.dtype)


def _head_rmsnorm(y, gain_row, groups):
    y2 = y * y
    if groups == 1:
        r = lax.rsqrt(jnp.mean(y2, axis=-1, keepdims=True) + EPS)
    else:
        lane = lax.broadcasted_iota(jnp.int32, y.shape, 1)
        lo = lane < QK_DIFF
        s_lo = jnp.sum(jnp.where(lo, y2, 0.0), axis=-1, keepdims=True)
        s_hi = jnp.sum(jnp.where(lo, 0.0, y2), axis=-1, keepdims=True)
        r = jnp.where(lo, lax.rsqrt(s_lo * (1.0 / QK_DIFF) + EPS),
                      lax.rsqrt(s_hi * (1.0 / QK_DIFF) + EPS))
    return (y * r) * gain_row


PROJ_SUB = 256


def _proj_subtiles(a_ref, w_ref, wb_ref, first_step, epilogue, w_transposed=False, side_work=None):
    @pl.when(first_step)
    def _():
        w = w_ref[...].T if w_transposed else w_ref[...]
        wb_ref[...] = w.astype(wb_ref.dtype)

    if side_work is not None:
        side_work()
    for sub in range(a_ref.shape[0] // PROJ_SUB):
        rows = slice(sub * PROJ_SUB, (sub + 1) * PROJ_SUB)
        epilogue(rows, _dot(a_ref[rows, :], wb_ref[...]))


def _heads(acc):
    return [acc[:, hh * HEAD_DIM:(hh + 1) * HEAD_DIM] for hh in range(N_HEADS)]


def _proj_qdiff_kernel(a_ref, w_ref, g_ref, q1_ref, q2_ref, wb_ref):
    def epilogue(rows, acc):
        lo = lax.broadcasted_iota(jnp.int32, (PROJ_SUB, HEAD_DIM), 1) < QK_DIFF
        for hh, y in enumerate(_heads(acc)):
            yn = _head_rmsnorm(y, g_ref[...], 2) * (QK_DIFF ** -0.5 * LOG2E)
            q1_ref[hh, 0, :, rows] = jnp.where(lo, yn, 0.0).T.astype(q1_ref.dtype)
            q2_ref[hh, 0, :, rows] = jnp.where(lo, 0.0, yn).T.astype(q2_ref.dtype)

    _proj_subtiles(a_ref, w_ref, wb_ref, pl.program_id(0) == 0, epilogue, w_transposed=True)


def _proj_qfox_kernel(a_ref, w_ref, g_ref, q_ref, wb_ref):
    def epilogue(rows, acc):
        for hh, y in enumerate(_heads(acc)):
            yn = _head_rmsnorm(y, g_ref[...], 1) * (HEAD_DIM ** -0.5 * LOG2E)
            q_ref[hh, 0, :, rows] = yn.T.astype(q_ref.dtype)

    _proj_subtiles(a_ref, w_ref, wb_ref, pl.program_id(0) == 0, epilogue, w_transposed=True)


def _proj_k_kernel(a_ref, w_ref, g_ref, k_ref, wb_ref, *, groups):
    def epilogue(rows, acc):
        for hh, y in enumerate(_heads(acc)):
            k_ref[hh, rows, :] = _head_rmsnorm(y, g_ref[...], groups).astype(k_ref.dtype)

    _proj_subtiles(a_ref, w_ref, wb_ref, pl.program_id(0) == 0, epilogue, w_transposed=True)


def _proj_vt_kernel(a_ref, w_ref, vt_ref, wb_ref):
    row = lax.broadcasted_iota(jnp.int32, (V_ROWS - HEAD_DIM, TKM), 0)
    ones_rows = jnp.where(row == 0, 1.0, 0.0).astype(vt_ref.dtype)
    for hh in range(N_HEADS):
        vt_ref[hh, 0, HEAD_DIM:, :] = ones_rows

    def epilogue(rows, acc):
        for hh, y in enumerate(_heads(acc)):
            vt_ref[hh, 0, :HEAD_DIM, rows] = y.T.astype(vt_ref.dtype)

    _proj_subtiles(a_ref, w_ref, wb_ref, pl.program_id(0) == 0, epilogue, w_transposed=True)


def _proj_gates_kernel(a_ref, w_ref, b_ref, *refs, n_cast):
    cast_in, o_ref, cast_out, wb_ref = refs[:n_cast], refs[n_cast], refs[n_cast + 1:-1], refs[-1]

    def epilogue(rows, acc):
        o_ref[rows, :] = (1.0 / (1.0 + jnp.exp(-(acc + b_ref[...])))).astype(o_ref.dtype)

    _proj_subtiles(a_ref, w_ref, wb_ref, pl.program_id(1) == 0, epilogue,
                   side_work=lambda: _cast_slabs(cast_in, cast_out))


def _proj_call(kernel, h, w, col_block, extras, out_shapes, out_specs, name, tm=TKM):
    s, d = h.shape
    in_specs = [pl.BlockSpec((tm, d), lambda i: (i, 0)),
                pl.BlockSpec((None, WIDTH, d), lambda i: (0, col_block, 0), pipeline_mode=pl.Buffered(1))]
    in_specs += [pl.BlockSpec(e.shape, lambda i: (0, 0)) for e in extras]
    return pl.pallas_call(
        kernel,
        grid=(s // tm,),
        in_specs=in_specs,
        out_specs=out_specs,
        out_shape=out_shapes,
        scratch_shapes=[pltpu.VMEM((d, WIDTH), _bf16)],
        compiler_params=_cparams(("arbitrary",)),
        name=name,
    )(h, w, *extras)


def _t_spec(tm):
    assert tm == TQ
    return pl.BlockSpec((N_HEADS, 1, HEAD_DIM, tm), lambda i: (0, i, 0, 0))


def _k_spec(tm):
    return pl.BlockSpec((N_HEADS, tm, HEAD_DIM), lambda i: (0, i, 0))


def _gates(h, w, b, cast_weights, tm=1024, tn=1024):
    s, d = h.shape
    n = w.shape[2]
    ni = s // tm
    cast_in, cast_out, cast_shapes = _cast_slab_specs(cast_weights, (n // tn) * ni, lambda j, i: j * ni + i)
    outs = pl.pallas_call(
        functools.partial(_proj_gates_kernel, n_cast=len(cast_weights)),
        grid=(n // tn, ni),
        in_specs=[pl.BlockSpec((tm, d), lambda j, i: (i, 0)),
                  pl.BlockSpec((None, d, tn), lambda j, i: (0, 0, j)),
                  pl.BlockSpec((1, tn), lambda j, i: (0, j))] + cast_in,
        out_specs=[pl.BlockSpec((tm, tn), lambda j, i: (i, j))] + cast_out,
        out_shape=[jax.ShapeDtypeStruct((s, n), _bf16)] + cast_shapes,
        scratch_shapes=[pltpu.VMEM((d, tn), _bf16)],
        compiler_params=_cparams(("arbitrary", "arbitrary")),
        name="proj_gates",
    )(h, w, b.reshape(1, n), *cast_weights)
    return outs[0], outs[1:]


F_ROWS = 16
F_CHUNK = 256


def _split3(x):
    hi = x.astype(_bf16)
    r1 = x - hi.astype(_f32)
    mid = r1.astype(_bf16)
    lo = (r1 - mid.astype(_f32)).astype(_bf16)
    return hi, mid, lo


def _select_rows_or_lanes(index, values):
    out = jnp.zeros(index.shape, _f32)
    for i in reversed(range(len(values))):
        out = jnp.where(index == i, values[i].astype(_f32), out)
    return out


def _forget_kernel(wt_ref, x_ref, g_ref, b_ref, h_ref, ft_ref, kx_ref, carry_ref):
    i = pl.program_id(0)

    @pl.when(i == 0)
    def _():
        carry_ref[...] = jnp.zeros_like(carry_ref)

    x = x_ref[...]
    r = lax.rsqrt(jnp.mean(x * x, axis=-1, keepdims=True) + EPS)
    h = ((x * r) * g_ref[...]).astype(h_ref.dtype)
    h_ref[...] = h
    wt8 = wt_ref[...]
    wt = jnp.concatenate([wt8, jnp.zeros_like(wt8)], axis=0).astype(_bf16)
    z = lax.dot_general(wt, h, (((1,), (1,)), ((), ())),
                        preferred_element_type=_f32) + b_ref[...]
    logf = (jnp.minimum(z, 0.0) - jnp.log(1.0 + jnp.exp(-jnp.abs(z)))) * LOG2E
    tm = logf.shape[1]
    r = lax.broadcasted_iota(jnp.int32, (F_CHUNK, F_CHUNK), 0)
    c = lax.broadcasted_iota(jnp.int32, (F_CHUNK, F_CHUNK), 1)
    upper = jnp.where(r <= c, 1.0, 0.0).astype(_bf16)
    carry = carry_ref[...]
    lane = lax.broadcasted_iota(jnp.int32, (HEAD_DIM, HEAD_DIM), 1)
    for ch in range(tm // F_CHUNK):
        x = logf[:, ch * F_CHUNK:(ch + 1) * F_CHUNK]
        hi, mid, lo = _split3(x)
        pre = _dot(hi, upper) + _dot(mid, upper) + _dot(lo, upper) + carry
        ft_ref[:, ch * F_CHUNK:(ch + 1) * F_CHUNK] = pre
        carry = pre[:, F_CHUNK - 1:F_CHUNK]
        for hh in range(N_HEADS):
            for sub in range(F_CHUNK // HEAD_DIM):
                row = pre[hh:hh + 1, sub * HEAD_DIM:(sub + 1) * HEAD_DIM]
                neg = -jnp.broadcast_to(row, (HEAD_DIM, HEAD_DIM)).T
                base = ch * F_CHUNK + sub * HEAD_DIM
                kx_ref[hh, base:base + HEAD_DIM, :] = _select_rows_or_lanes(
                    lane, _split3(neg)).astype(kx_ref.dtype)
    carry_ref[...] = carry


def _norm_and_forget(x, g, w_in_t, b_col, tm=1024):
    s, d = x.shape
    f_block = 6 * WIDTH // N_HEADS
    return pl.pallas_call(
        _forget_kernel,
        grid=(s // tm,),
        in_specs=[pl.BlockSpec((None, N_HEADS, d), lambda i: (0, f_block, 0)),
                  pl.BlockSpec((tm, d), lambda i: (i, 0)),
                  pl.BlockSpec((1, d), lambda i: (0, 0)),
                  pl.BlockSpec((F_ROWS, 1), lambda i: (0, 0))],
        out_specs=[pl.BlockSpec((tm, d), lambda i: (i, 0)),
                   pl.BlockSpec((F_ROWS, tm), lambda i: (0, i)),
                   pl.BlockSpec((N_HEADS, tm, HEAD_DIM), lambda i: (0, i, 0))],
        out_shape=[jax.ShapeDtypeStruct((s, d), _bf16),
                   jax.ShapeDtypeStruct((F_ROWS, s), _f32),
                   jax.ShapeDtypeStruct((N_HEADS, s, HEAD_DIM), _bf16)],
        scratch_shapes=[pltpu.VMEM((F_ROWS, 1), _f32)],
        compiler_params=_cparams(("arbitrary",)),
        name="norm_forget_gate",
    )(w_in_t, x, g.reshape(1, d), b_col)


def _block_partial(z, vt):
    mloc = jnp.max(z, axis=0, keepdims=True)
    o = _dot(vt, jnp.exp2(z - mloc).astype(_bf16))
    return mloc, o[HEAD_DIM:HEAD_DIM + 1], o[:HEAD_DIM]


def _merge_partials(m_ref, l_ref, acc_ref, idx, c, parts):
    cs = slice(c * TQS, (c + 1) * TQS)
    m_old = m_ref[idx, :, cs]
    m_new = m_old
    for mt, _, _ in parts:
        m_new = jnp.maximum(m_new, mt)
    a = jnp.exp2(m_old - m_new)
    l_new = a * l_ref[idx, :, cs]
    acc_new = a * acc_ref[idx, :, cs]
    for mt, l, o in parts:
        b = jnp.exp2(mt - m_new)
        l_new = l_new + b * l
        acc_new = acc_new + b * o
    m_ref[idx, :, cs] = m_new
    l_ref[idx, :, cs] = l_new
    acc_ref[idx, :, cs] = acc_new


def _init_state(m_ref, l_ref, acc_ref):
    m_ref[...] = jnp.full_like(m_ref, NEG_BIG)
    l_ref[...] = jnp.zeros_like(l_ref)
    acc_ref[...] = jnp.zeros_like(acc_ref)


def _head_sweep(scores_cols, main_chain, band_row, start_tile, finish_tile, sa_ref, sb_ref, sc_ref):
    n_maps = len(sa_ref)
    n_tiles = SEQ // TQ

    def n_keys(c, band):
        return (c + 1) * TK if band else TKM

    def fill_cols(refs, tile, block, c, band=False):
        for ref, val in zip(refs, scores_cols(tile, block, c, n_keys(c, band))):
            ref[:n_keys(c, band), c * TQS:(c + 1) * TQS] = val

    def overlapped(tile, cur_block, cur_refs, nxt_block, nxt_refs, band=False):
        for c in range(NSUB):
            for idx in range(n_maps):
                nxt_refs[idx][:n_keys(c, band), c * TQS:(c + 1) * TQS] = scores_cols(
                    tile, nxt_block, c, n_keys(c, band))[idx]
                main_chain(tile, cur_block, cur_refs, c, idx)

    for c in range(NSUB):
        fill_cols(sc_ref, 0, 0, c, band=True)
    start_tile()

    def tile_body(qi, carry):
        def pair(t, carry2):
            overlapped(qi, 2 * t, sa_ref, 2 * t + 1, sb_ref)
            overlapped(qi, 2 * t + 1, sb_ref, 2 * t + 2, sa_ref)
            return carry2

        n_pairs = lax.shift_right_logical(jnp.maximum(qi - 1, 0), 1)
        lax.fori_loop(0, n_pairs, pair, 0)
        last = 2 * n_pairs

        @pl.when(qi - last == 2)
        def _():
            overlapped(qi, last, sa_ref, last + 1, sb_ref)
            overlapped(qi, last + 1, sb_ref, qi, sc_ref, band=True)

        @pl.when(qi - last == 1)
        def _():
            overlapped(qi, last, sa_ref, qi, sc_ref, band=True)

        nxt = jnp.minimum(qi + 1, n_tiles - 1)
        for j in range(NSUB):
            band_row(qi, j, sc_ref)
            fill_cols(sa_ref, nxt, 0, j)
        finish_tile(qi)
        start_tile()
        return carry

    lax.fori_loop(0, n_tiles, tile_body, 0)


def _local_iotas():
    lane = lax.broadcasted_iota(jnp.int32, (TK, TQS), 1)
    sub = lax.broadcasted_iota(jnp.int32, (TK, TQS), 0)
    return lane, sub


def _diff_attn_kernel(slope_ref, q1_ref, q2_ref, k_ref, vt_ref, lq1_ref, lk1_ref, lq2_ref, lk2_ref,
                      gsub_ref, o_ref, m_ref, l_ref, acc_ref, *s_refs):
    h = pl.program_id(0)
    neg_slope = -slope_ref[h] * LOG2E
    q_refs = (q1_ref, q2_ref)

    slope2 = -neg_slope
    krow = lax.broadcasted_iota(jnp.int32, (TKM, HEAD_DIM), 0)
    klane = lax.broadcasted_iota(jnp.int32, (TKM, HEAD_DIM), 1)
    kx = jnp.where(klane < 3, (krow // HEAD_DIM).astype(_f32),
                   jnp.where(klane < 6, (krow % HEAD_DIM).astype(_f32), 0.0)).astype(_bf16)
    qrow = lax.broadcasted_iota(jnp.int32, (HEAD_DIM, TQS), 0)
    ones = jnp.ones((HEAD_DIM, TQS), _f32)
    qx = _select_rows_or_lanes(qrow, _split3(ones * (slope2 * HEAD_DIM)) + _split3(ones * slope2)).astype(_bf16)
    t_loc = lax.broadcasted_iota(jnp.int32, (1, TQS), 1).astype(_f32)
    lane, sub = _local_iotas()
    fix_diag = jnp.where((sub // 64) <= (lane // 64),
                         (sub.astype(_f32) + jnp.abs((lane - sub).astype(_f32))) * neg_slope, NEG_BIG)
    lam = (jnp.exp(jnp.sum(lq1_ref[...] * lk1_ref[...], axis=-1, keepdims=True))
           - jnp.exp(jnp.sum(lq2_ref[...] * lk2_ref[...], axis=-1, keepdims=True))
           + LAMBDA_INIT)

    def scores_cols(tile, block, c, nk):
        start = pl.multiple_of(block * TKM, TKM)
        k_aug = jnp.concatenate([k_ref[0, pl.ds(start, nk), :], kx[:nk]], axis=1)
        cs = slice(c * TQS, (c + 1) * TQS)
        return tuple(_dot(k_aug, jnp.concatenate([q_ref[0, tile, :, cs], qx], axis=0)) for q_ref in q_refs)

    def query_shift(tile, block, c):
        return (jnp.asarray((tile - block) * TQ + c * TQS, _f32) + t_loc) * neg_slope

    def main_chain(tile, block, s_refs, c, idx):
        vt = vt_ref[0, block]
        mloc, l, o = _block_partial(s_refs[idx][:, c * TQS:(c + 1) * TQS], vt)
        _merge_partials(m_ref, l_ref, acc_ref, idx, c, [(mloc + query_shift(tile, block, c), l, o)])

    def band_row(tile, j, s_refs):
        vt = vt_ref[0, tile, :, j * TK:(j + 1) * TK]
        for idx in range(2):
            for c in range(j, NSUB):
                z = s_refs[idx][j * TK:(j + 1) * TK, c * TQS:(c + 1) * TQS]
                if c == j:
                    mloc, l, o = _block_partial(z + fix_diag, vt)
                    part = (mloc + (j * TK) * neg_slope, l, o)
                else:
                    mloc, l, o = _block_partial(z, vt)
                    part = (mloc + query_shift(tile, tile, c), l, o)
                _merge_partials(m_ref, l_ref, acc_ref, idx, c, [part])

    def finish_tile(tile):
        o = acc_ref[0] / l_ref[0] - lam * (acc_ref[1] / l_ref[1])
        r = lax.rsqrt(jnp.mean(o * o, axis=0, keepdims=True) + EPS)
        y = (o * r) * gsub_ref[...] * (1.0 - LAMBDA_INIT)
        o_ref[pl.ds(pl.multiple_of(tile * TQ, TQ), TQ), :] = y.T.astype(o_ref.dtype)

    n = len(s_refs) // 3
    _head_sweep(scores_cols, main_chain, band_row, lambda: _init_state(m_ref, l_ref, acc_ref), finish_tile,
                s_refs[:n], s_refs[n:2 * n], s_refs[2 * n:])


def _head_specs(n_q):
    nq = SEQ // TQ
    q_spec = pl.BlockSpec((1, nq, HEAD_DIM, TQ), lambda h: (h, 0, 0, 0))
    return [q_spec] * n_q + [pl.BlockSpec((1, SEQ, HEAD_DIM), lambda h: (h, 0, 0)),
                             pl.BlockSpec((1, SEQ // TKM, V_ROWS, TKM), lambda h: (h, 0, 0, 0))]


def _head_scratch(n_maps):
    return [pltpu.VMEM((n_maps, 1, TQ), _f32),
            pltpu.VMEM((n_maps, 1, TQ), _f32),
            pltpu.VMEM((n_maps, HEAD_DIM, TQ), _f32)] + [pltpu.VMEM((TKM, TQ), _f32)] * (3 * n_maps)


def _diff_attention(slopes, q1t, q2t, k, vt, lq1, lk1, lq2, lk2, gsub_col):
    vec = lambda: pl.BlockSpec((1, QK_DIFF), lambda h: (0, 0))
    q_specs = _head_specs(2)
    return pl.pallas_call(
        _diff_attn_kernel,
        grid=(N_HEADS,),
        in_specs=[pl.BlockSpec(memory_space=pltpu.SMEM)] + q_specs + [
            vec(), vec(), vec(), vec(), pl.BlockSpec((HEAD_DIM, 1), lambda h: (0, 0))],
        out_specs=pl.BlockSpec((SEQ, HEAD_DIM), lambda h: (0, h)),
        out_shape=jax.ShapeDtypeStruct((SEQ, WIDTH), _bf16),
        scratch_shapes=_head_scratch(2),
        compiler_params=_cparams(("parallel",)),
        name="diff_attention",
    )(slopes, q1t, q2t, k, vt, lq1, lk1, lq2, lk2, gsub_col)


def _fox_attn_kernel(q_ref, k_ref, vt_ref, ft_ref, kx_ref, o_ref, m_ref, l_ref, acc_ref, *s_refs):
    qrow = lax.broadcasted_iota(jnp.int32, (HEAD_DIM, TQS), 0)
    ones_rows = jnp.where(qrow < 3, 1.0, 0.0).astype(_bf16)
    lane, sub = _local_iotas()
    causal = sub <= lane

    def scores_cols(tile, block, c, nk):
        start = pl.multiple_of(block * TKM, TKM)
        k_aug = jnp.concatenate([k_ref[0, pl.ds(start, nk), :], kx_ref[0, pl.ds(start, nk), :]], axis=1)
        q_aug = jnp.concatenate([q_ref[0, tile, :, c * TQS:(c + 1) * TQS], ones_rows], axis=0)
        return (_dot(k_aug, q_aug),)

    def main_chain(tile, block, s_ref, c, idx):
        cs = slice(c * TQS, (c + 1) * TQS)
        mloc, l, o = _block_partial(s_ref[0][:, cs], vt_ref[0, block])
        _merge_partials(m_ref, l_ref, acc_ref, 0, c, [(mloc + ft_ref[0, tile, :, cs], l, o)])

    def band_row(tile, j, s_ref):
        vt = vt_ref[0, tile, :, j * TK:(j + 1) * TK]
        for c in range(j, NSUB):
            cs = slice(c * TQS, (c + 1) * TQS)
            z = s_ref[0][j * TK:(j + 1) * TK, cs]
            if c == j:
                z = jnp.where(causal, z, NEG_BIG)
            mloc, l, o = _block_partial(z, vt)
            _merge_partials(m_ref, l_ref, acc_ref, 0, c, [(mloc + ft_ref[0, tile, :, cs], l, o)])

    def finish_tile(tile):
        o = acc_ref[0] / l_ref[0]
        o_ref[pl.ds(pl.multiple_of(tile * TQ, TQ), TQ), :] = o.T.astype(o_ref.dtype)

    _head_sweep(scores_cols, main_chain, band_row, lambda: _init_state(m_ref, l_ref, acc_ref), finish_tile,
                s_refs[:1], s_refs[1:2], s_refs[2:])


def _fox_attention(qt, k, vt, ft4, kx):
    nq = SEQ // TQ
    return pl.pallas_call(
        _fox_attn_kernel,
        grid=(N_HEADS,),
        in_specs=_head_specs(1) + [pl.BlockSpec((1, nq, 1, TQ), lambda h: (h, 0, 0, 0)),
                                   pl.BlockSpec((1, SEQ, HEAD_DIM), lambda h: (h, 0, 0))],
        out_specs=pl.BlockSpec((SEQ, HEAD_DIM), lambda h: (0, h)),
        out_shape=jax.ShapeDtypeStruct((SEQ, WIDTH), _bf16),
        scratch_shapes=_head_scratch(1),
        compiler_params=_cparams(("parallel",)),
        name="fox_attention",
    )(qt, k, vt, ft4, kx)


def _merge_kernel(oa_ref, ob_ref, g_ref, x_ref, wbd_ref, wbf_ref, wout_ref, gm_ref, x1_ref, h2_ref):
    a = _dot(oa_ref[...], wbd_ref[...])
    b = _dot(ob_ref[...], wbf_ref[...])
    g = g_ref[...].astype(_f32)
    merged = g[:, :D_MODEL] * a + g[:, D_MODEL:] * b
    x1 = x_ref[...] + _dot(merged.astype(_bf16), wout_ref[...])
    x1_ref[...] = x1
    r = lax.rsqrt(jnp.mean(x1 * x1, axis=-1, keepdims=True) + EPS)
    h2_ref[...] = ((x1 * r) * gm_ref[...]).astype(h2_ref.dtype)


def _merge(oa, ob, gates, x, wbd, wbf, wout, gm, tm=256):
    s, d = x.shape
    const = lambda shape: pl.BlockSpec(shape, lambda i: (0, 0), pipeline_mode=pl.Buffered(1))
    return pl.pallas_call(
        _merge_kernel,
        grid=(s // tm,),
        in_specs=[pl.BlockSpec((tm, WIDTH), lambda i: (i, 0)),
                  pl.BlockSpec((tm, WIDTH), lambda i: (i, 0)),
                  pl.BlockSpec((tm, 2 * d), lambda i: (i, 0)),
                  pl.BlockSpec((tm, d), lambda i: (i, 0)),
                  const((WIDTH, d)), const((WIDTH, d)), const((d, d)),
                  pl.BlockSpec((1, d), lambda i: (0, 0))],
        out_specs=[pl.BlockSpec((tm, d), lambda i: (i, 0)),
                   pl.BlockSpec((tm, d), lambda i: (i, 0))],
        out_shape=[jax.ShapeDtypeStruct((s, d), _f32),
                   jax.ShapeDtypeStruct((s, d), _bf16)],
        compiler_params=_cparams(("parallel",)),
        name="merge_out_proj",
    )(oa, ob, gates, x, wbd, wbf, wout, gm.reshape(1, d))


def _mlp_kernel(h_ref, wu_ref, wd_ref, x_ref, o_ref):
    k = pl.program_id(1)

    @pl.when(k == 0)
    def _():
        o_ref[...] = x_ref[...]

    u = jnp.maximum(_dot(h_ref[...], wu_ref[...]), 0.0)
    o_ref[...] += _dot((u * u).astype(_bf16), wd_ref[...])


def _mlp(h2, wu, wd, x1, tm=512, tf=1024):
    s, d = x1.shape
    f = wu.shape[1]
    return pl.pallas_call(
        _mlp_kernel,
        grid=(s // tm, f // tf),
        in_specs=[pl.BlockSpec((tm, d), lambda i, k: (i, 0)),
                  pl.BlockSpec((d, tf), lambda i, k: (0, k)),
                  pl.BlockSpec((tf, d), lambda i, k: (k, 0)),
                  pl.BlockSpec((tm, d), lambda i, k: (i, 0))],
        out_specs=pl.BlockSpec((tm, d), lambda i, k: (i, 0)),
        out_shape=jax.ShapeDtypeStruct((s, d), _f32),
        compiler_params=_cparams(("parallel", "arbitrary")),
        name="mlp_relu2",
    )(h2, wu, wd, x1)


def kernel(x, norm_mix, w_in, b_forget, qnorm_diff, knorm_diff, lambda_q1, lambda_k1, lambda_q2, lambda_k2,
           subln_diff, qnorm_fox, knorm_fox, w_branch_diff, w_branch_fox, w_gate, b_gate, w_out, norm_mlp,
           w_mlp_up, w_mlp_down):
    assert x.shape == (1, SEQ, D_MODEL)
    x2 = x[0]
    w_in_t = jnp.swapaxes(w_in, 1, 2)

    bf_col = jnp.zeros((F_ROWS, 1), _f32).at[:N_HEADS, 0].set(b_forget[0])
    h, ft, kx_fox = _norm_and_forget(x2, norm_mix[0], w_in_t, bf_col)
    ft4 = ft.reshape(F_ROWS, SEQ // TQ, 1, TQ)

    g_qd = jnp.tile(qnorm_diff[0], 2).reshape(1, HEAD_DIM)
    g_kd = jnp.tile(knorm_diff[0], 2).reshape(1, HEAD_DIM)
    tm = TKM
    t_shape = jax.ShapeDtypeStruct((N_HEADS, SEQ // TQ, HEAD_DIM, TQ), _bf16)
    k_shape = jax.ShapeDtypeStruct((N_HEADS, SEQ, HEAD_DIM), _bf16)
    vt_shape = jax.ShapeDtypeStruct((N_HEADS, SEQ // TKM, V_ROWS, TKM), _bf16)
    vt_spec = pl.BlockSpec((N_HEADS, 1, V_ROWS, TKM), lambda i: (0, i, 0, 0))

    q1t, q2t = _proj_call(_proj_qdiff_kernel, h, w_in_t, 0, [g_qd], [t_shape, t_shape],
                          [_t_spec(tm), _t_spec(tm)], "proj_q_diff")
    ka = _proj_call(functools.partial(_proj_k_kernel, groups=2), h, w_in_t, 1, [g_kd], k_shape,
                    _k_spec(tm), "proj_k_diff")
    vat = _proj_call(_proj_vt_kernel, h, w_in_t, 2, [], vt_shape, vt_spec, "proj_v_diff")
    qbt = _proj_call(_proj_qfox_kernel, h, w_in_t, 3, [qnorm_fox[0].reshape(1, HEAD_DIM)], t_shape,
                     _t_spec(tm), "proj_q_fox")
    kb = _proj_call(functools.partial(_proj_k_kernel, groups=1), h, w_in_t, 4,
                    [knorm_fox[0].reshape(1, HEAD_DIM)], k_shape, _k_spec(tm), "proj_k_fox")
    vbt = _proj_call(_proj_vt_kernel, h, w_in_t, 5, [], vt_shape, vt_spec, "proj_v_fox")

    gates, (wbd, wbf, wout, wup, wdown) = _gates(
        h, w_gate, b_gate[0], [w_branch_diff, w_branch_fox, w_out, w_mlp_up, w_mlp_down])

    slopes = 2.0 ** (-8.0 * jnp.arange(1, N_HEADS + 1, dtype=_f32) / N_HEADS)
    row = lambda v: v[0].reshape(1, QK_DIFF)
    oa = _diff_attention(slopes, q1t, q2t, ka, vat, row(lambda_q1), row(lambda_k1), row(lambda_q2),
                         row(lambda_k2), subln_diff[0].reshape(HEAD_DIM, 1))
    ob = _fox_attention(qbt, kb, vbt, ft4, kx_fox)

    x1, h2 = _merge(oa, ob, gates, x2, wbd, wbf, wout, norm_mlp[0])
    out = _mlp(h2, wup, wdown, x1)
    return out[None]
```

```python
import functools

import jax
import jax.numpy as jnp
from jax import lax
from jax.experimental import pallas as pl
from jax.experimental.pallas import tpu as pltpu

D_MODEL = 2048
SEQ = 8192
HEAD_DIM = 128
N_HEADS = 8
QK_DIFF = 64
WIDTH = N_HEADS * HEAD_DIM
D_FF = 4 * D_MODEL
EPS = 1e-6
LAMBDA_INIT = 0.8 - 0.6 * 1.0
NEG_BIG = -1e30

TQ = 1024
TQS = 256
TK = 256
TKM = TQ
NSUB = TQ // TQS
LOG2E = 1.4426950408889634
V_ROWS = HEAD_DIM + 16
VMEM_LIMIT = 56 * 1024 * 1024

_f32 = jnp.float32
_bf16 = jnp.bfloat16


def _cparams(sem):
    return pltpu.CompilerParams(dimension_semantics=sem, vmem_limit_bytes=VMEM_LIMIT)


def _dot(a, b):
    return jnp.dot(a, b, preferred_element_type=_f32)


def _cast_slab_specs(weights, nsteps, step_of):
    in_specs, out_specs, out_shapes = [], [], []
    for w in weights:
        _, rows, cols = w.shape
        slab = rows // nsteps
        assert slab * nsteps == rows and slab % 16 == 0
        in_specs.append(pl.BlockSpec((None, slab, cols), lambda *ids: (0, step_of(*ids), 0)))
        out_specs.append(pl.BlockSpec((slab, cols), lambda *ids: (step_of(*ids), 0)))
        out_shapes.append(jax.ShapeDtypeStruct((rows, cols), _bf16))
    return in_specs, out_specs, out_shapes


def _cast_slabs(src_refs, dst_refs):
    for src, dst in zip(src_refs, dst_refs):
        dst[...] = src[...].astype(dst.dtype)


def _head_rmsnorm(y, gain_row, groups):
    y2 = y * y
    if groups == 1:
        r = lax.rsqrt(jnp.mean(y2, axis=-1, keepdims=True) + EPS)
    else:
        lane = lax.broadcasted_iota(jnp.int32, y.shape, 1)
        lo = lane < QK_DIFF
        s_lo = jnp.sum(jnp.where(lo, y2, 0.0), axis=-1, keepdims=True)
        s_hi = jnp.sum(jnp.where(lo, 0.0, y2), axis=-1, keepdims=True)
        r = jnp.where(lo, lax.rsqrt(s_lo * (1.0 / QK_DIFF) + EPS),
                      lax.rsqrt(s_hi * (1.0 / QK_DIFF) + EPS))
    return (y * r) * gain_row


PROJ_SUB = 256


def _proj_subtiles(a_ref, w_ref, wb_ref, first_step, epilogue, w_transposed=False, side_work=None):
    @pl.when(first_step)
    def _():
        w = w_ref[...].T if w_transposed else w_ref[...]
        wb_ref[...] = w.astype(wb_ref.dtype)

    if side_work is not None:
        side_work()
    for sub in range(a_ref.shape[0] // PROJ_SUB):
        rows = slice(sub * PROJ_SUB, (sub + 1) * PROJ_SUB)
        epilogue(rows, _dot(a_ref[rows, :], wb_ref[...]))


def _heads(acc):
    return [acc[:, hh * HEAD_DIM:(hh + 1) * HEAD_DIM] for hh in range(N_HEADS)]


def _proj_qdiff_kernel(a_ref, w_ref, g_ref, q1_ref, q2_ref, wb_ref):
    def epilogue(rows, acc):
        lo = lax.broadcasted_iota(jnp.int32, (PROJ_SUB, HEAD_DIM), 1) < QK_DIFF
        for hh, y in enumerate(_heads(acc)):
            yn = _head_rmsnorm(y, g_ref[...], 2) * (QK_DIFF ** -0.5 * LOG2E)
            q1_ref[hh, 0, :, rows] = jnp.where(lo, yn, 0.0).T.astype(q1_ref.dtype)
            q2_ref[hh, 0, :, rows] = jnp.where(lo, 0.0, yn).T.astype(q2_ref.dtype)

    _proj_subtiles(a_ref, w_ref, wb_ref, pl.program_id(0) == 0, epilogue, w_transposed=True)


def _proj_qfox_kernel(a_ref, w_ref, g_ref, q_ref, wb_ref):
    def epilogue(rows, acc):
        for hh, y in enumerate(_heads(acc)):
            yn = _head_rmsnorm(y, g_ref[...], 1) * (HEAD_DIM ** -0.5 * LOG2E)
            q_ref[hh, 0, :, rows] = yn.T.astype(q_ref.dtype)

    _proj_subtiles(a_ref, w_ref, wb_ref, pl.program_id(0) == 0, epilogue, w_transposed=True)


def _proj_k_kernel(a_ref, w_ref, g_ref, k_ref, wb_ref, *, groups):
    def epilogue(rows, acc):
        for hh, y in enumerate(_heads(acc)):
            k_ref[hh, rows, :] = _head_rmsnorm(y, g_ref[...], groups).astype(k_ref.dtype)

    _proj_subtiles(a_ref, w_ref, wb_ref, pl.program_id(0) == 0, epilogue, w_transposed=True)


def _proj_vt_kernel(a_ref, w_ref, vt_ref, wb_ref):
    row = lax.broadcasted_iota(jnp.int32, (V_ROWS - HEAD_DIM, TKM), 0)
    ones_rows = jnp.where(row == 0, 1.0, 0.0).astype(vt_ref.dtype)
    for hh in range(N_HEADS):
        vt_ref[hh, 0, HEAD_DIM:, :] = ones_rows

    def epilogue(rows, acc):
        for hh, y in enumerate(_heads(acc)):
            vt_ref[hh, 0, :HEAD_DIM, rows] = y.T.astype(vt_ref.dtype)

    _proj_subtiles(a_ref, w_ref, wb_ref, pl.program_id(0) == 0, epilogue, w_transposed=True)


def _proj_gates_kernel(a_ref, w_ref, b_ref, *refs, n_cast):
    cast_in, o_ref, cast_out, wb_ref = refs[:n_cast], refs[n_cast], refs[n_cast + 1:-1], refs[-1]

    def epilogue(rows, acc):
        o_ref[rows, :] = (1.0 / (1.0 + jnp.exp(-(acc + b_ref[...])))).astype(o_ref.dtype)

    _proj_subtiles(a_ref, w_ref, wb_ref, pl.program_id(1) == 0, epilogue,
                   side_work=lambda: _cast_slabs(cast_in, cast_out))


def _proj_call(kernel, h, w, col_block, extras, out_shapes, out_specs, name, tm=TKM):
    s, d = h.shape
    in_specs = [pl.BlockSpec((tm, d), lambda i: (i, 0)),
                pl.BlockSpec((None, WIDTH, d), lambda i: (0, col_block, 0), pipeline_mode=pl.Buffered(1))]
    in_specs += [pl.BlockSpec(e.shape, lambda i: (0, 0)) for e in extras]
    return pl.pallas_call(
        kernel,
        grid=(s // tm,),
        in_specs=in_specs,
        out_specs=out_specs,
        out_shape=out_shapes,
        scratch_shapes=[pltpu.VMEM((d, WIDTH), _bf16)],
        compiler_params=_cparams(("arbitrary",)),
        name=name,
    )(h, w, *extras)


def _t_spec(tm):
    assert tm == TQ
    return pl.BlockSpec((N_HEADS, 1, HEAD_DIM, tm), lambda i: (0, i, 0, 0))


def _k_spec(tm):
    return pl.BlockSpec((N_HEADS, tm, HEAD_DIM), lambda i: (0, i, 0))


def _gates(h, w, b, cast_weights, tm=1024, tn=1024):
    s, d = h.shape
    n = w.shape[2]
    ni = s // tm
    cast_in, cast_out, cast_shapes = _cast_slab_specs(cast_weights, (n // tn) * ni, lambda j, i: j * ni + i)
    outs = pl.pallas_call(
        functools.partial(_proj_gates_kernel, n_cast=len(cast_weights)),
        grid=(n // tn, ni),
        in_specs=[pl.BlockSpec((tm, d), lambda j, i: (i, 0)),
                  pl.BlockSpec((None, d, tn), lambda j, i: (0, 0, j)),
                  pl.BlockSpec((1, tn), lambda j, i: (0, j))] + cast_in,
        out_specs=[pl.BlockSpec((tm, tn), lambda j, i: (i, j))] + cast_out,
        out_shape=[jax.ShapeDtypeStruct((s, n), _bf16)] + cast_shapes,
        scratch_shapes=[pltpu.VMEM((d, tn), _bf16)],
        compiler_params=_cparams(("arbitrary", "arbitrary")),
        name="proj_gates",
    )(h, w, b.reshape(1, n), *cast_weights)
    return outs[0], outs[1:]


F_ROWS = 16
F_CHUNK = 256


def _split3(x):
    hi = x.astype(_bf16)
    r1 = x - hi.astype(_f32)
    mid = r1.astype(_bf16)
    lo = (r1 - mid.astype(_f32)).astype(_bf16)
    return hi, mid, lo


def _select_rows_or_lanes(index, values):
    out = jnp.zeros(index.shape, _f32)
    for i in reversed(range(len(values))):
        out = jnp.where(index == i, values[i].astype(_f32), out)
    return out


def _forget_kernel(wt_ref, x_ref, g_ref, b_ref, h_ref, ft_ref, kx_ref, carry_ref):
    i = pl.program_id(0)

    @pl.when(i == 0)
    def _():
        carry_ref[...] = jnp.zeros_like(carry_ref)

    x = x_ref[...]
    r = lax.rsqrt(jnp.mean(x * x, axis=-1, keepdims=True) + EPS)
    h = ((x * r) * g_ref[...]).astype(h_ref.dtype)
    h_ref[...] = h
    wt8 = wt_ref[...]
    wt = jnp.concatenate([wt8, jnp.zeros_like(wt8)], axis=0).astype(_bf16)
    z = lax.dot_general(wt, h, (((1,), (1,)), ((), ())),
                        preferred_element_type=_f32) + b_ref[...]
    logf = (jnp.minimum(z, 0.0) - jnp.log(1.0 + jnp.exp(-jnp.abs(z)))) * LOG2E
    tm = logf.shape[1]
    r = lax.broadcasted_iota(jnp.int32, (F_CHUNK, F_CHUNK), 0)
    c = lax.broadcasted_iota(jnp.int32, (F_CHUNK, F_CHUNK), 1)
    upper = jnp.where(r <= c, 1.0, 0.0).astype(_bf16)
    carry = carry_ref[...]
    lane = lax.broadcasted_iota(jnp.int32, (HEAD_DIM, HEAD_DIM), 1)
    for ch in range(tm // F_CHUNK):
        x = logf[:, ch * F_CHUNK:(ch + 1) * F_CHUNK]
        hi, mid, lo = _split3(x)
        pre = _dot(hi, upper) + _dot(mid, upper) + _dot(lo, upper) + carry
        ft_ref[:, ch * F_CHUNK:(ch + 1) * F_CHUNK] = pre
        carry = pre[:, F_CHUNK - 1:F_CHUNK]
        for hh in range(N_HEADS):
            for sub in range(F_CHUNK // HEAD_DIM):
                row = pre[hh:hh + 1, sub * HEAD_DIM:(sub + 1) * HEAD_DIM]
                neg = -jnp.broadcast_to(row, (HEAD_DIM, HEAD_DIM)).T
                base = ch * F_CHUNK + sub * HEAD_DIM
                kx_ref[hh, base:base + HEAD_DIM, :] = _select_rows_or_lanes(
                    lane, _split3(neg)).astype(kx_ref.dtype)
    carry_ref[...] = carry


def _norm_and_forget(x, g, w_in_t, b_col, tm=1024):
    s, d = x.shape
    f_block = 6 * WIDTH // N_HEADS
    return pl.pallas_call(
        _forget_kernel,
        grid=(s // tm,),
        in_specs=[pl.BlockSpec((None, N_HEADS, d), lambda i: (0, f_block, 0)),
                  pl.BlockSpec((tm, d), lambda i: (i, 0)),
                  pl.BlockSpec((1, d), lambda i: (0, 0)),
                  pl.BlockSpec((F_ROWS, 1), lambda i: (0, 0))],
        out_specs=[pl.BlockSpec((tm, d), lambda i: (i, 0)),
                   pl.BlockSpec((F_ROWS, tm), lambda i: (0, i)),
                   pl.BlockSpec((N_HEADS, tm, HEAD_DIM), lambda i: (0, i, 0))],
        out_shape=[jax.ShapeDtypeStruct((s, d), _bf16),
                   jax.ShapeDtypeStruct((F_ROWS, s), _f32),
                   jax.ShapeDtypeStruct((N_HEADS, s, HEAD_DIM), _bf16)],
        scratch_shapes=[pltpu.VMEM((F_ROWS, 1), _f32)],
        compiler_params=_cparams(("arbitrary",)),
        name="norm_forget_gate",
    )(w_in_t, x, g.reshape(1, d), b_col)


def _block_partial(z, vt):
    mloc = jnp.max(z, axis=0, keepdims=True)
    o = _dot(vt, jnp.exp2(z - mloc).astype(_bf16))
    return mloc, o[HEAD_DIM:HEAD_DIM + 1], o[:HEAD_DIM]


def _merge_partials(m_ref, l_ref, acc_ref, idx, c, parts):
    cs = slice(c * TQS, (c + 1) * TQS)
    m_old = m_ref[idx, :, cs]
    m_new = m_old
    for mt, _, _ in parts:
        m_new = jnp.maximum(m_new, mt)
    a = jnp.exp2(m_old - m_new)
    l_new = a * l_ref[idx, :, cs]
    acc_new = a * acc_ref[idx, :, cs]
    for mt, l, o in parts:
        b = jnp.exp2(mt - m_new)
        l_new = l_new + b * l
        acc_new = acc_new + b * o
    m_ref[idx, :, cs] = m_new
    l_ref[idx, :, cs] = l_new
    acc_ref[idx, :, cs] = acc_new


def _init_state(m_ref, l_ref, acc_ref):
    m_ref[...] = jnp.full_like(m_ref, NEG_BIG)
    l_ref[...] = jnp.zeros_like(l_ref)
    acc_ref[...] = jnp.zeros_like(acc_ref)


def _head_sweep(scores_cols, main_chain, band_row, start_tile, finish_tile, sa_ref, sb_ref, sc_ref):
    n_maps = len(sa_ref)
    n_tiles = SEQ // TQ

    def n_keys(c, band):
        return (c + 1) * TK if band else TKM

    def fill_cols(refs, tile, block, c, band=False):
        for ref, val in zip(refs, scores_cols(tile, block, c, n_keys(c, band))):
            ref[:n_keys(c, band), c * TQS:(c + 1) * TQS] = val

    def overlapped(tile, cur_block, cur_refs, nxt_block, nxt_refs, band=False):
        for c in range(NSUB):
            for idx in range(n_maps):
                nxt_refs[idx][:n_keys(c, band), c * TQS:(c + 1) * TQS] = scores_cols(
                    tile, nxt_block, c, n_keys(c, band))[idx]
                main_chain(tile, cur_block, cur_refs, c, idx)

    for c in range(NSUB):
        fill_cols(sc_ref, 0, 0, c, band=True)
    start_tile()

    def tile_body(qi, carry):
        def pair(t, carry2):
            overlapped(qi, 2 * t, sa_ref, 2 * t + 1, sb_ref)
            overlapped(qi, 2 * t + 1, sb_ref, 2 * t + 2, sa_ref)
            return carry2

        n_pairs = lax.shift_right_logical(jnp.maximum(qi - 1, 0), 1)
        lax.fori_loop(0, n_pairs, pair, 0)
        last = 2 * n_pairs

        @pl.when(qi - last == 2)
        def _():
            overlapped(qi, last, sa_ref, last + 1, sb_ref)
            overlapped(qi, last + 1, sb_ref, qi, sc_ref, band=True)

        @pl.when(qi - last == 1)
        def _():
            overlapped(qi, last, sa_ref, qi, sc_ref, band=True)

        nxt = jnp.minimum(qi + 1, n_tiles - 1)
        for j in range(NSUB):
            band_row(qi, j, sc_ref)
            fill_cols(sa_ref, nxt, 0, j)
        finish_tile(qi)
        start_tile()
        return carry

    lax.fori_loop(0, n_tiles, tile_body, 0)


def _local_iotas():
    lane = lax.broadcasted_iota(jnp.int32, (TK, TQS), 1)
    sub = lax.broadcasted_iota(jnp.int32, (TK, TQS), 0)
    return lane, sub


def _diff_attn_kernel(slope_ref, q1_ref, q2_ref, k_ref, vt_ref, lq1_ref, lk1_ref, lq2_ref, lk2_ref,
                      gsub_ref, o_ref, m_ref, l_ref, acc_ref, *s_refs):
    h = pl.program_id(0)
    neg_slope = -slope_ref[h] * LOG2E
    q_refs = (q1_ref, q2_ref)

    slope2 = -neg_slope
    krow = lax.broadcasted_iota(jnp.int32, (TKM, HEAD_DIM), 0)
    klane = lax.broadcasted_iota(jnp.int32, (TKM, HEAD_DIM), 1)
    kx = jnp.where(klane < 3, (krow // HEAD_DIM).astype(_f32),
                   jnp.where(klane < 6, (krow % HEAD_DIM).astype(_f32), 0.0)).astype(_bf16)
    qrow = lax.broadcasted_iota(jnp.int32, (HEAD_DIM, TQS), 0)
    ones = jnp.ones((HEAD_DIM, TQS), _f32)
    qx = _select_rows_or_lanes(qrow, _split3(ones * (slope2 * HEAD_DIM)) + _split3(ones * slope2)).astype(_bf16)
    t_loc = lax.broadcasted_iota(jnp.int32, (1, TQS), 1).astype(_f32)
    lane, sub = _local_iotas()
    fix_diag = jnp.where((sub // 64) <= (lane // 64),
                         (sub.astype(_f32) + jnp.abs((lane - sub).astype(_f32))) * neg_slope, NEG_BIG)
    lam = (jnp.exp(jnp.sum(lq1_ref[...] * lk1_ref[...], axis=-1, keepdims=True))
           - jnp.exp(jnp.sum(lq2_ref[...] * lk2_ref[...], axis=-1, keepdims=True))
           + LAMBDA_INIT)

    def scores_cols(tile, block, c, nk):
        start = pl.multiple_of(block * TKM, TKM)
        k_aug = jnp.concatenate([k_ref[0, pl.ds(start, nk), :], kx[:nk]], axis=1)
        cs = slice(c * TQS, (c + 1) * TQS)
        return tuple(_dot(k_aug, jnp.concatenate([q_ref[0, tile, :, cs], qx], axis=0)) for q_ref in q_refs)

    def query_shift(tile, block, c):
        return (jnp.asarray((tile - block) * TQ + c * TQS, _f32) + t_loc) * neg_slope

    def main_chain(tile, block, s_refs, c, idx):
        vt = vt_ref[0, block]
        mloc, l, o = _block_partial(s_refs[idx][:, c * TQS:(c + 1) * TQS], vt)
        _merge_partials(m_ref, l_ref, acc_ref, idx, c, [(mloc + query_shift(tile, block, c), l, o)])

    def band_row(tile, j, s_refs):
        vt = vt_ref[0, tile, :, j * TK:(j + 1) * TK]
        for idx in range(2):
            for c in range(j, NSUB):
                z = s_refs[idx][j * TK:(j + 1) * TK, c * TQS:(c + 1) * TQS]
                if c == j:
                    mloc, l, o = _block_partial(z + fix_diag, vt)
                    part = (mloc + (j * TK) * neg_slope, l, o)
                else:
                    mloc, l, o = _block_partial(z, vt)
                    part = (mloc + query_shift(tile, tile, c), l, o)
                _merge_partials(m_ref, l_ref, acc_ref, idx, c, [part])

    def finish_tile(tile):
        o = acc_ref[0] / l_ref[0] - lam * (acc_ref[1] / l_ref[1])
        r = lax.rsqrt(jnp.mean(o * o, axis=0, keepdims=True) + EPS)
        y = (o * r) * gsub_ref[...] * (1.0 - LAMBDA_INIT)
        o_ref[pl.ds(pl.multiple_of(tile * TQ, TQ), TQ), :] = y.T.astype(o_ref.dtype)

    n = len(s_refs) // 3
    _head_sweep(scores_cols, main_chain, band_row, lambda: _init_state(m_ref, l_ref, acc_ref), finish_tile,
                s_refs[:n], s_refs[n:2 * n], s_refs[2 * n:])


def _head_specs(n_q):
    nq = SEQ // TQ
    q_spec = pl.BlockSpec((1, nq, HEAD_DIM, TQ), lambda h: (h, 0, 0, 0))
    return [q_spec] * n_q + [pl.BlockSpec((1, SEQ, HEAD_DIM), lambda h: (h, 0, 0)),
                             pl.BlockSpec((1, SEQ // TKM, V_ROWS, TKM), lambda h: (h, 0, 0, 0))]


def _head_scratch(n_maps):
    return [pltpu.VMEM((n_maps, 1, TQ), _f32),
            pltpu.VMEM((n_maps, 1, TQ), _f32),
            pltpu.VMEM((n_maps, HEAD_DIM, TQ), _f32)] + [pltpu.VMEM((TKM, TQ), _f32)] * (3 * n_maps)


def _diff_attention(slopes, q1t, q2t, k, vt, lq1, lk1, lq2, lk2, gsub_col):
    vec = lambda: pl.BlockSpec((1, QK_DIFF), lambda h: (0, 0))
    q_specs = _head_specs(2)
    return pl.pallas_call(
        _diff_attn_kernel,
        grid=(N_HEADS,),
        in_specs=[pl.BlockSpec(memory_space=pltpu.SMEM)] + q_specs + [
            vec(), vec(), vec(), vec(), pl.BlockSpec((HEAD_DIM, 1), lambda h: (0, 0))],
        out_specs=pl.BlockSpec((SEQ, HEAD_DIM), lambda h: (0, h)),
        out_shape=jax.ShapeDtypeStruct((SEQ, WIDTH), _bf16),
        scratch_shapes=_head_scratch(2),
        compiler_params=_cparams(("parallel",)),
        name="diff_attention",
    )(slopes, q1t, q2t, k, vt, lq1, lk1, lq2, lk2, gsub_col)


def _fox_attn_kernel(q_ref, k_ref, vt_ref, ft_ref, kx_ref, o_ref, m_ref, l_ref, acc_ref, *s_refs):
    qrow = lax.broadcasted_iota(jnp.int32, (HEAD_DIM, TQS), 0)
    ones_rows = jnp.where(qrow < 3, 1.0, 0.0).astype(_bf16)
    lane, sub = _local_iotas()
    causal = sub <= lane

    def scores_cols(tile, block, c, nk):
        start = pl.multiple_of(block * TKM, TKM)
        k_aug = jnp.concatenate([k_ref[0, pl.ds(start, nk), :], kx_ref[0, pl.ds(start, nk), :]], axis=1)
        q_aug = jnp.concatenate([q_ref[0, tile, :, c * TQS:(c + 1) * TQS], ones_rows], axis=0)
        return (_dot(k_aug, q_aug),)

    def main_chain(tile, block, s_ref, c, idx):
        cs = slice(c * TQS, (c + 1) * TQS)
        mloc, l, o = _block_partial(s_ref[0][:, cs], vt_ref[0, block])
        _merge_partials(m_ref, l_ref, acc_ref, 0, c, [(mloc + ft_ref[0, tile, :, cs], l, o)])

    def band_row(tile, j, s_ref):
        vt = vt_ref[0, tile, :, j * TK:(j + 1) * TK]
        for c in range(j, NSUB):
            cs = slice(c * TQS, (c + 1) * TQS)
            z = s_ref[0][j * TK:(j + 1) * TK, cs]
            if c == j:
                z = jnp.where(causal, z, NEG_BIG)
            mloc, l, o = _block_partial(z, vt)
            _merge_partials(m_ref, l_ref, acc_ref, 0, c, [(mloc + ft_ref[0, tile, :, cs], l, o)])

    def finish_tile(tile):
        o = acc_ref[0] / l_ref[0]
        o_ref[pl.ds(pl.multiple_of(tile * TQ, TQ), TQ), :] = o.T.astype(o_ref.dtype)

    _head_sweep(scores_cols, main_chain, band_row, lambda: _init_state(m_ref, l_ref, acc_ref), finish_tile,
                s_refs[:1], s_refs[1:2], s_refs[2:])


def _fox_attention(qt, k, vt, ft4, kx):
    nq = SEQ // TQ
    return pl.pallas_call(
        _fox_attn_kernel,
        grid=(N_HEADS,),
        in_specs=_head_specs(1) + [pl.BlockSpec((1, nq, 1, TQ), lambda h: (h, 0, 0, 0)),
                                   pl.BlockSpec((1, SEQ, HEAD_DIM), lambda h: (h, 0, 0))],
        out_specs=pl.BlockSpec((SEQ, HEAD_DIM), lambda h: (0, h)),
        out_shape=jax.ShapeDtypeStruct((SEQ, WIDTH), _bf16),
        scratch_shapes=_head_scratch(1),
        compiler_params=_cparams(("parallel",)),
        name="fox_attention",
    )(qt, k, vt, ft4, kx)


def _merge_kernel(oa_ref, ob_ref, g_ref, x_ref, wbd_ref, wbf_ref, wout_ref, gm_ref, x1_ref, h2_ref):
    for sub in range(oa_ref.shape[0] // PROJ_SUB):
        rows = slice(sub * PROJ_SUB, (sub + 1) * PROJ_SUB)
        a = _dot(oa_ref[rows, :], wbd_ref[...])
        b = _dot(ob_ref[rows, :], wbf_ref[...])
        g = g_ref[rows, :].astype(_f32)
        merged = g[:, :D_MODEL] * a + g[:, D_MODEL:] * b
        x1 = x_ref[rows, :] + _dot(merged.astype(_bf16), wout_ref[...])
        x1_ref[rows, :] = x1
        r = lax.rsqrt(jnp.mean(x1 * x1, axis=-1, keepdims=True) + EPS)
        h2_ref[rows, :] = ((x1 * r) * gm_ref[...]).astype(h2_ref.dtype)


def _merge(oa, ob, gates, x, wbd, wbf, wout, gm, tm=512):
    s, d = x.shape
    const = lambda shape: pl.BlockSpec(shape, lambda i: (0, 0), pipeline_mode=pl.Buffered(1))
    return pl.pallas_call(
        _merge_kernel,
        grid=(s // tm,),
        in_specs=[pl.BlockSpec((tm, WIDTH), lambda i: (i, 0)),
                  pl.BlockSpec((tm, WIDTH), lambda i: (i, 0)),
                  pl.BlockSpec((tm, 2 * d), lambda i: (i, 0)),
                  pl.BlockSpec((tm, d), lambda i: (i, 0)),
                  const((WIDTH, d)), const((WIDTH, d)), const((d, d)),
                  pl.BlockSpec((1, d), lambda i: (0, 0))],
        out_specs=[pl.BlockSpec((tm, d), lambda i: (i, 0)),
                   pl.BlockSpec((tm, d), lambda i: (i, 0))],
        out_shape=[jax.ShapeDtypeStruct((s, d), _f32),
                   jax.ShapeDtypeStruct((s, d), _bf16)],
        compiler_params=_cparams(("parallel",)),
        name="merge_out_proj",
    )(oa, ob, gates, x, wbd, wbf, wout, gm.reshape(1, d))


def _mlp_kernel(h_ref, wu_ref, wd_ref, x_ref, o_ref):
    k = pl.program_id(1)

    @pl.when(k == 0)
    def _():
        o_ref[...] = x_ref[...]

    for sub in range(h_ref.shape[0] // PROJ_SUB):
        rows = slice(sub * PROJ_SUB, (sub + 1) * PROJ_SUB)
        u = jnp.maximum(_dot(h_ref[rows, :], wu_ref[...]), 0.0)
        o_ref[rows, :] += _dot((u * u).astype(_bf16), wd_ref[...])


def _mlp(h2, wu, wd, x1, tm=512, tf=2048):
    s, d = x1.shape
    f = wu.shape[1]
    return pl.pallas_call(
        _mlp_kernel,
        grid=(s // tm, f // tf),
        in_specs=[pl.BlockSpec((tm, d), lambda i, k: (i, 0)),
                  pl.BlockSpec((d, tf), lambda i, k: (0, k)),
                  pl.BlockSpec((tf, d), lambda i, k: (k, 0)),
                  pl.BlockSpec((tm, d), lambda i, k: (i, 0))],
        out_specs=pl.BlockSpec((tm, d), lambda i, k: (i, 0)),
        out_shape=jax.ShapeDtypeStruct((s, d), _f32),
        compiler_params=_cparams(("parallel", "arbitrary")),
        name="mlp_relu2",
    )(h2, wu, wd, x1)


def kernel(x, norm_mix, w_in, b_forget, qnorm_diff, knorm_diff, lambda_q1, lambda_k1, lambda_q2, lambda_k2,
           subln_diff, qnorm_fox, knorm_fox, w_branch_diff, w_branch_fox, w_gate, b_gate, w_out, norm_mlp,
           w_mlp_up, w_mlp_down):
    assert x.shape == (1, SEQ, D_MODEL)
    x2 = x[0]
    w_in_t = jnp.swapaxes(w_in, 1, 2)

    bf_col = jnp.zeros((F_ROWS, 1), _f32).at[:N_HEADS, 0].set(b_forget[0])
    h, ft, kx_fox = _norm_and_forget(x2, norm_mix[0], w_in_t, bf_col)
    ft4 = ft.reshape(F_ROWS, SEQ // TQ, 1, TQ)

    g_qd = jnp.tile(qnorm_diff[0], 2).reshape(1, HEAD_DIM)
    g_kd = jnp.tile(knorm_diff[0], 2).reshape(1, HEAD_DIM)
    tm = TKM
    t_shape = jax.ShapeDtypeStruct((N_HEADS, SEQ // TQ, HEAD_DIM, TQ), _bf16)
    k_shape = jax.ShapeDtypeStruct((N_HEADS, SEQ, HEAD_DIM), _bf16)
    vt_shape = jax.ShapeDtypeStruct((N_HEADS, SEQ // TKM, V_ROWS, TKM), _bf16)
    vt_spec = pl.BlockSpec((N_HEADS, 1, V_ROWS, TKM), lambda i: (0, i, 0, 0))

    q1t, q2t = _proj_call(_proj_qdiff_kernel, h, w_in_t, 0, [g_qd], [t_shape, t_shape],
                          [_t_spec(tm), _t_spec(tm)], "proj_q_diff")
    ka = _proj_call(functools.partial(_proj_k_kernel, groups=2), h, w_in_t, 1, [g_kd], k_shape,
                    _k_spec(tm), "proj_k_diff")
    vat = _proj_call(_proj_vt_kernel, h, w_in_t, 2, [], vt_shape, vt_spec, "proj_v_diff")
    qbt = _proj_call(_proj_qfox_kernel, h, w_in_t, 3, [qnorm_fox[0].reshape(1, HEAD_DIM)], t_shape,
                     _t_spec(tm), "proj_q_fox")
    kb = _proj_call(functools.partial(_proj_k_kernel, groups=1), h, w_in_t, 4,
                    [knorm_fox[0].reshape(1, HEAD_DIM)], k_shape, _k_spec(tm), "proj_k_fox")
    vbt = _proj_call(_proj_vt_kernel, h, w_in_t, 5, [], vt_shape, vt_spec, "proj_v_fox")

    gates, (wbd, wbf, wout, wup, wdown) = _gates(
        h, w_gate, b_gate[0], [w_branch_diff, w_branch_fox, w_out, w_mlp_up, w_mlp_down])

    slopes = 2.0 ** (-8.0 * jnp.arange(1, N_HEADS + 1, dtype=_f32) / N_HEADS)
    row = lambda v: v[0].reshape(1, QK_DIFF)
    oa = _diff_attention(slopes, q1t, q2t, ka, vat, row(lambda_q1), row(lambda_k1), row(lambda_q2),
                         row(lambda_k2), subln_diff[0].reshape(HEAD_DIM, 1))
    ob = _fox_attention(qbt, kb, vbt, ft4, kx_fox)

    x1, h2 = _merge(oa, ob, gates, x2, wbd, wbf, wout, norm_mlp[0])
    out = _mlp(h2, wup, wdown, x1)
    return out[None]
```

```python
import functools

import jax
import jax.numpy as jnp
from jax import lax
from jax.experimental import pallas as pl
from jax.experimental.pallas import tpu as pltpu

D_MODEL = 2048
SEQ = 8192
HEAD_DIM = 128
N_HEADS = 8
QK_DIFF = 64
WIDTH = N_HEADS * HEAD_DIM
D_FF = 4 * D_MODEL
EPS = 1e-6
LAMBDA_INIT = 0.8 - 0.6 * 1.0
NEG_BIG = -1e30

TQ = 1024
TQS = 256
TK = 256
TKM = TQ
NSUB = TQ // TQS
LOG2E = 1.4426950408889634
V_ROWS = HEAD_DIM + 16
VMEM_LIMIT = 56 * 1024 * 1024

_f32 = jnp.float32
_bf16 = jnp.bfloat16


def _cparams(sem):
    return pltpu.CompilerParams(dimension_semantics=sem, vmem_limit_bytes=VMEM_LIMIT)


def _dot(a, b):
    return jnp.dot(a, b, preferred_element_type=_f32)


def _cast_slab_specs(weights, nsteps, step_of):
    in_specs, out_specs, out_shapes = [], [], []
    for w in weights:
        _, rows, cols = w.shape
        slab = rows // nsteps
        assert slab * nsteps == rows and slab % 16 == 0
        in_specs.append(pl.BlockSpec((None, slab, cols), lambda *ids: (0, step_of(*ids), 0)))
        out_specs.append(pl.BlockSpec((slab, cols), lambda *ids: (step_of(*ids), 0)))
        out_shapes.append(jax.ShapeDtypeStruct((rows, cols), _bf16))
    return in_specs, out_specs, out_shapes


def _cast_slabs(src_refs, dst_refs):
    for src, dst in zip(src_refs, dst_refs):
        dst[...] = src[...].astype(dst.dtype)


def _head_rmsnorm(y, gain_row, groups):
    y2 = y * y
    if groups == 1:
        r = lax.rsqrt(jnp.mean(y2, axis=-1, keepdims=True) + EPS)
    else:
        lane = lax.broadcasted_iota(jnp.int32, y.shape, 1)
        lo = lane < QK_DIFF
        s_lo = jnp.sum(jnp.where(lo, y2, 0.0), axis=-1, keepdims=True)
        s_hi = jnp.sum(jnp.where(lo, 0.0, y2), axis=-1, keepdims=True)
        r = jnp.where(lo, lax.rsqrt(s_lo * (1.0 / QK_DIFF) + EPS),
                      lax.rsqrt(s_hi * (1.0 / QK_DIFF) + EPS))
    return (y * r) * gain_row


PROJ_SUB = 256


def _proj_subtiles(a_ref, w_ref, wb_ref, first_step, epilogue, w_transposed=False, side_work=None):
    @pl.when(first_step)
    def _():
        w = w_ref[...].T if w_transposed else w_ref[...]
        wb_ref[...] = w.astype(wb_ref.dtype)

    if side_work is not None:
        side_work()
    for sub in range(a_ref.shape[0] // PROJ_SUB):
        rows = slice(sub * PROJ_SUB, (sub + 1) * PROJ_SUB)
        epilogue(rows, _dot(a_ref[rows, :], wb_ref[...]))


def _heads(acc):
    return [acc[:, hh * HEAD_DIM:(hh + 1) * HEAD_DIM] for hh in range(N_HEADS)]


def _proj_qdiff_kernel(a_ref, w_ref, g_ref, q1_ref, q2_ref, wb_ref):
    def epilogue(rows, acc):
        lo = lax.broadcasted_iota(jnp.int32, (PROJ_SUB, HEAD_DIM), 1) < QK_DIFF
        for hh, y in enumerate(_heads(acc)):
            yn = _head_rmsnorm(y, g_ref[...], 2) * (QK_DIFF ** -0.5 * LOG2E)
            q1_ref[hh, 0, :, rows] = jnp.where(lo, yn, 0.0).T.astype(q1_ref.dtype)
            q2_ref[hh, 0, :, rows] = jnp.where(lo, 0.0, yn).T.astype(q2_ref.dtype)

    _proj_subtiles(a_ref, w_ref, wb_ref, pl.program_id(0) == 0, epilogue, w_transposed=True)


def _proj_qfox_kernel(a_ref, w_ref, g_ref, q_ref, wb_ref):
    def epilogue(rows, acc):
        for hh, y in enumerate(_heads(acc)):
            yn = _head_rmsnorm(y, g_ref[...], 1) * (HEAD_DIM ** -0.5 * LOG2E)
            q_ref[hh, 0, :, rows] = yn.T.astype(q_ref.dtype)

    _proj_subtiles(a_ref, w_ref, wb_ref, pl.program_id(0) == 0, epilogue, w_transposed=True)


def _proj_k_kernel(a_ref, w_ref, g_ref, k_ref, wb_ref, *, groups):
    def epilogue(rows, acc):
        for hh, y in enumerate(_heads(acc)):
            k_ref[hh, rows, :] = _head_rmsnorm(y, g_ref[...], groups).astype(k_ref.dtype)

    _proj_subtiles(a_ref, w_ref, wb_ref, pl.program_id(0) == 0, epilogue, w_transposed=True)


def _proj_vt_kernel(a_ref, w_ref, vt_ref, wb_ref):
    row = lax.broadcasted_iota(jnp.int32, (V_ROWS - HEAD_DIM, TKM), 0)
    ones_rows = jnp.where(row == 0, 1.0, 0.0).astype(vt_ref.dtype)
    for hh in range(N_HEADS):
        vt_ref[hh, 0, HEAD_DIM:, :] = ones_rows

    def epilogue(rows, acc):
        for hh, y in enumerate(_heads(acc)):
            vt_ref[hh, 0, :HEAD_DIM, rows] = y.T.astype(vt_ref.dtype)

    _proj_subtiles(a_ref, w_ref, wb_ref, pl.program_id(0) == 0, epilogue, w_transposed=True)


def _proj_gates_kernel(a_ref, w_ref, b_ref, *refs, n_cast):
    cast_in, o_ref, cast_out, wb_ref = refs[:n_cast], refs[n_cast], refs[n_cast + 1:-1], refs[-1]

    def epilogue(rows, acc):
        o_ref[rows, :] = (1.0 / (1.0 + jnp.exp(-(acc + b_ref[...])))).astype(o_ref.dtype)

    _proj_subtiles(a_ref, w_ref, wb_ref, pl.program_id(1) == 0, epilogue,
                   side_work=lambda: _cast_slabs(cast_in, cast_out))


def _proj_call(kernel, h, w, col_block, extras, out_shapes, out_specs, name, tm=TKM):
    s, d = h.shape
    in_specs = [pl.BlockSpec((tm, d), lambda i: (i, 0)),
                pl.BlockSpec((None, WIDTH, d), lambda i: (0, col_block, 0), pipeline_mode=pl.Buffered(1))]
    in_specs += [pl.BlockSpec(e.shape, lambda i: (0, 0)) for e in extras]
    return pl.pallas_call(
        kernel,
        grid=(s // tm,),
        in_specs=in_specs,
        out_specs=out_specs,
        out_shape=out_shapes,
        scratch_shapes=[pltpu.VMEM((d, WIDTH), _bf16)],
        compiler_params=_cparams(("arbitrary",)),
        name=name,
    )(h, w, *extras)


def _t_spec(tm):
    assert tm == TQ
    return pl.BlockSpec((N_HEADS, 1, HEAD_DIM, tm), lambda i: (0, i, 0, 0))


def _k_spec(tm):
    return pl.BlockSpec((N_HEADS, tm, HEAD_DIM), lambda i: (0, i, 0))


def _gates(h, w, b, cast_weights, tm=1024, tn=1024):
    s, d = h.shape
    n = w.shape[2]
    ni = s // tm
    cast_in, cast_out, cast_shapes = _cast_slab_specs(cast_weights, (n // tn) * ni, lambda j, i: j * ni + i)
    outs = pl.pallas_call(
        functools.partial(_proj_gates_kernel, n_cast=len(cast_weights)),
        grid=(n // tn, ni),
        in_specs=[pl.BlockSpec((tm, d), lambda j, i: (i, 0)),
                  pl.BlockSpec((None, d, tn), lambda j, i: (0, 0, j)),
                  pl.BlockSpec((1, tn), lambda j, i: (0, j))] + cast_in,
        out_specs=[pl.BlockSpec((tm, tn), lambda j, i: (i, j))] + cast_out,
        out_shape=[jax.ShapeDtypeStruct((s, n), _bf16)] + cast_shapes,
        scratch_shapes=[pltpu.VMEM((d, tn), _bf16)],
        compiler_params=_cparams(("arbitrary", "arbitrary")),
        name="proj_gates",
    )(h, w, b.reshape(1, n), *cast_weights)
    return outs[0], outs[1:]


F_ROWS = 16
F_CHUNK = 256


def _split3(x):
    hi = x.astype(_bf16)
    r1 = x - hi.astype(_f32)
    mid = r1.astype(_bf16)
    lo = (r1 - mid.astype(_f32)).astype(_bf16)
    return hi, mid, lo


def _select_rows_or_lanes(index, values):
    out = jnp.zeros(index.shape, _f32)
    for i in reversed(range(len(values))):
        out = jnp.where(index == i, values[i].astype(_f32), out)
    return out


def _forget_kernel(wt_ref, x_ref, g_ref, b_ref, h_ref, ft_ref, kx_ref, carry_ref):
    i = pl.program_id(0)

    @pl.when(i == 0)
    def _():
        carry_ref[...] = jnp.zeros_like(carry_ref)

    x = x_ref[...]
    r = lax.rsqrt(jnp.mean(x * x, axis=-1, keepdims=True) + EPS)
    h = ((x * r) * g_ref[...]).astype(h_ref.dtype)
    h_ref[...] = h
    wt8 = wt_ref[...]
    wt = jnp.concatenate([wt8, jnp.zeros_like(wt8)], axis=0).astype(_bf16)
    z = lax.dot_general(wt, h, (((1,), (1,)), ((), ())),
                        preferred_element_type=_f32) + b_ref[...]
    logf = (jnp.minimum(z, 0.0) - jnp.log(1.0 + jnp.exp(-jnp.abs(z)))) * LOG2E
    tm = logf.shape[1]
    r = lax.broadcasted_iota(jnp.int32, (F_CHUNK, F_CHUNK), 0)
    c = lax.broadcasted_iota(jnp.int32, (F_CHUNK, F_CHUNK), 1)
    upper = jnp.where(r <= c, 1.0, 0.0).astype(_bf16)
    carry = carry_ref[...]
    lane = lax.broadcasted_iota(jnp.int32, (HEAD_DIM, HEAD_DIM), 1)
    for ch in range(tm // F_CHUNK):
        x = logf[:, ch * F_CHUNK:(ch + 1) * F_CHUNK]
        hi, mid, lo = _split3(x)
        pre = _dot(hi, upper) + _dot(mid, upper) + _dot(lo, upper) + carry
        ft_ref[:, ch * F_CHUNK:(ch + 1) * F_CHUNK] = pre
        carry = pre[:, F_CHUNK - 1:F_CHUNK]
        for hh in range(N_HEADS):
            for sub in range(F_CHUNK // HEAD_DIM):
                row = pre[hh:hh + 1, sub * HEAD_DIM:(sub + 1) * HEAD_DIM]
                neg = -jnp.broadcast_to(row, (HEAD_DIM, HEAD_DIM)).T
                base = ch * F_CHUNK + sub * HEAD_DIM
                kx_ref[hh, base:base + HEAD_DIM, :] = _select_rows_or_lanes(
                    lane, _split3(neg)).astype(kx_ref.dtype)
    carry_ref[...] = carry


def _norm_and_forget(x, g, w_in_t, b_col, tm=1024):
    s, d = x.shape
    f_block = 6 * WIDTH // N_HEADS
    return pl.pallas_call(
        _forget_kernel,
        grid=(s // tm,),
        in_specs=[pl.BlockSpec((None, N_HEADS, d), lambda i: (0, f_block, 0)),
                  pl.BlockSpec((tm, d), lambda i: (i, 0)),
                  pl.BlockSpec((1, d), lambda i: (0, 0)),
                  pl.BlockSpec((F_ROWS, 1), lambda i: (0, 0))],
        out_specs=[pl.BlockSpec((tm, d), lambda i: (i, 0)),
                   pl.BlockSpec((F_ROWS, tm), lambda i: (0, i)),
                   pl.BlockSpec((N_HEADS, tm, HEAD_DIM), lambda i: (0, i, 0))],
        out_shape=[jax.ShapeDtypeStruct((s, d), _bf16),
                   jax.ShapeDtypeStruct((F_ROWS, s), _f32),
                   jax.ShapeDtypeStruct((N_HEADS, s, HEAD_DIM), _bf16)],
        scratch_shapes=[pltpu.VMEM((F_ROWS, 1), _f32)],
        compiler_params=_cparams(("arbitrary",)),
        name="norm_forget_gate",
    )(w_in_t, x, g.reshape(1, d), b_col)


def _block_partial(z, vt):
    mloc = jnp.max(z, axis=0, keepdims=True)
    o = _dot(vt, jnp.exp2(z - mloc).astype(_bf16))
    return mloc, o[HEAD_DIM:HEAD_DIM + 1], o[:HEAD_DIM]


def _merge_partials(m_ref, l_ref, acc_ref, idx, c, parts):
    cs = slice(c * TQS, (c + 1) * TQS)
    m_old = m_ref[idx, :, cs]
    m_new = m_old
    for mt, _, _ in parts:
        m_new = jnp.maximum(m_new, mt)
    a = jnp.exp2(m_old - m_new)
    l_new = a * l_ref[idx, :, cs]
    acc_new = a * acc_ref[idx, :, cs]
    for mt, l, o in parts:
        b = jnp.exp2(mt - m_new)
        l_new = l_new + b * l
        acc_new = acc_new + b * o
    m_ref[idx, :, cs] = m_new
    l_ref[idx, :, cs] = l_new
    acc_ref[idx, :, cs] = acc_new


def _init_state(m_ref, l_ref, acc_ref):
    m_ref[...] = jnp.full_like(m_ref, NEG_BIG)
    l_ref[...] = jnp.zeros_like(l_ref)
    acc_ref[...] = jnp.zeros_like(acc_ref)


def _head_sweep(scores_cols, main_chain, band_row, start_tile, finish_tile, sa_ref, sb_ref, sc_ref):
    n_maps = len(sa_ref)
    n_tiles = SEQ // TQ

    def n_keys(c, band):
        return (c + 1) * TK if band else TKM

    def fill_cols(refs, tile, block, c, band=False):
        for ref, val in zip(refs, scores_cols(tile, block, c, n_keys(c, band))):
            ref[:n_keys(c, band), c * TQS:(c + 1) * TQS] = val

    def overlapped(tile, cur_block, cur_refs, nxt_block, nxt_refs, band=False):
        for c in range(NSUB):
            for idx in range(n_maps):
                nxt_refs[idx][:n_keys(c, band), c * TQS:(c + 1) * TQS] = scores_cols(
                    tile, nxt_block, c, n_keys(c, band))[idx]
                main_chain(tile, cur_block, cur_refs, c, idx)

    for c in range(NSUB):
        fill_cols(sc_ref, 0, 0, c, band=True)
    start_tile()

    def tile_body(qi, carry):
        def pair(t, carry2):
            overlapped(qi, 2 * t, sa_ref, 2 * t + 1, sb_ref)
            overlapped(qi, 2 * t + 1, sb_ref, 2 * t + 2, sa_ref)
            return carry2

        n_pairs = lax.shift_right_logical(jnp.maximum(qi - 1, 0), 1)
        lax.fori_loop(0, n_pairs, pair, 0)
        last = 2 * n_pairs

        @pl.when(qi - last == 2)
        def _():
            overlapped(qi, last, sa_ref, last + 1, sb_ref)
            overlapped(qi, last + 1, sb_ref, qi, sc_ref, band=True)

        @pl.when(qi - last == 1)
        def _():
            overlapped(qi, last, sa_ref, qi, sc_ref, band=True)

        nxt = jnp.minimum(qi + 1, n_tiles - 1)
        for j in range(NSUB):
            band_row(qi, j, sc_ref)
            fill_cols(sa_ref, nxt, 0, j)
        finish_tile(qi)
        start_tile()
        return carry

    lax.fori_loop(0, n_tiles, tile_body, 0)


def _local_iotas():
    lane = lax.broadcasted_iota(jnp.int32, (TK, TQS), 1)
    sub = lax.broadcasted_iota(jnp.int32, (TK, TQS), 0)
    return lane, sub


def _diff_attn_kernel(slope_ref, q1_ref, q2_ref, k_ref, vt_ref, lq1_ref, lk1_ref, lq2_ref, lk2_ref,
                      gsub_ref, o_ref, m_ref, l_ref, acc_ref, *s_refs):
    h = pl.program_id(0)
    neg_slope = -slope_ref[h] * LOG2E
    q_refs = (q1_ref, q2_ref)

    slope2 = -neg_slope
    krow = lax.broadcasted_iota(jnp.int32, (TKM, HEAD_DIM), 0)
    klane = lax.broadcasted_iota(jnp.int32, (TKM, HEAD_DIM), 1)
    kx = jnp.where(klane < 3, (krow // HEAD_DIM).astype(_f32),
                   jnp.where(klane < 6, (krow % HEAD_DIM).astype(_f32), 0.0)).astype(_bf16)
    qrow = lax.broadcasted_iota(jnp.int32, (HEAD_DIM, TQS), 0)
    ones = jnp.ones((HEAD_DIM, TQS), _f32)
    qx = _select_rows_or_lanes(qrow, _split3(ones * (slope2 * HEAD_DIM)) + _split3(ones * slope2)).astype(_bf16)
    t_loc = lax.broadcasted_iota(jnp.int32, (1, TQS), 1).astype(_f32)
    lane, sub = _local_iotas()
    fix_diag = jnp.where((sub // 64) <= (lane // 64),
                         (sub.astype(_f32) + jnp.abs((lane - sub).astype(_f32))) * neg_slope, NEG_BIG)
    lam = (jnp.exp(jnp.sum(lq1_ref[...] * lk1_ref[...], axis=-1, keepdims=True))
           - jnp.exp(jnp.sum(lq2_ref[...] * lk2_ref[...], axis=-1, keepdims=True))
           + LAMBDA_INIT)

    def scores_cols(tile, block, c, nk):
        start = pl.multiple_of(block * TKM, TKM)
        k_aug = jnp.concatenate([k_ref[0, pl.ds(start, nk), :], kx[:nk]], axis=1)
        cs = slice(c * TQS, (c + 1) * TQS)
        return tuple(_dot(k_aug, jnp.concatenate([q_ref[0, tile, :, cs], qx], axis=0)) for q_ref in q_refs)

    def query_shift(tile, block, c):
        return (jnp.asarray((tile - block) * TQ + c * TQS, _f32) + t_loc) * neg_slope

    def main_chain(tile, block, s_refs, c, idx):
        vt = vt_ref[0, block]
        mloc, l, o = _block_partial(s_refs[idx][:, c * TQS:(c + 1) * TQS], vt)
        _merge_partials(m_ref, l_ref, acc_ref, idx, c, [(mloc + query_shift(tile, block, c), l, o)])

    def band_row(tile, j, s_refs):
        vt = vt_ref[0, tile, :, j * TK:(j + 1) * TK]
        for idx in range(2):
            for c in range(j, NSUB):
                z = s_refs[idx][j * TK:(j + 1) * TK, c * TQS:(c + 1) * TQS]
                if c == j:
                    mloc, l, o = _block_partial(z + fix_diag, vt)
                    part = (mloc + (j * TK) * neg_slope, l, o)
                else:
                    mloc, l, o = _block_partial(z, vt)
                    part = (mloc + query_shift(tile, tile, c), l, o)
                _merge_partials(m_ref, l_ref, acc_ref, idx, c, [part])

    def finish_tile(tile):
        o = acc_ref[0] / l_ref[0] - lam * (acc_ref[1] / l_ref[1])
        r = lax.rsqrt(jnp.mean(o * o, axis=0, keepdims=True) + EPS)
        y = (o * r) * gsub_ref[...] * (1.0 - LAMBDA_INIT)
        o_ref[pl.ds(pl.multiple_of(tile * TQ, TQ), TQ), :] = y.T.astype(o_ref.dtype)

    n = len(s_refs) // 3
    _head_sweep(scores_cols, main_chain, band_row, lambda: _init_state(m_ref, l_ref, acc_ref), finish_tile,
                s_refs[:n], s_refs[n:2 * n], s_refs[2 * n:])


def _head_specs(n_q):
    nq = SEQ // TQ
    q_spec = pl.BlockSpec((1, nq, HEAD_DIM, TQ), lambda h: (h, 0, 0, 0))
    return [q_spec] * n_q + [pl.BlockSpec((1, SEQ, HEAD_DIM), lambda h: (h, 0, 0)),
                             pl.BlockSpec((1, SEQ // TKM, V_ROWS, TKM), lambda h: (h, 0, 0, 0))]


def _head_scratch(n_maps):
    return [pltpu.VMEM((n_maps, 1, TQ), _f32),
            pltpu.VMEM((n_maps, 1, TQ), _f32),
            pltpu.VMEM((n_maps, HEAD_DIM, TQ), _f32)] + [pltpu.VMEM((TKM, TQ + HEAD_DIM), _f32)] * (3 * n_maps)


def _diff_attention(slopes, q1t, q2t, k, vt, lq1, lk1, lq2, lk2, gsub_col):
    vec = lambda: pl.BlockSpec((1, QK_DIFF), lambda h: (0, 0))
    q_specs = _head_specs(2)
    return pl.pallas_call(
        _diff_attn_kernel,
        grid=(N_HEADS,),
        in_specs=[pl.BlockSpec(memory_space=pltpu.SMEM)] + q_specs + [
            vec(), vec(), vec(), vec(), pl.BlockSpec((HEAD_DIM, 1), lambda h: (0, 0))],
        out_specs=pl.BlockSpec((SEQ, HEAD_DIM), lambda h: (0, h)),
        out_shape=jax.ShapeDtypeStruct((SEQ, WIDTH), _bf16),
        scratch_shapes=_head_scratch(2),
        compiler_params=_cparams(("parallel",)),
        name="diff_attention",
    )(slopes, q1t, q2t, k, vt, lq1, lk1, lq2, lk2, gsub_col)


def _fox_attn_kernel(q_ref, k_ref, vt_ref, ft_ref, kx_ref, o_ref, m_ref, l_ref, acc_ref, *s_refs):
    qrow = lax.broadcasted_iota(jnp.int32, (HEAD_DIM, TQS), 0)
    ones_rows = jnp.where(qrow < 3, 1.0, 0.0).astype(_bf16)
    lane, sub = _local_iotas()
    causal = sub <= lane

    def scores_cols(tile, block, c, nk):
        start = pl.multiple_of(block * TKM, TKM)
        k_aug = jnp.concatenate([k_ref[0, pl.ds(start, nk), :], kx_ref[0, pl.ds(start, nk), :]], axis=1)
        q_aug = jnp.concatenate([q_ref[0, tile, :, c * TQS:(c + 1) * TQS], ones_rows], axis=0)
        return (_dot(k_aug, q_aug),)

    def main_chain(tile, block, s_ref, c, idx):
        cs = slice(c * TQS, (c + 1) * TQS)
        mloc, l, o = _block_partial(s_ref[0][:, cs], vt_ref[0, block])
        _merge_partials(m_ref, l_ref, acc_ref, 0, c, [(mloc + ft_ref[0, tile, :, cs], l, o)])

    def band_row(tile, j, s_ref):
        vt = vt_ref[0, tile, :, j * TK:(j + 1) * TK]
        for c in range(j, NSUB):
            cs = slice(c * TQS, (c + 1) * TQS)
            z = s_ref[0][j * TK:(j + 1) * TK, cs]
            if c == j:
                z = jnp.where(causal, z, NEG_BIG)
            mloc, l, o = _block_partial(z, vt)
            _merge_partials(m_ref, l_ref, acc_ref, 0, c, [(mloc + ft_ref[0, tile, :, cs], l, o)])

    def finish_tile(tile):
        o = acc_ref[0] / l_ref[0]
        o_ref[pl.ds(pl.multiple_of(tile * TQ, TQ), TQ), :] = o.T.astype(o_ref.dtype)

    _head_sweep(scores_cols, main_chain, band_row, lambda: _init_state(m_ref, l_ref, acc_ref), finish_tile,
                s_refs[:1], s_refs[1:2], s_refs[2:])


def _fox_attention(qt, k, vt, ft4, kx):
    nq = SEQ // TQ
    return pl.pallas_call(
        _fox_attn_kernel,
        grid=(N_HEADS,),
        in_specs=_head_specs(1) + [pl.BlockSpec((1, nq, 1, TQ), lambda h: (h, 0, 0, 0)),
                                   pl.BlockSpec((1, SEQ, HEAD_DIM), lambda h: (h, 0, 0))],
        out_specs=pl.BlockSpec((SEQ, HEAD_DIM), lambda h: (0, h)),
        out_shape=jax.ShapeDtypeStruct((SEQ, WIDTH), _bf16),
        scratch_shapes=_head_scratch(1),
        compiler_params=_cparams(("parallel",)),
        name="fox_attention",
    )(qt, k, vt, ft4, kx)


def _merge_kernel(oa_ref, ob_ref, g_ref, x_ref, wbd_ref, wbf_ref, wout_ref, gm_ref, x1_ref, h2_ref):
    for sub in range(oa_ref.shape[0] // PROJ_SUB):
        rows = slice(sub * PROJ_SUB, (sub + 1) * PROJ_SUB)
        a = _dot(oa_ref[rows, :], wbd_ref[...])
        b = _dot(ob_ref[rows, :], wbf_ref[...])
        g = g_ref[rows, :].astype(_f32)
        merged = g[:, :D_MODEL] * a + g[:, D_MODEL:] * b
        x1 = x_ref[rows, :] + _dot(merged.astype(_bf16), wout_ref[...])
        x1_ref[rows, :] = x1
        r = lax.rsqrt(jnp.mean(x1 * x1, axis=-1, keepdims=True) + EPS)
        h2_ref[rows, :] = ((x1 * r) * gm_ref[...]).astype(h2_ref.dtype)


def _merge(oa, ob, gates, x, wbd, wbf, wout, gm, tm=512):
    s, d = x.shape
    const = lambda shape: pl.BlockSpec(shape, lambda i: (0, 0), pipeline_mode=pl.Buffered(1))
    return pl.pallas_call(
        _merge_kernel,
        grid=(s // tm,),
        in_specs=[pl.BlockSpec((tm, WIDTH), lambda i: (i, 0)),
                  pl.BlockSpec((tm, WIDTH), lambda i: (i, 0)),
                  pl.BlockSpec((tm, 2 * d), lambda i: (i, 0)),
                  pl.BlockSpec((tm, d), lambda i: (i, 0)),
                  const((WIDTH, d)), const((WIDTH, d)), const((d, d)),
                  pl.BlockSpec((1, d), lambda i: (0, 0))],
        out_specs=[pl.BlockSpec((tm, d), lambda i: (i, 0)),
                   pl.BlockSpec((tm, d), lambda i: (i, 0))],
        out_shape=[jax.ShapeDtypeStruct((s, d), _f32),
                   jax.ShapeDtypeStruct((s, d), _bf16)],
        compiler_params=_cparams(("parallel",)),
        name="merge_out_proj",
    )(oa, ob, gates, x, wbd, wbf, wout, gm.reshape(1, d))


def _mlp_kernel(h_ref, wu_ref, wd_ref, x_ref, o_ref):
    k = pl.program_id(1)

    @pl.when(k == 0)
    def _():
        o_ref[...] = x_ref[...]

    for sub in range(h_ref.shape[0] // PROJ_SUB):
        rows = slice(sub * PROJ_SUB, (sub + 1) * PROJ_SUB)
        u = jnp.maximum(_dot(h_ref[rows, :], wu_ref[...]), 0.0)
        o_ref[rows, :] += _dot((u * u).astype(_bf16), wd_ref[...])


def _mlp(h2, wu, wd, x1, tm=512, tf=2048):
    s, d = x1.shape
    f = wu.shape[1]
    return pl.pallas_call(
        _mlp_kernel,
        grid=(s // tm, f // tf),
        in_specs=[pl.BlockSpec((tm, d), lambda i, k: (i, 0)),
                  pl.BlockSpec((d, tf), lambda i, k: (0, k)),
                  pl.BlockSpec((tf, d), lambda i, k: (k, 0)),
                  pl.BlockSpec((tm, d), lambda i, k: (i, 0))],
        out_specs=pl.BlockSpec((tm, d), lambda i, k: (i, 0)),
        out_shape=jax.ShapeDtypeStruct((s, d), _f32),
        compiler_params=_cparams(("parallel", "arbitrary")),
        name="mlp_relu2",
    )(h2, wu, wd, x1)


def kernel(x, norm_mix, w_in, b_forget, qnorm_diff, knorm_diff, lambda_q1, lambda_k1, lambda_q2, lambda_k2,
           subln_diff, qnorm_fox, knorm_fox, w_branch_diff, w_branch_fox, w_gate, b_gate, w_out, norm_mlp,
           w_mlp_up, w_mlp_down):
    assert x.shape == (1, SEQ, D_MODEL)
    x2 = x[0]
    w_in_t = jnp.swapaxes(w_in, 1, 2)

    bf_col = jnp.zeros((F_ROWS, 1), _f32).at[:N_HEADS, 0].set(b_forget[0])
    h, ft, kx_fox = _norm_and_forget(x2, norm_mix[0], w_in_t, bf_col)
    ft4 = ft.reshape(F_ROWS, SEQ // TQ, 1, TQ)

    g_qd = jnp.tile(qnorm_diff[0], 2).reshape(1, HEAD_DIM)
    g_kd = jnp.tile(knorm_diff[0], 2).reshape(1, HEAD_DIM)
    tm = TKM
    t_shape = jax.ShapeDtypeStruct((N_HEADS, SEQ // TQ, HEAD_DIM, TQ), _bf16)
    k_shape = jax.ShapeDtypeStruct((N_HEADS, SEQ, HEAD_DIM), _bf16)
    vt_shape = jax.ShapeDtypeStruct((N_HEADS, SEQ // TKM, V_ROWS, TKM), _bf16)
    vt_spec = pl.BlockSpec((N_HEADS, 1, V_ROWS, TKM), lambda i: (0, i, 0, 0))

    q1t, q2t = _proj_call(_proj_qdiff_kernel, h, w_in_t, 0, [g_qd], [t_shape, t_shape],
                          [_t_spec(tm), _t_spec(tm)], "proj_q_diff")
    ka = _proj_call(functools.partial(_proj_k_kernel, groups=2), h, w_in_t, 1, [g_kd], k_shape,
                    _k_spec(tm), "proj_k_diff")
    vat = _proj_call(_proj_vt_kernel, h, w_in_t, 2, [], vt_shape, vt_spec, "proj_v_diff")
    qbt = _proj_call(_proj_qfox_kernel, h, w_in_t, 3, [qnorm_fox[0].reshape(1, HEAD_DIM)], t_shape,
                     _t_spec(tm), "proj_q_fox")
    kb = _proj_call(functools.partial(_proj_k_kernel, groups=1), h, w_in_t, 4,
                    [knorm_fox[0].reshape(1, HEAD_DIM)], k_shape, _k_spec(tm), "proj_k_fox")
    vbt = _proj_call(_proj_vt_kernel, h, w_in_t, 5, [], vt_shape, vt_spec, "proj_v_fox")

    gates, (wbd, wbf, wout, wup, wdown) = _gates(
        h, w_gate, b_gate[0], [w_branch_diff, w_branch_fox, w_out, w_mlp_up, w_mlp_down])

    slopes = 2.0 ** (-8.0 * jnp.arange(1, N_HEADS + 1, dtype=_f32) / N_HEADS)
    row = lambda v: v[0].reshape(1, QK_DIFF)
    oa = _diff_attention(slopes, q1t, q2t, ka, vat, row(lambda_q1), row(lambda_k1), row(lambda_q2),
                         row(lambda_k2), subln_diff[0].reshape(HEAD_DIM, 1))
    ob = _fox_attention(qbt, kb, vbt, ft4, kx_fox)

    x1, h2 = _merge(oa, ob, gates, x2, wbd, wbf, wout, norm_mlp[0])
    out = _mlp(h2, wup, wdown, x1)
    return out[None]
```

```python
import functools

import jax
import jax.numpy as jnp
from jax import lax
from jax.experimental import pallas as pl
from jax.experimental.pallas import tpu as pltpu

D_MODEL = 2048
SEQ = 8192
HEAD_DIM = 128
N_HEADS = 8
QK_DIFF = 64
WIDTH = N_HEADS * HEAD_DIM
D_FF = 4 * D_MODEL
EPS = 1e-6
LAMBDA_INIT = 0.8 - 0.6 * 1.0
NEG_BIG = -1e30

TQ = 1024
TQS = 256
TK = 256
TKM = TQ
NSUB = TQ // TQS
LOG2E = 1.4426950408889634
V_ROWS = HEAD_DIM + 16
VMEM_LIMIT = 56 * 1024 * 1024

_f32 = jnp.float32
_bf16 = jnp.bfloat16


def _cparams(sem):
    return pltpu.CompilerParams(dimension_semantics=sem, vmem_limit_bytes=VMEM_LIMIT)


def _dot(a, b):
    return jnp.dot(a, b, preferred_element_type=_f32)


def _cast_slab_specs(weights, nsteps, step_of):
    in_specs, out_specs, out_shapes = [], [], []
    for w in weights:
        _, rows, cols = w.shape
        slab = rows // nsteps
        assert slab * nsteps == rows and slab % 16 == 0
        in_specs.append(pl.BlockSpec((None, slab, cols), lambda *ids: (0, step_of(*ids), 0)))
        out_specs.append(pl.BlockSpec((slab, cols), lambda *ids: (step_of(*ids), 0)))
        out_shapes.append(jax.ShapeDtypeStruct((rows, cols), _bf16))
    return in_specs, out_specs, out_shapes


def _cast_slabs(src_refs, dst_refs):
    for src, dst in zip(src_refs, dst_refs):
        dst[...] = src[...].astype(dst.dtype)


def _head_rmsnorm(y, gain_row, groups):
    y2 = y * y
    if groups == 1:
        r = lax.rsqrt(jnp.mean(y2, axis=-1, keepdims=True) + EPS)
    else:
        lane = lax.broadcasted_iota(jnp.int32, y.shape, 1)
        lo = lane < QK_DIFF
        s_lo = jnp.sum(jnp.where(lo, y2, 0.0), axis=-1, keepdims=True)
        s_hi = jnp.sum(jnp.where(lo, 0.0, y2), axis=-1, keepdims=True)
        r = jnp.where(lo, lax.rsqrt(s_lo * (1.0 / QK_DIFF) + EPS),
                      lax.rsqrt(s_hi * (1.0 / QK_DIFF) + EPS))
    return (y * r) * gain_row


PROJ_SUB = 256


def _proj_subtiles(a_ref, w_ref, wb_ref, first_step, epilogue, w_transposed=False, side_work=None):
    @pl.when(first_step)
    def _():
        w = w_ref[...].T if w_transposed else w_ref[...]
        wb_ref[...] = w.astype(wb_ref.dtype)

    if side_work is not None:
        side_work()
    for sub in range(a_ref.shape[0] // PROJ_SUB):
        rows = slice(sub * PROJ_SUB, (sub + 1) * PROJ_SUB)
        epilogue(rows, _dot(a_ref[rows, :], wb_ref[...]))


def _heads(acc):
    return [acc[:, hh * HEAD_DIM:(hh + 1) * HEAD_DIM] for hh in range(N_HEADS)]


def _tile_cols(rows):
    blk, col = divmod(rows.start, TQ)
    return blk, slice(col, col + PROJ_SUB)


def _proj_qdiff_kernel(a_ref, w_ref, g_ref, q1_ref, q2_ref, wb_ref):
    def epilogue(rows, acc):
        lo = lax.broadcasted_iota(jnp.int32, (PROJ_SUB, HEAD_DIM), 1) < QK_DIFF
        blk, cols = _tile_cols(rows)
        for hh, y in enumerate(_heads(acc)):
            yn = _head_rmsnorm(y, g_ref[...], 2) * (QK_DIFF ** -0.5 * LOG2E)
            q1_ref[hh, blk, :, cols] = jnp.where(lo, yn, 0.0).T.astype(q1_ref.dtype)
            q2_ref[hh, blk, :, cols] = jnp.where(lo, 0.0, yn).T.astype(q2_ref.dtype)

    _proj_subtiles(a_ref, w_ref, wb_ref, pl.program_id(0) == 0, epilogue, w_transposed=True)


def _proj_qfox_kernel(a_ref, w_ref, g_ref, q_ref, wb_ref):
    def epilogue(rows, acc):
        blk, cols = _tile_cols(rows)
        for hh, y in enumerate(_heads(acc)):
            yn = _head_rmsnorm(y, g_ref[...], 1) * (HEAD_DIM ** -0.5 * LOG2E)
            q_ref[hh, blk, :, cols] = yn.T.astype(q_ref.dtype)

    _proj_subtiles(a_ref, w_ref, wb_ref, pl.program_id(0) == 0, epilogue, w_transposed=True)


def _proj_k_kernel(a_ref, w_ref, g_ref, k_ref, wb_ref, *, groups):
    def epilogue(rows, acc):
        for hh, y in enumerate(_heads(acc)):
            k_ref[hh, rows, :] = _head_rmsnorm(y, g_ref[...], groups).astype(k_ref.dtype)

    _proj_subtiles(a_ref, w_ref, wb_ref, pl.program_id(0) == 0, epilogue, w_transposed=True)


def _proj_vt_kernel(a_ref, w_ref, vt_ref, wb_ref):
    row = lax.broadcasted_iota(jnp.int32, (V_ROWS - HEAD_DIM, TKM), 0)
    ones_rows = jnp.where(row == 0, 1.0, 0.0).astype(vt_ref.dtype)
    for hh in range(N_HEADS):
        for blk in range(vt_ref.shape[1]):
            vt_ref[hh, blk, HEAD_DIM:, :] = ones_rows

    def epilogue(rows, acc):
        blk, cols = _tile_cols(rows)
        for hh, y in enumerate(_heads(acc)):
            vt_ref[hh, blk, :HEAD_DIM, cols] = y.T.astype(vt_ref.dtype)

    _proj_subtiles(a_ref, w_ref, wb_ref, pl.program_id(0) == 0, epilogue, w_transposed=True)


def _proj_gates_kernel(a_ref, w_ref, b_ref, *refs, n_cast):
    cast_in, o_ref, cast_out, wb_ref = refs[:n_cast], refs[n_cast], refs[n_cast + 1:-1], refs[-1]

    def epilogue(rows, acc):
        o_ref[rows, :] = (1.0 / (1.0 + jnp.exp(-(acc + b_ref[...])))).astype(o_ref.dtype)

    _proj_subtiles(a_ref, w_ref, wb_ref, pl.program_id(1) == 0, epilogue,
                   side_work=lambda: _cast_slabs(cast_in, cast_out))


PROJ_TM = 2 * TQ


def _proj_call(kernel, h, w, col_block, extras, out_shapes, out_specs, name, tm=PROJ_TM):
    s, d = h.shape
    in_specs = [pl.BlockSpec((tm, d), lambda i: (i, 0)),
                pl.BlockSpec((None, WIDTH, d), lambda i: (0, col_block, 0), pipeline_mode=pl.Buffered(1))]
    in_specs += [pl.BlockSpec(e.shape, lambda i: (0, 0)) for e in extras]
    return pl.pallas_call(
        kernel,
        grid=(s // tm,),
        in_specs=in_specs,
        out_specs=out_specs,
        out_shape=out_shapes,
        scratch_shapes=[pltpu.VMEM((d, WIDTH), _bf16)],
        compiler_params=_cparams(("arbitrary",)),
        name=name,
    )(h, w, *extras)


def _t_spec(tm):
    assert tm % TQ == 0
    return pl.BlockSpec((N_HEADS, tm // TQ, HEAD_DIM, TQ), lambda i: (0, i, 0, 0))


def _k_spec(tm):
    return pl.BlockSpec((N_HEADS, tm, HEAD_DIM), lambda i: (0, i, 0))


def _gates(h, w, b, cast_weights, tm=1024, tn=1024):
    s, d = h.shape
    n = w.shape[2]
    ni = s // tm
    cast_in, cast_out, cast_shapes = _cast_slab_specs(cast_weights, (n // tn) * ni, lambda j, i: j * ni + i)
    outs = pl.pallas_call(
        functools.partial(_proj_gates_kernel, n_cast=len(cast_weights)),
        grid=(n // tn, ni),
        in_specs=[pl.BlockSpec((tm, d), lambda j, i: (i, 0)),
                  pl.BlockSpec((None, d, tn), lambda j, i: (0, 0, j)),
                  pl.BlockSpec((1, tn), lambda j, i: (0, j))] + cast_in,
        out_specs=[pl.BlockSpec((tm, tn), lambda j, i: (i, j))] + cast_out,
        out_shape=[jax.ShapeDtypeStruct((s, n), _bf16)] + cast_shapes,
        scratch_shapes=[pltpu.VMEM((d, tn), _bf16)],
        compiler_params=_cparams(("arbitrary", "arbitrary")),
        name="proj_gates",
    )(h, w, b.reshape(1, n), *cast_weights)
    return outs[0], outs[1:]


F_ROWS = 16
F_CHUNK = 256


def _split3(x):
    hi = x.astype(_bf16)
    r1 = x - hi.astype(_f32)
    mid = r1.astype(_bf16)
    lo = (r1 - mid.astype(_f32)).astype(_bf16)
    return hi, mid, lo


def _select_rows_or_lanes(index, values):
    out = jnp.zeros(index.shape, _f32)
    for i in reversed(range(len(values))):
        out = jnp.where(index == i, values[i].astype(_f32), out)
    return out


def _forget_kernel(wt_ref, x_ref, g_ref, b_ref, h_ref, ft_ref, kx_ref, carry_ref):
    i = pl.program_id(0)

    @pl.when(i == 0)
    def _():
        carry_ref[...] = jnp.zeros_like(carry_ref)

    x = x_ref[...]
    r = lax.rsqrt(jnp.mean(x * x, axis=-1, keepdims=True) + EPS)
    h = ((x * r) * g_ref[...]).astype(h_ref.dtype)
    h_ref[...] = h
    wt8 = wt_ref[...]
    wt = jnp.concatenate([wt8, jnp.zeros_like(wt8)], axis=0).astype(_bf16)
    z = lax.dot_general(wt, h, (((1,), (1,)), ((), ())),
                        preferred_element_type=_f32) + b_ref[...]
    logf = (jnp.minimum(z, 0.0) - jnp.log(1.0 + jnp.exp(-jnp.abs(z)))) * LOG2E
    tm = logf.shape[1]
    r = lax.broadcasted_iota(jnp.int32, (F_CHUNK, F_CHUNK), 0)
    c = lax.broadcasted_iota(jnp.int32, (F_CHUNK, F_CHUNK), 1)
    upper = jnp.where(r <= c, 1.0, 0.0).astype(_bf16)
    carry = carry_ref[...]
    lane = lax.broadcasted_iota(jnp.int32, (HEAD_DIM, HEAD_DIM), 1)
    for ch in range(tm // F_CHUNK):
        x = logf[:, ch * F_CHUNK:(ch + 1) * F_CHUNK]
        hi, mid, lo = _split3(x)
        pre = _dot(hi, upper) + _dot(mid, upper) + _dot(lo, upper) + carry
        ft_ref[:, ch * F_CHUNK:(ch + 1) * F_CHUNK] = pre
        carry = pre[:, F_CHUNK - 1:F_CHUNK]
        for hh in range(N_HEADS):
            for sub in range(F_CHUNK // HEAD_DIM):
                row = pre[hh:hh + 1, sub * HEAD_DIM:(sub + 1) * HEAD_DIM]
                neg = -jnp.broadcast_to(row, (HEAD_DIM, HEAD_DIM)).T
                base = ch * F_CHUNK + sub * HEAD_DIM
                kx_ref[hh, base:base + HEAD_DIM, :] = _select_rows_or_lanes(
                    lane, _split3(neg)).astype(kx_ref.dtype)
    carry_ref[...] = carry


def _norm_and_forget(x, g, w_in_t, b_col, tm=1024):
    s, d = x.shape
    f_block = 6 * WIDTH // N_HEADS
    return pl.pallas_call(
        _forget_kernel,
        grid=(s // tm,),
        in_specs=[pl.BlockSpec((None, N_HEADS, d), lambda i: (0, f_block, 0)),
                  pl.BlockSpec((tm, d), lambda i: (i, 0)),
                  pl.BlockSpec((1, d), lambda i: (0, 0)),
                  pl.BlockSpec((F_ROWS, 1), lambda i: (0, 0))],
        out_specs=[pl.BlockSpec((tm, d), lambda i: (i, 0)),
                   pl.BlockSpec((F_ROWS, tm), lambda i: (0, i)),
                   pl.BlockSpec((N_HEADS, tm, HEAD_DIM), lambda i: (0, i, 0))],
        out_shape=[jax.ShapeDtypeStruct((s, d), _bf16),
                   jax.ShapeDtypeStruct((F_ROWS, s), _f32),
                   jax.ShapeDtypeStruct((N_HEADS, s, HEAD_DIM), _bf16)],
        scratch_shapes=[pltpu.VMEM((F_ROWS, 1), _f32)],
        compiler_params=_cparams(("arbitrary",)),
        name="norm_forget_gate",
    )(w_in_t, x, g.reshape(1, d), b_col)


def _block_partial(z, vt):
    mloc = jnp.max(z, axis=0, keepdims=True)
    o = _dot(vt, jnp.exp2(z - mloc).astype(_bf16))
    return mloc, o[HEAD_DIM:HEAD_DIM + 1], o[:HEAD_DIM]


def _merge_partials(m_ref, l_ref, acc_ref, idx, c, parts):
    cs = slice(c * TQS, (c + 1) * TQS)
    m_old = m_ref[idx, :, cs]
    m_new = m_old
    for mt, _, _ in parts:
        m_new = jnp.maximum(m_new, mt)
    a = jnp.exp2(m_old - m_new)
    l_new = a * l_ref[idx, :, cs]
    acc_new = a * acc_ref[idx, :, cs]
    for mt, l, o in parts:
        b = jnp.exp2(mt - m_new)
        l_new = l_new + b * l
        acc_new = acc_new + b * o
    m_ref[idx, :, cs] = m_new
    l_ref[idx, :, cs] = l_new
    acc_ref[idx, :, cs] = acc_new


def _init_state(m_ref, l_ref, acc_ref):
    m_ref[...] = jnp.full_like(m_ref, NEG_BIG)
    l_ref[...] = jnp.zeros_like(l_ref)
    acc_ref[...] = jnp.zeros_like(acc_ref)


def _head_sweep(scores_cols, main_chain, band_row, start_tile, finish_tile, sa_ref, sb_ref, sc_ref):
    n_maps = len(sa_ref)
    n_tiles = SEQ // TQ

    def n_keys(c, band):
        return (c + 1) * TK if band else TKM

    def fill_cols(refs, tile, block, c, band=False):
        for ref, val in zip(refs, scores_cols(tile, block, c, n_keys(c, band))):
            ref[:n_keys(c, band), c * TQS:(c + 1) * TQS] = val

    def overlapped(tile, cur_block, cur_refs, nxt_block, nxt_refs, band=False):
        for c in range(NSUB):
            for idx in range(n_maps):
                nxt_refs[idx][:n_keys(c, band), c * TQS:(c + 1) * TQS] = scores_cols(
                    tile, nxt_block, c, n_keys(c, band))[idx]
                main_chain(tile, cur_block, cur_refs, c, idx)

    for c in range(NSUB):
        fill_cols(sc_ref, 0, 0, c, band=True)
    start_tile()

    def tile_body(qi, carry):
        def pair(t, carry2):
            overlapped(qi, 2 * t, sa_ref, 2 * t + 1, sb_ref)
            overlapped(qi, 2 * t + 1, sb_ref, 2 * t + 2, sa_ref)
            return carry2

        n_pairs = lax.shift_right_logical(jnp.maximum(qi - 1, 0), 1)
        lax.fori_loop(0, n_pairs, pair, 0)
        last = 2 * n_pairs

        @pl.when(qi - last == 2)
        def _():
            overlapped(qi, last, sa_ref, last + 1, sb_ref)
            overlapped(qi, last + 1, sb_ref, qi, sc_ref, band=True)

        @pl.when(qi - last == 1)
        def _():
            overlapped(qi, last, sa_ref, qi, sc_ref, band=True)

        nxt = jnp.minimum(qi + 1, n_tiles - 1)
        for j in range(NSUB):
            band_row(qi, j, sc_ref)
            fill_cols(sa_ref, nxt, 0, j)
        finish_tile(qi)
        start_tile()
        return carry

    lax.fori_loop(0, n_tiles, tile_body, 0)


def _local_iotas():
    lane = lax.broadcasted_iota(jnp.int32, (TK, TQS), 1)
    sub = lax.broadcasted_iota(jnp.int32, (TK, TQS), 0)
    return lane, sub


def _diff_attn_kernel(slope_ref, q1_ref, q2_ref, k_ref, vt_ref, lq1_ref, lk1_ref, lq2_ref, lk2_ref,
                      gsub_ref, o_ref, m_ref, l_ref, acc_ref, *s_refs):
    h = pl.program_id(0)
    neg_slope = -slope_ref[h] * LOG2E
    q_refs = (q1_ref, q2_ref)

    slope2 = -neg_slope
    krow = lax.broadcasted_iota(jnp.int32, (TKM, HEAD_DIM), 0)
    klane = lax.broadcasted_iota(jnp.int32, (TKM, HEAD_DIM), 1)
    kx = jnp.where(klane < 3, (krow // HEAD_DIM).astype(_f32),
                   jnp.where(klane < 6, (krow % HEAD_DIM).astype(_f32), 0.0)).astype(_bf16)
    qrow = lax.broadcasted_iota(jnp.int32, (HEAD_DIM, TQS), 0)
    ones = jnp.ones((HEAD_DIM, TQS), _f32)
    qx = _select_rows_or_lanes(qrow, _split3(ones * (slope2 * HEAD_DIM)) + _split3(ones * slope2)).astype(_bf16)
    t_loc = lax.broadcasted_iota(jnp.int32, (1, TQS), 1).astype(_f32)
    lane, sub = _local_iotas()
    fix_diag = jnp.where((sub // 64) <= (lane // 64),
                         (sub.astype(_f32) + jnp.abs((lane - sub).astype(_f32))) * neg_slope, NEG_BIG)
    lam = (jnp.exp(jnp.sum(lq1_ref[...] * lk1_ref[...], axis=-1, keepdims=True))
           - jnp.exp(jnp.sum(lq2_ref[...] * lk2_ref[...], axis=-1, keepdims=True))
           + LAMBDA_INIT)

    def scores_cols(tile, block, c, nk):
        start = pl.multiple_of(block * TKM, TKM)
        k_aug = jnp.concatenate([k_ref[0, pl.ds(start, nk), :], kx[:nk]], axis=1)
        cs = slice(c * TQS, (c + 1) * TQS)
        return tuple(_dot(k_aug, jnp.concatenate([q_ref[0, tile, :, cs], qx], axis=0)) for q_ref in q_refs)

    def query_shift(tile, block, c):
        return (jnp.asarray((tile - block) * TQ + c * TQS, _f32) + t_loc) * neg_slope

    def main_chain(tile, block, s_refs, c, idx):
        vt = vt_ref[0, block]
        mloc, l, o = _block_partial(s_refs[idx][:, c * TQS:(c + 1) * TQS], vt)
        _merge_partials(m_ref, l_ref, acc_ref, idx, c, [(mloc + query_shift(tile, block, c), l, o)])

    def band_row(tile, j, s_refs):
        vt = vt_ref[0, tile, :, j * TK:(j + 1) * TK]
        for idx in range(2):
            for c in range(j, NSUB):
                z = s_refs[idx][j * TK:(j + 1) * TK, c * TQS:(c + 1) * TQS]
                if c == j:
                    mloc, l, o = _block_partial(z + fix_diag, vt)
                    part = (mloc + (j * TK) * neg_slope, l, o)
                else:
                    mloc, l, o = _block_partial(z, vt)
                    part = (mloc + query_shift(tile, tile, c), l, o)
                _merge_partials(m_ref, l_ref, acc_ref, idx, c, [part])

    def finish_tile(tile):
        o = acc_ref[0] / l_ref[0] - lam * (acc_ref[1] / l_ref[1])
        r = lax.rsqrt(jnp.mean(o * o, axis=0, keepdims=True) + EPS)
        y = (o * r) * gsub_ref[...] * (1.0 - LAMBDA_INIT)
        o_ref[pl.ds(pl.multiple_of(tile * TQ, TQ), TQ), :] = y.T.astype(o_ref.dtype)

    n = len(s_refs) // 3
    _head_sweep(scores_cols, main_chain, band_row, lambda: _init_state(m_ref, l_ref, acc_ref), finish_tile,
                s_refs[:n], s_refs[n:2 * n], s_refs[2 * n:])


def _head_specs(n_q):
    nq = SEQ // TQ
    q_spec = pl.BlockSpec((1, nq, HEAD_DIM, TQ), lambda h: (h, 0, 0, 0))
    return [q_spec] * n_q + [pl.BlockSpec((1, SEQ, HEAD_DIM), lambda h: (h, 0, 0)),
                             pl.BlockSpec((1, SEQ // TKM, V_ROWS, TKM), lambda h: (h, 0, 0, 0))]


def _head_scratch(n_maps):
    return [pltpu.VMEM((n_maps, 1, TQ), _f32),
            pltpu.VMEM((n_maps, 1, TQ), _f32),
            pltpu.VMEM((n_maps, HEAD_DIM, TQ), _f32)] + [pltpu.VMEM((TKM, TQ + HEAD_DIM), _f32)] * (3 * n_maps)


def _diff_attention(slopes, q1t, q2t, k, vt, lq1, lk1, lq2, lk2, gsub_col):
    vec = lambda: pl.BlockSpec((1, QK_DIFF), lambda h: (0, 0))
    q_specs = _head_specs(2)
    return pl.pallas_call(
        _diff_attn_kernel,
        grid=(N_HEADS,),
        in_specs=[pl.BlockSpec(memory_space=pltpu.SMEM)] + q_specs + [
            vec(), vec(), vec(), vec(), pl.BlockSpec((HEAD_DIM, 1), lambda h: (0, 0))],
        out_specs=pl.BlockSpec((SEQ, HEAD_DIM), lambda h: (0, h)),
        out_shape=jax.ShapeDtypeStruct((SEQ, WIDTH), _bf16),
        scratch_shapes=_head_scratch(2),
        compiler_params=_cparams(("parallel",)),
        name="diff_attention",
    )(slopes, q1t, q2t, k, vt, lq1, lk1, lq2, lk2, gsub_col)


def _fox_attn_kernel(q_ref, k_ref, vt_ref, ft_ref, kx_ref, o_ref, m_ref, l_ref, acc_ref, *s_refs):
    qrow = lax.broadcasted_iota(jnp.int32, (HEAD_DIM, TQS), 0)
    ones_rows = jnp.where(qrow < 3, 1.0, 0.0).astype(_bf16)
    lane, sub = _local_iotas()
    causal = sub <= lane

    def scores_cols(tile, block, c, nk):
        start = pl.multiple_of(block * TKM, TKM)
        k_aug = jnp.concatenate([k_ref[0, pl.ds(start, nk), :], kx_ref[0, pl.ds(start, nk), :]], axis=1)
        q_aug = jnp.concatenate([q_ref[0, tile, :, c * TQS:(c + 1) * TQS], ones_rows], axis=0)
        return (_dot(k_aug, q_aug),)

    def main_chain(tile, block, s_ref, c, idx):
        cs = slice(c * TQS, (c + 1) * TQS)
        mloc, l, o = _block_partial(s_ref[0][:, cs], vt_ref[0, block])
        _merge_partials(m_ref, l_ref, acc_ref, 0, c, [(mloc + ft_ref[0, tile, :, cs], l, o)])

    def band_row(tile, j, s_ref):
        vt = vt_ref[0, tile, :, j * TK:(j + 1) * TK]
        for c in range(j, NSUB):
            cs = slice(c * TQS, (c + 1) * TQS)
            z = s_ref[0][j * TK:(j + 1) * TK, cs]
            if c == j:
                z = jnp.where(causal, z, NEG_BIG)
            mloc, l, o = _block_partial(z, vt)
            _merge_partials(m_ref, l_ref, acc_ref, 0, c, [(mloc + ft_ref[0, tile, :, cs], l, o)])

    def finish_tile(tile):
        o = acc_ref[0] / l_ref[0]
        o_ref[pl.ds(pl.multiple_of(tile * TQ, TQ), TQ), :] = o.T.astype(o_ref.dtype)

    _head_sweep(scores_cols, main_chain, band_row, lambda: _init_state(m_ref, l_ref, acc_ref), finish_tile,
                s_refs[:1], s_refs[1:2], s_refs[2:])


def _fox_attention(qt, k, vt, ft4, kx):
    nq = SEQ // TQ
    return pl.pallas_call(
        _fox_attn_kernel,
        grid=(N_HEADS,),
        in_specs=_head_specs(1) + [pl.BlockSpec((1, nq, 1, TQ), lambda h: (h, 0, 0, 0)),
                                   pl.BlockSpec((1, SEQ, HEAD_DIM), lambda h: (h, 0, 0))],
        out_specs=pl.BlockSpec((SEQ, HEAD_DIM), lambda h: (0, h)),
        out_shape=jax.ShapeDtypeStruct((SEQ, WIDTH), _bf16),
        scratch_shapes=_head_scratch(1),
        compiler_params=_cparams(("parallel",)),
        name="fox_attention",
    )(qt, k, vt, ft4, kx)


def _merge_kernel(oa_ref, ob_ref, g_ref, x_ref, wbd_ref, wbf_ref, wout_ref, gm_ref, x1_ref, h2_ref):
    for sub in range(oa_ref.shape[0] // PROJ_SUB):
        rows = slice(sub * PROJ_SUB, (sub + 1) * PROJ_SUB)
        a = _dot(oa_ref[rows, :], wbd_ref[...])
        b = _dot(ob_ref[rows, :], wbf_ref[...])
        g = g_ref[rows, :].astype(_f32)
        merged = g[:, :D_MODEL] * a + g[:, D_MODEL:] * b
        x1 = x_ref[rows, :] + _dot(merged.astype(_bf16), wout_ref[...])
        x1_ref[rows, :] = x1
        r = lax.rsqrt(jnp.mean(x1 * x1, axis=-1, keepdims=True) + EPS)
        h2_ref[rows, :] = ((x1 * r) * gm_ref[...]).astype(h2_ref.dtype)


def _merge(oa, ob, gates, x, wbd, wbf, wout, gm, tm=512):
    s, d = x.shape
    const = lambda shape: pl.BlockSpec(shape, lambda i: (0, 0), pipeline_mode=pl.Buffered(1))
    return pl.pallas_call(
        _merge_kernel,
        grid=(s // tm,),
        in_specs=[pl.BlockSpec((tm, WIDTH), lambda i: (i, 0)),
                  pl.BlockSpec((tm, WIDTH), lambda i: (i, 0)),
                  pl.BlockSpec((tm, 2 * d), lambda i: (i, 0)),
                  pl.BlockSpec((tm, d), lambda i: (i, 0)),
                  const((WIDTH, d)), const((WIDTH, d)), const((d, d)),
                  pl.BlockSpec((1, d), lambda i: (0, 0))],
        out_specs=[pl.BlockSpec((tm, d), lambda i: (i, 0)),
                   pl.BlockSpec((tm, d), lambda i: (i, 0))],
        out_shape=[jax.ShapeDtypeStruct((s, d), _f32),
                   jax.ShapeDtypeStruct((s, d), _bf16)],
        compiler_params=_cparams(("parallel",)),
        name="merge_out_proj",
    )(oa, ob, gates, x, wbd, wbf, wout, gm.reshape(1, d))


def _mlp_kernel(h_ref, wu_ref, wd_ref, x_ref, o_ref):
    k = pl.program_id(1)

    @pl.when(k == 0)
    def _():
        o_ref[...] = x_ref[...]

    for sub in range(h_ref.shape[0] // PROJ_SUB):
        rows = slice(sub * PROJ_SUB, (sub + 1) * PROJ_SUB)
        u = jnp.maximum(_dot(h_ref[rows, :], wu_ref[...]), 0.0)
        o_ref[rows, :] += _dot((u * u).astype(_bf16), wd_ref[...])


def _mlp(h2, wu, wd, x1, tm=512, tf=2048):
    s, d = x1.shape
    f = wu.shape[1]
    return pl.pallas_call(
        _mlp_kernel,
        grid=(s // tm, f // tf),
        in_specs=[pl.BlockSpec((tm, d), lambda i, k: (i, 0)),
                  pl.BlockSpec((d, tf), lambda i, k: (0, k)),
                  pl.BlockSpec((tf, d), lambda i, k: (k, 0)),
                  pl.BlockSpec((tm, d), lambda i, k: (i, 0))],
        out_specs=pl.BlockSpec((tm, d), lambda i, k: (i, 0)),
        out_shape=jax.ShapeDtypeStruct((s, d), _f32),
        compiler_params=_cparams(("parallel", "arbitrary")),
        name="mlp_relu2",
    )(h2, wu, wd, x1)


def kernel(x, norm_mix, w_in, b_forget, qnorm_diff, knorm_diff, lambda_q1, lambda_k1, lambda_q2, lambda_k2,
           subln_diff, qnorm_fox, knorm_fox, w_branch_diff, w_branch_fox, w_gate, b_gate, w_out, norm_mlp,
           w_mlp_up, w_mlp_down):
    assert x.shape == (1, SEQ, D_MODEL)
    x2 = x[0]
    w_in_t = jnp.swapaxes(w_in, 1, 2)

    bf_col = jnp.zeros((F_ROWS, 1), _f32).at[:N_HEADS, 0].set(b_forget[0])
    h, ft, kx_fox = _norm_and_forget(x2, norm_mix[0], w_in_t, bf_col)
    ft4 = ft.reshape(F_ROWS, SEQ // TQ, 1, TQ)

    g_qd = jnp.tile(qnorm_diff[0], 2).reshape(1, HEAD_DIM)
    g_kd = jnp.tile(knorm_diff[0], 2).reshape(1, HEAD_DIM)
    tm = PROJ_TM
    t_shape = jax.ShapeDtypeStruct((N_HEADS, SEQ // TQ, HEAD_DIM, TQ), _bf16)
    k_shape = jax.ShapeDtypeStruct((N_HEADS, SEQ, HEAD_DIM), _bf16)
    vt_shape = jax.ShapeDtypeStruct((N_HEADS, SEQ // TKM, V_ROWS, TKM), _bf16)
    vt_spec = pl.BlockSpec((N_HEADS, tm // TKM, V_ROWS, TKM), lambda i: (0, i, 0, 0))

    q1t, q2t = _proj_call(_proj_qdiff_kernel, h, w_in_t, 0, [g_qd], [t_shape, t_shape],
                          [_t_spec(TQ), _t_spec(TQ)], "proj_q_diff", tm=TQ)
    ka = _proj_call(functools.partial(_proj_k_kernel, groups=2), h, w_in_t, 1, [g_kd], k_shape,
                    _k_spec(tm), "proj_k_diff")
    vat = _proj_call(_proj_vt_kernel, h, w_in_t, 2, [], vt_shape, vt_spec, "proj_v_diff")
    qbt = _proj_call(_proj_qfox_kernel, h, w_in_t, 3, [qnorm_fox[0].reshape(1, HEAD_DIM)], t_shape,
                     _t_spec(tm), "proj_q_fox")
    kb = _proj_call(functools.partial(_proj_k_kernel, groups=1), h, w_in_t, 4,
                    [knorm_fox[0].reshape(1, HEAD_DIM)], k_shape, _k_spec(tm), "proj_k_fox")
    vbt = _proj_call(_proj_vt_kernel, h, w_in_t, 5, [], vt_shape, vt_spec, "proj_v_fox")

    gates, (wbd, wbf, wout, wup, wdown) = _gates(
        h, w_gate, b_gate[0], [w_branch_diff, w_branch_fox, w_out, w_mlp_up, w_mlp_down])

    slopes = 2.0 ** (-8.0 * jnp.arange(1, N_HEADS + 1, dtype=_f32) / N_HEADS)
    row = lambda v: v[0].reshape(1, QK_DIFF)
    oa = _diff_attention(slopes, q1t, q2t, ka, vat, row(lambda_q1), row(lambda_k1), row(lambda_q2),
                         row(lambda_k2), subln_diff[0].reshape(HEAD_DIM, 1))
    ob = _fox_attention(qbt, kb, vbt, ft4, kx_fox)

    x1, h2 = _merge(oa, ob, gates, x2, wbd, wbf, wout, norm_mlp[0])
    out = _mlp(h2, wup, wdown, x1)
    return out[None]
```

```python
import functools

import jax
import jax.numpy as jnp
from jax import lax
from jax.experimental import pallas as pl
from jax.experimental.pallas import tpu as pltpu

D_MODEL = 2048
SEQ = 8192
HEAD_DIM = 128
N_HEADS = 8
QK_DIFF = 64
WIDTH = N_HEADS * HEAD_DIM
D_FF = 4 * D_MODEL
EPS = 1e-6
LAMBDA_INIT = 0.8 - 0.6 * 1.0
NEG_BIG = -1e30

TQ = 1024
TQS = 256
TK = 256
TKM = TQ
NSUB = TQ // TQS
LOG2E = 1.4426950408889634
V_ROWS = HEAD_DIM + 16
VMEM_LIMIT = 60 * 1024 * 1024

_f32 = jnp.float32
_bf16 = jnp.bfloat16


def _cparams(sem):
    return pltpu.CompilerParams(dimension_semantics=sem, vmem_limit_bytes=VMEM_LIMIT)


def _dot(a, b):
    return jnp.dot(a, b, preferred_element_type=_f32)


def _cast_slab_specs(weights, nsteps, step_of):
    in_specs, out_specs, out_shapes = [], [], []
    for w in weights:
        _, rows, cols = w.shape
        slab = rows // nsteps
        assert slab * nsteps == rows and slab % 16 == 0
        in_specs.append(pl.BlockSpec((None, slab, cols), lambda *ids: (0, step_of(*ids), 0)))
        out_specs.append(pl.BlockSpec((slab, cols), lambda *ids: (step_of(*ids), 0)))
        out_shapes.append(jax.ShapeDtypeStruct((rows, cols), _bf16))
    return in_specs, out_specs, out_shapes


def _cast_slabs(src_refs, dst_refs):
    for src, dst in zip(src_refs, dst_refs):
        dst[...] = src[...].astype(dst.dtype)


def _head_rmsnorm(y, gain_row, groups):
    y2 = y * y
    if groups == 1:
        r = lax.rsqrt(jnp.mean(y2, axis=-1, keepdims=True) + EPS)
    else:
        lane = lax.broadcasted_iota(jnp.int32, y.shape, 1)
        lo = lane < QK_DIFF
        s_lo = jnp.sum(jnp.where(lo, y2, 0.0), axis=-1, keepdims=True)
        s_hi = jnp.sum(jnp.where(lo, 0.0, y2), axis=-1, keepdims=True)
        r = jnp.where(lo, lax.rsqrt(s_lo * (1.0 / QK_DIFF) + EPS),
                      lax.rsqrt(s_hi * (1.0 / QK_DIFF) + EPS))
    return (y * r) * gain_row


PROJ_SUB = 256


def _proj_subtiles(a_ref, w_ref, wb_ref, first_step, epilogue, w_transposed=False, side_work=None):
    @pl.when(first_step)
    def _():
        w = w_ref[...].T if w_transposed else w_ref[...]
        wb_ref[...] = w.astype(wb_ref.dtype)

    if side_work is not None:
        side_work()
    for sub in range(a_ref.shape[0] // PROJ_SUB):
        rows = slice(sub * PROJ_SUB, (sub + 1) * PROJ_SUB)
        epilogue(rows, _dot(a_ref[rows, :], wb_ref[...]))


def _heads(acc):
    return [acc[:, hh * HEAD_DIM:(hh + 1) * HEAD_DIM] for hh in range(N_HEADS)]


def _tile_cols(rows):
    blk, col = divmod(rows.start, TQ)
    return blk, slice(col, col + PROJ_SUB)


def _proj_qdiff_kernel(a_ref, w_ref, g_ref, q1_ref, q2_ref, wb_ref):
    def epilogue(rows, acc):
        lo = lax.broadcasted_iota(jnp.int32, (PROJ_SUB, HEAD_DIM), 1) < QK_DIFF
        blk, cols = _tile_cols(rows)
        for hh, y in enumerate(_heads(acc)):
            yn = _head_rmsnorm(y, g_ref[...], 2) * (QK_DIFF ** -0.5 * LOG2E)
            q1_ref[hh, blk, :, cols] = jnp.where(lo, yn, 0.0).T.astype(q1_ref.dtype)
            q2_ref[hh, blk, :, cols] = jnp.where(lo, 0.0, yn).T.astype(q2_ref.dtype)

    _proj_subtiles(a_ref, w_ref, wb_ref, pl.program_id(0) == 0, epilogue, w_transposed=True)


def _proj_qfox_kernel(a_ref, w_ref, g_ref, q_ref, wb_ref):
    def epilogue(rows, acc):
        blk, cols = _tile_cols(rows)
        for hh, y in enumerate(_heads(acc)):
            yn = _head_rmsnorm(y, g_ref[...], 1) * (HEAD_DIM ** -0.5 * LOG2E)
            q_ref[hh, blk, :, cols] = yn.T.astype(q_ref.dtype)

    _proj_subtiles(a_ref, w_ref, wb_ref, pl.program_id(0) == 0, epilogue, w_transposed=True)


def _proj_k_kernel(a_ref, w_ref, g_ref, k_ref, wb_ref, *, groups):
    def epilogue(rows, acc):
        for hh, y in enumerate(_heads(acc)):
            k_ref[hh, rows, :] = _head_rmsnorm(y, g_ref[...], groups).astype(k_ref.dtype)

    _proj_subtiles(a_ref, w_ref, wb_ref, pl.program_id(0) == 0, epilogue, w_transposed=True)


def _proj_vt_kernel(a_ref, w_ref, vt_ref, wb_ref):
    row = lax.broadcasted_iota(jnp.int32, (V_ROWS - HEAD_DIM, TKM), 0)
    ones_rows = jnp.where(row == 0, 1.0, 0.0).astype(vt_ref.dtype)
    for hh in range(N_HEADS):
        for blk in range(vt_ref.shape[1]):
            vt_ref[hh, blk, HEAD_DIM:, :] = ones_rows

    def epilogue(rows, acc):
        blk, cols = _tile_cols(rows)
        for hh, y in enumerate(_heads(acc)):
            vt_ref[hh, blk, :HEAD_DIM, cols] = y.T.astype(vt_ref.dtype)

    _proj_subtiles(a_ref, w_ref, wb_ref, pl.program_id(0) == 0, epilogue, w_transposed=True)


def _proj_gates_kernel(a_ref, w_ref, b_ref, *refs, n_cast):
    cast_in, o_ref, cast_out, wb_ref = refs[:n_cast], refs[n_cast], refs[n_cast + 1:-1], refs[-1]

    def epilogue(rows, acc):
        o_ref[rows, :] = (1.0 / (1.0 + jnp.exp(-(acc + b_ref[...])))).astype(o_ref.dtype)

    _proj_subtiles(a_ref, w_ref, wb_ref, pl.program_id(1) == 0, epilogue,
                   side_work=lambda: _cast_slabs(cast_in, cast_out))


PROJ_TM = 2 * TQ


def _proj_call(kernel, h, w, col_block, extras, out_shapes, out_specs, name, tm=PROJ_TM):
    s, d = h.shape
    in_specs = [pl.BlockSpec((tm, d), lambda i: (i, 0)),
                pl.BlockSpec((None, WIDTH, d), lambda i: (0, col_block, 0), pipeline_mode=pl.Buffered(1))]
    in_specs += [pl.BlockSpec(e.shape, lambda i: (0, 0)) for e in extras]
    return pl.pallas_call(
        kernel,
        grid=(s // tm,),
        in_specs=in_specs,
        out_specs=out_specs,
        out_shape=out_shapes,
        scratch_shapes=[pltpu.VMEM((d, WIDTH), _bf16)],
        compiler_params=_cparams(("arbitrary",)),
        name=name,
    )(h, w, *extras)


def _t_spec(tm):
    assert tm % TQ == 0
    return pl.BlockSpec((N_HEADS, tm // TQ, HEAD_DIM, TQ), lambda i: (0, i, 0, 0))


def _k_spec(tm):
    return pl.BlockSpec((N_HEADS, tm, HEAD_DIM), lambda i: (0, i, 0))


def _gates(h, w, b, cast_weights, tm=1024, tn=1024):
    s, d = h.shape
    n = w.shape[2]
    ni = s // tm
    cast_in, cast_out, cast_shapes = _cast_slab_specs(cast_weights, (n // tn) * ni, lambda j, i: j * ni + i)
    outs = pl.pallas_call(
        functools.partial(_proj_gates_kernel, n_cast=len(cast_weights)),
        grid=(n // tn, ni),
        in_specs=[pl.BlockSpec((tm, d), lambda j, i: (i, 0)),
                  pl.BlockSpec((None, d, tn), lambda j, i: (0, 0, j)),
                  pl.BlockSpec((1, tn), lambda j, i: (0, j))] + cast_in,
        out_specs=[pl.BlockSpec((tm, tn), lambda j, i: (i, j))] + cast_out,
        out_shape=[jax.ShapeDtypeStruct((s, n), _bf16)] + cast_shapes,
        scratch_shapes=[pltpu.VMEM((d, tn), _bf16)],
        compiler_params=_cparams(("arbitrary", "arbitrary")),
        name="proj_gates",
    )(h, w, b.reshape(1, n), *cast_weights)
    return outs[0], outs[1:]


F_ROWS = 16
F_CHUNK = 256


def _split3(x):
    hi = x.astype(_bf16)
    r1 = x - hi.astype(_f32)
    mid = r1.astype(_bf16)
    lo = (r1 - mid.astype(_f32)).astype(_bf16)
    return hi, mid, lo


def _select_rows_or_lanes(index, values):
    out = jnp.zeros(index.shape, _f32)
    for i in reversed(range(len(values))):
        out = jnp.where(index == i, values[i].astype(_f32), out)
    return out


def _forget_kernel(wt_ref, x_ref, g_ref, b_ref, h_ref, ft_ref, kx_ref, carry_ref):
    i = pl.program_id(0)

    @pl.when(i == 0)
    def _():
        carry_ref[...] = jnp.zeros_like(carry_ref)

    x = x_ref[...]
    r = lax.rsqrt(jnp.mean(x * x, axis=-1, keepdims=True) + EPS)
    h = ((x * r) * g_ref[...]).astype(h_ref.dtype)
    h_ref[...] = h
    wt8 = wt_ref[...]
    wt = jnp.concatenate([wt8, jnp.zeros_like(wt8)], axis=0).astype(_bf16)
    z = lax.dot_general(wt, h, (((1,), (1,)), ((), ())),
                        preferred_element_type=_f32) + b_ref[...]
    logf = (jnp.minimum(z, 0.0) - jnp.log(1.0 + jnp.exp(-jnp.abs(z)))) * LOG2E
    tm = logf.shape[1]
    r = lax.broadcasted_iota(jnp.int32, (F_CHUNK, F_CHUNK), 0)
    c = lax.broadcasted_iota(jnp.int32, (F_CHUNK, F_CHUNK), 1)
    upper = jnp.where(r <= c, 1.0, 0.0).astype(_bf16)
    carry = carry_ref[...]
    lane = lax.broadcasted_iota(jnp.int32, (HEAD_DIM, HEAD_DIM), 1)
    for ch in range(tm // F_CHUNK):
        x = logf[:, ch * F_CHUNK:(ch + 1) * F_CHUNK]
        hi, mid, lo = _split3(x)
        pre = _dot(hi, upper) + _dot(mid, upper) + _dot(lo, upper) + carry
        ft_ref[:, ch * F_CHUNK:(ch + 1) * F_CHUNK] = pre
        carry = pre[:, F_CHUNK - 1:F_CHUNK]
        for hh in range(N_HEADS):
            for sub in range(F_CHUNK // HEAD_DIM):
                row = pre[hh:hh + 1, sub * HEAD_DIM:(sub + 1) * HEAD_DIM]
                neg = -jnp.broadcast_to(row, (HEAD_DIM, HEAD_DIM)).T
                base = ch * F_CHUNK + sub * HEAD_DIM
                kx_ref[hh, base:base + HEAD_DIM, :] = _select_rows_or_lanes(
                    lane, _split3(neg)).astype(kx_ref.dtype)
    carry_ref[...] = carry


def _norm_and_forget(x, g, w_in_t, b_col, tm=1024):
    s, d = x.shape
    f_block = 6 * WIDTH // N_HEADS
    return pl.pallas_call(
        _forget_kernel,
        grid=(s // tm,),
        in_specs=[pl.BlockSpec((None, N_HEADS, d), lambda i: (0, f_block, 0)),
                  pl.BlockSpec((tm, d), lambda i: (i, 0)),
                  pl.BlockSpec((1, d), lambda i: (0, 0)),
                  pl.BlockSpec((F_ROWS, 1), lambda i: (0, 0))],
        out_specs=[pl.BlockSpec((tm, d), lambda i: (i, 0)),
                   pl.BlockSpec((F_ROWS, tm), lambda i: (0, i)),
                   pl.BlockSpec((N_HEADS, tm, HEAD_DIM), lambda i: (0, i, 0))],
        out_shape=[jax.ShapeDtypeStruct((s, d), _bf16),
                   jax.ShapeDtypeStruct((F_ROWS, s), _f32),
                   jax.ShapeDtypeStruct((N_HEADS, s, HEAD_DIM), _bf16)],
        scratch_shapes=[pltpu.VMEM((F_ROWS, 1), _f32)],
        compiler_params=_cparams(("arbitrary",)),
        name="norm_forget_gate",
    )(w_in_t, x, g.reshape(1, d), b_col)


def _block_partial(z, vt):
    mloc = jnp.max(z, axis=0, keepdims=True)
    o = _dot(vt, jnp.exp2(z - mloc).astype(_bf16))
    return mloc, o[HEAD_DIM:HEAD_DIM + 1], o[:HEAD_DIM]


def _merge_partials(m_ref, l_ref, acc_ref, idx, c, parts):
    cs = slice(c * TQS, (c + 1) * TQS)
    m_old = m_ref[idx, :, cs]
    m_new = m_old
    for mt, _, _ in parts:
        m_new = jnp.maximum(m_new, mt)
    a = jnp.exp2(m_old - m_new)
    l_new = a * l_ref[idx, :, cs]
    acc_new = a * acc_ref[idx, :, cs]
    for mt, l, o in parts:
        b = jnp.exp2(mt - m_new)
        l_new = l_new + b * l
        acc_new = acc_new + b * o
    m_ref[idx, :, cs] = m_new
    l_ref[idx, :, cs] = l_new
    acc_ref[idx, :, cs] = acc_new


def _init_state(m_ref, l_ref, acc_ref):
    m_ref[...] = jnp.full_like(m_ref, NEG_BIG)
    l_ref[...] = jnp.zeros_like(l_ref)
    acc_ref[...] = jnp.zeros_like(acc_ref)


def _head_sweep(scores_cols, main_chain, band_row, start_tile, finish_tile, sa_ref, sb_ref, sc_ref,
                unroll_pairs=False):
    n_maps = len(sa_ref)
    n_tiles = SEQ // TQ

    def n_keys(c, band):
        return (c + 1) * TK if band else TKM

    def fill_cols(refs, tile, block, c, band=False):
        for ref, val in zip(refs, scores_cols(tile, block, c, n_keys(c, band))):
            ref[:n_keys(c, band), c * TQS:(c + 1) * TQS] = val

    def overlapped(tile, cur_block, cur_refs, nxt_block, nxt_refs, band=False):
        for c in range(NSUB):
            for idx in range(n_maps):
                nxt_refs[idx][:n_keys(c, band), c * TQS:(c + 1) * TQS] = scores_cols(
                    tile, nxt_block, c, n_keys(c, band))[idx]
                main_chain(tile, cur_block, cur_refs, c, idx)

    for c in range(NSUB):
        fill_cols(sc_ref, 0, 0, c, band=True)
    start_tile()

    def tile_body(qi, carry):
        def pair(t, carry2):
            overlapped(qi, 2 * t, sa_ref, 2 * t + 1, sb_ref)
            overlapped(qi, 2 * t + 1, sb_ref, 2 * t + 2, sa_ref)
            return carry2

        n_pairs = lax.shift_right_logical(jnp.maximum(qi - 1, 0), 1)

        if unroll_pairs:
            def quad(u, carry2):
                pair(2 * u, carry2)
                pair(2 * u + 1, carry2)
                return carry2

            lax.fori_loop(0, lax.shift_right_logical(n_pairs, 1), quad, 0)

            @pl.when(lax.rem(n_pairs, 2) == 1)
            def _():
                pair(n_pairs - 1, 0)
        else:
            lax.fori_loop(0, n_pairs, pair, 0)

        last = 2 * n_pairs

        @pl.when(qi - last == 2)
        def _():
            overlapped(qi, last, sa_ref, last + 1, sb_ref)
            overlapped(qi, last + 1, sb_ref, qi, sc_ref, band=True)

        @pl.when(qi - last == 1)
        def _():
            overlapped(qi, last, sa_ref, qi, sc_ref, band=True)

        nxt = jnp.minimum(qi + 1, n_tiles - 1)
        for j in range(NSUB):
            band_row(qi, j, sc_ref)
            fill_cols(sa_ref, nxt, 0, j)
        finish_tile(qi)
        start_tile()
        return carry

    lax.fori_loop(0, n_tiles, tile_body, 0)


def _local_iotas():
    lane = lax.broadcasted_iota(jnp.int32, (TK, TQS), 1)
    sub = lax.broadcasted_iota(jnp.int32, (TK, TQS), 0)
    return lane, sub


def _diff_attn_kernel(slope_ref, q1_ref, q2_ref, k_ref, vt_ref, lq1_ref, lk1_ref, lq2_ref, lk2_ref,
                      gsub_ref, o_ref, m_ref, l_ref, acc_ref, *s_refs):
    h = pl.program_id(0)
    neg_slope = -slope_ref[h] * LOG2E
    q_refs = (q1_ref, q2_ref)

    slope2 = -neg_slope
    krow = lax.broadcasted_iota(jnp.int32, (TKM, HEAD_DIM), 0)
    klane = lax.broadcasted_iota(jnp.int32, (TKM, HEAD_DIM), 1)
    kx = jnp.where(klane < 3, (krow // HEAD_DIM).astype(_f32),
                   jnp.where(klane < 6, (krow % HEAD_DIM).astype(_f32), 0.0)).astype(_bf16)
    qrow = lax.broadcasted_iota(jnp.int32, (HEAD_DIM, TQS), 0)
    ones = jnp.ones((HEAD_DIM, TQS), _f32)
    qx = _select_rows_or_lanes(qrow, _split3(ones * (slope2 * HEAD_DIM)) + _split3(ones * slope2)).astype(_bf16)
    t_loc = lax.broadcasted_iota(jnp.int32, (1, TQS), 1).astype(_f32)
    lane, sub = _local_iotas()
    fix_diag = jnp.where((sub // 64) <= (lane // 64),
                         (sub.astype(_f32) + jnp.abs((lane - sub).astype(_f32))) * neg_slope, NEG_BIG)
    lam = (jnp.exp(jnp.sum(lq1_ref[...] * lk1_ref[...], axis=-1, keepdims=True))
           - jnp.exp(jnp.sum(lq2_ref[...] * lk2_ref[...], axis=-1, keepdims=True))
           + LAMBDA_INIT)

    def scores_cols(tile, block, c, nk):
        start = pl.multiple_of(block * TKM, TKM)
        k_aug = jnp.concatenate([k_ref[0, pl.ds(start, nk), :], kx[:nk]], axis=1)
        cs = slice(c * TQS, (c + 1) * TQS)
        return tuple(_dot(k_aug, jnp.concatenate([q_ref[0, tile, :, cs], qx], axis=0)) for q_ref in q_refs)

    def query_shift(tile, block, c):
        return (jnp.asarray((tile - block) * TQ + c * TQS, _f32) + t_loc) * neg_slope

    def main_chain(tile, block, s_refs, c, idx):
        vt = vt_ref[0, block]
        mloc, l, o = _block_partial(s_refs[idx][:, c * TQS:(c + 1) * TQS], vt)
        _merge_partials(m_ref, l_ref, acc_ref, idx, c, [(mloc + query_shift(tile, block, c), l, o)])

    def band_row(tile, j, s_refs):
        vt = vt_ref[0, tile, :, j * TK:(j + 1) * TK]
        for idx in range(2):
            for c in range(j, NSUB):
                z = s_refs[idx][j * TK:(j + 1) * TK, c * TQS:(c + 1) * TQS]
                if c == j:
                    mloc, l, o = _block_partial(z + fix_diag, vt)
                    part = (mloc + (j * TK) * neg_slope, l, o)
                else:
                    mloc, l, o = _block_partial(z, vt)
                    part = (mloc + query_shift(tile, tile, c), l, o)
                _merge_partials(m_ref, l_ref, acc_ref, idx, c, [part])

    def finish_tile(tile):
        o = acc_ref[0] / l_ref[0] - lam * (acc_ref[1] / l_ref[1])
        r = lax.rsqrt(jnp.mean(o * o, axis=0, keepdims=True) + EPS)
        y = (o * r) * gsub_ref[...] * (1.0 - LAMBDA_INIT)
        o_ref[pl.ds(pl.multiple_of(tile * TQ, TQ), TQ), :] = y.T.astype(o_ref.dtype)

    n = len(s_refs) // 3
    _head_sweep(scores_cols, main_chain, band_row, lambda: _init_state(m_ref, l_ref, acc_ref), finish_tile,
                s_refs[:n], s_refs[n:2 * n], s_refs[2 * n:], unroll_pairs=True)


def _head_specs(n_q):
    nq = SEQ // TQ
    q_spec = pl.BlockSpec((1, nq, HEAD_DIM, TQ), lambda h: (h, 0, 0, 0))
    return [q_spec] * n_q + [pl.BlockSpec((1, SEQ, HEAD_DIM), lambda h: (h, 0, 0)),
                             pl.BlockSpec((1, SEQ // TKM, V_ROWS, TKM), lambda h: (h, 0, 0, 0))]


def _head_scratch(n_maps):
    return [pltpu.VMEM((n_maps, 1, TQ), _f32),
            pltpu.VMEM((n_maps, 1, TQ), _f32),
            pltpu.VMEM((n_maps, HEAD_DIM, TQ), _f32)] + [pltpu.VMEM((TKM, TQ + HEAD_DIM), _f32)] * (3 * n_maps)


def _diff_attention(slopes, q1t, q2t, k, vt, lq1, lk1, lq2, lk2, gsub_col):
    vec = lambda: pl.BlockSpec((1, QK_DIFF), lambda h: (0, 0))
    q_specs = _head_specs(2)
    return pl.pallas_call(
        _diff_attn_kernel,
        grid=(N_HEADS,),
        in_specs=[pl.BlockSpec(memory_space=pltpu.SMEM)] + q_specs + [
            vec(), vec(), vec(), vec(), pl.BlockSpec((HEAD_DIM, 1), lambda h: (0, 0))],
        out_specs=pl.BlockSpec((SEQ, HEAD_DIM), lambda h: (0, h)),
        out_shape=jax.ShapeDtypeStruct((SEQ, WIDTH), _bf16),
        scratch_shapes=_head_scratch(2),
        compiler_params=_cparams(("parallel",)),
        name="diff_attention",
    )(slopes, q1t, q2t, k, vt, lq1, lk1, lq2, lk2, gsub_col)


def _fox_attn_kernel(q_ref, k_ref, vt_ref, ft_ref, kx_ref, o_ref, m_ref, l_ref, acc_ref, *s_refs):
    qrow = lax.broadcasted_iota(jnp.int32, (HEAD_DIM, TQS), 0)
    ones_rows = jnp.where(qrow < 3, 1.0, 0.0).astype(_bf16)
    lane, sub = _local_iotas()
    causal = sub <= lane

    def scores_cols(tile, block, c, nk):
        start = pl.multiple_of(block * TKM, TKM)
        k_aug = jnp.concatenate([k_ref[0, pl.ds(start, nk), :], kx_ref[0, pl.ds(start, nk), :]], axis=1)
        q_aug = jnp.concatenate([q_ref[0, tile, :, c * TQS:(c + 1) * TQS], ones_rows], axis=0)
        return (_dot(k_aug, q_aug),)

    def main_chain(tile, block, s_ref, c, idx):
        cs = slice(c * TQS, (c + 1) * TQS)
        mloc, l, o = _block_partial(s_ref[0][:, cs], vt_ref[0, block])
        _merge_partials(m_ref, l_ref, acc_ref, 0, c, [(mloc + ft_ref[0, tile, :, cs], l, o)])

    def band_row(tile, j, s_ref):
        vt = vt_ref[0, tile, :, j * TK:(j + 1) * TK]
        for c in range(j, NSUB):
            cs = slice(c * TQS, (c + 1) * TQS)
            z = s_ref[0][j * TK:(j + 1) * TK, cs]
            if c == j:
                z = jnp.where(causal, z, NEG_BIG)
            mloc, l, o = _block_partial(z, vt)
            _merge_partials(m_ref, l_ref, acc_ref, 0, c, [(mloc + ft_ref[0, tile, :, cs], l, o)])

    def finish_tile(tile):
        o = acc_ref[0] / l_ref[0]
        o_ref[pl.ds(pl.multiple_of(tile * TQ, TQ), TQ), :] = o.T.astype(o_ref.dtype)

    _head_sweep(scores_cols, main_chain, band_row, lambda: _init_state(m_ref, l_ref, acc_ref), finish_tile,
                s_refs[:1], s_refs[1:2], s_refs[2:], unroll_pairs=True)


def _fox_attention(qt, k, vt, ft4, kx):
    nq = SEQ // TQ
    return pl.pallas_call(
        _fox_attn_kernel,
        grid=(N_HEADS,),
        in_specs=_head_specs(1) + [pl.BlockSpec((1, nq, 1, TQ), lambda h: (h, 0, 0, 0)),
                                   pl.BlockSpec((1, SEQ, HEAD_DIM), lambda h: (h, 0, 0))],
        out_specs=pl.BlockSpec((SEQ, HEAD_DIM), lambda h: (0, h)),
        out_shape=jax.ShapeDtypeStruct((SEQ, WIDTH), _bf16),
        scratch_shapes=_head_scratch(1),
        compiler_params=_cparams(("parallel",)),
        name="fox_attention",
    )(qt, k, vt, ft4, kx)


def _merge_kernel(oa_ref, ob_ref, g_ref, x_ref, wbd_ref, wbf_ref, wout_ref, gm_ref, x1_ref, h2_ref):
    for sub in range(oa_ref.shape[0] // PROJ_SUB):
        rows = slice(sub * PROJ_SUB, (sub + 1) * PROJ_SUB)
        a = _dot(oa_ref[rows, :], wbd_ref[...])
        b = _dot(ob_ref[rows, :], wbf_ref[...])
        g = g_ref[rows, :].astype(_f32)
        merged = g[:, :D_MODEL] * a + g[:, D_MODEL:] * b
        x1 = x_ref[rows, :] + _dot(merged.astype(_bf16), wout_ref[...])
        x1_ref[rows, :] = x1
        r = lax.rsqrt(jnp.mean(x1 * x1, axis=-1, keepdims=True) + EPS)
        h2_ref[rows, :] = ((x1 * r) * gm_ref[...]).astype(h2_ref.dtype)


def _merge(oa, ob, gates, x, wbd, wbf, wout, gm, tm=512):
    s, d = x.shape
    const = lambda shape: pl.BlockSpec(shape, lambda i: (0, 0), pipeline_mode=pl.Buffered(1))
    return pl.pallas_call(
        _merge_kernel,
        grid=(s // tm,),
        in_specs=[pl.BlockSpec((tm, WIDTH), lambda i: (i, 0)),
                  pl.BlockSpec((tm, WIDTH), lambda i: (i, 0)),
                  pl.BlockSpec((tm, 2 * d), lambda i: (i, 0)),
                  pl.BlockSpec((tm, d), lambda i: (i, 0)),
                  const((WIDTH, d)), const((WIDTH, d)), const((d, d)),
                  pl.BlockSpec((1, d), lambda i: (0, 0))],
        out_specs=[pl.BlockSpec((tm, d), lambda i: (i, 0)),
                   pl.BlockSpec((tm, d), lambda i: (i, 0))],
        out_shape=[jax.ShapeDtypeStruct((s, d), _f32),
                   jax.ShapeDtypeStruct((s, d), _bf16)],
        compiler_params=_cparams(("parallel",)),
        name="merge_out_proj",
    )(oa, ob, gates, x, wbd, wbf, wout, gm.reshape(1, d))


def _mlp_kernel(h_ref, wu_ref, wd_ref, x_ref, o_ref):
    k = pl.program_id(1)

    @pl.when(k == 0)
    def _():
        o_ref[...] = x_ref[...]

    for sub in range(h_ref.shape[0] // PROJ_SUB):
        rows = slice(sub * PROJ_SUB, (sub + 1) * PROJ_SUB)
        u = jnp.maximum(_dot(h_ref[rows, :], wu_ref[...]), 0.0)
        o_ref[rows, :] += _dot((u * u).astype(_bf16), wd_ref[...])


def _mlp(h2, wu, wd, x1, tm=512, tf=2048):
    s, d = x1.shape
    f = wu.shape[1]
    return pl.pallas_call(
        _mlp_kernel,
        grid=(s // tm, f // tf),
        in_specs=[pl.BlockSpec((tm, d), lambda i, k: (i, 0)),
                  pl.BlockSpec((d, tf), lambda i, k: (0, k)),
                  pl.BlockSpec((tf, d), lambda i, k: (k, 0)),
                  pl.BlockSpec((tm, d), lambda i, k: (i, 0))],
        out_specs=pl.BlockSpec((tm, d), lambda i, k: (i, 0)),
        out_shape=jax.ShapeDtypeStruct((s, d), _f32),
        compiler_params=_cparams(("parallel", "arbitrary")),
        name="mlp_relu2",
    )(h2, wu, wd, x1)


def kernel(x, norm_mix, w_in, b_forget, qnorm_diff, knorm_diff, lambda_q1, lambda_k1, lambda_q2, lambda_k2,
           subln_diff, qnorm_fox, knorm_fox, w_branch_diff, w_branch_fox, w_gate, b_gate, w_out, norm_mlp,
           w_mlp_up, w_mlp_down):
    assert x.shape == (1, SEQ, D_MODEL)
    x2 = x[0]
    w_in_t = jnp.swapaxes(w_in, 1, 2)

    bf_col = jnp.zeros((F_ROWS, 1), _f32).at[:N_HEADS, 0].set(b_forget[0])
    h, ft, kx_fox = _norm_and_forget(x2, norm_mix[0], w_in_t, bf_col)
    ft4 = ft.reshape(F_ROWS, SEQ // TQ, 1, TQ)

    g_qd = jnp.tile(qnorm_diff[0], 2).reshape(1, HEAD_DIM)
    g_kd = jnp.tile(knorm_diff[0], 2).reshape(1, HEAD_DIM)
    tm = PROJ_TM
    t_shape = jax.ShapeDtypeStruct((N_HEADS, SEQ // TQ, HEAD_DIM, TQ), _bf16)
    k_shape = jax.ShapeDtypeStruct((N_HEADS, SEQ, HEAD_DIM), _bf16)
    vt_shape = jax.ShapeDtypeStruct((N_HEADS, SEQ // TKM, V_ROWS, TKM), _bf16)
    vt_spec = pl.BlockSpec((N_HEADS, tm // TKM, V_ROWS, TKM), lambda i: (0, i, 0, 0))

    q1t, q2t = _proj_call(_proj_qdiff_kernel, h, w_in_t, 0, [g_qd], [t_shape, t_shape],
                          [_t_spec(TQ), _t_spec(TQ)], "proj_q_diff", tm=TQ)
    ka = _proj_call(functools.partial(_proj_k_kernel, groups=2), h, w_in_t, 1, [g_kd], k_shape,
                    _k_spec(tm), "proj_k_diff")
    vat = _proj_call(_proj_vt_kernel, h, w_in_t, 2, [], vt_shape, vt_spec, "proj_v_diff")
    qbt = _proj_call(_proj_qfox_kernel, h, w_in_t, 3, [qnorm_fox[0].reshape(1, HEAD_DIM)], t_shape,
                     _t_spec(tm), "proj_q_fox")
    kb = _proj_call(functools.partial(_proj_k_kernel, groups=1), h, w_in_t, 4,
                    [knorm_fox[0].reshape(1, HEAD_DIM)], k_shape, _k_spec(tm), "proj_k_fox")
    vbt = _proj_call(_proj_vt_kernel, h, w_in_t, 5, [], vt_shape, vt_spec, "proj_v_fox")

    gates, (wbd, wbf, wout, wup, wdown) = _gates(
        h, w_gate, b_gate[0], [w_branch_diff, w_branch_fox, w_out, w_mlp_up, w_mlp_down])

    slopes = 2.0 ** (-8.0 * jnp.arange(1, N_HEADS + 1, dtype=_f32) / N_HEADS)
    row = lambda v: v[0].reshape(1, QK_DIFF)
    oa = _diff_attention(slopes, q1t, q2t, ka, vat, row(lambda_q1), row(lambda_k1), row(lambda_q2),
                         row(lambda_k2), subln_diff[0].reshape(HEAD_DIM, 1))
    ob = _fox_attention(qbt, kb, vbt, ft4, kx_fox)

    x1, h2 = _merge(oa, ob, gates, x2, wbd, wbf, wout, norm_mlp[0])
    out = _mlp(h2, wup, wdown, x1)
    return out[None]
```

```python
import functools

import jax
import jax.numpy as jnp
from jax import lax
from jax.experimental import pallas as pl
from jax.experimental.pallas import tpu as pltpu

D_MODEL = 2048
SEQ = 8192
HEAD_DIM = 128
N_HEADS = 8
QK_DIFF = 64
WIDTH = N_HEADS * HEAD_DIM
D_FF = 4 * D_MODEL
EPS = 1e-6
LAMBDA_INIT = 0.8 - 0.6 * 1.0
NEG_BIG = -1e30

TQ = 1024
TQS = 256
TK = 256
TKM = TQ
NSUB = TQ // TQS
LOG2E = 1.4426950408889634
V_ROWS = HEAD_DIM + 16
VMEM_LIMIT = 60 * 1024 * 1024

_f32 = jnp.float32
_bf16 = jnp.bfloat16


def _cparams(sem):
    return pltpu.CompilerParams(dimension_semantics=sem, vmem_limit_bytes=VMEM_LIMIT)


def _dot(a, b):
    return jnp.dot(a, b, preferred_element_type=_f32)


def _cast_slab_specs(weights, nsteps, step_of):
    in_specs, out_specs, out_shapes = [], [], []
    for w in weights:
        _, rows, cols = w.shape
        slab = rows // nsteps
        assert slab * nsteps == rows and slab % 16 == 0
        in_specs.append(pl.BlockSpec((None, slab, cols), lambda *ids: (0, step_of(*ids), 0)))
        out_specs.append(pl.BlockSpec((slab, cols), lambda *ids: (step_of(*ids), 0)))
        out_shapes.append(jax.ShapeDtypeStruct((rows, cols), _bf16))
    return in_specs, out_specs, out_shapes


def _cast_slabs(src_refs, dst_refs):
    for src, dst in zip(src_refs, dst_refs):
        dst[...] = src[...].astype(dst.dtype)


def _head_rmsnorm(y, gain_row, groups):
    y2 = y * y
    if groups == 1:
        r = lax.rsqrt(jnp.mean(y2, axis=-1, keepdims=True) + EPS)
    else:
        lane = lax.broadcasted_iota(jnp.int32, y.shape, 1)
        lo = lane < QK_DIFF
        s_lo = jnp.sum(jnp.where(lo, y2, 0.0), axis=-1, keepdims=True)
        s_hi = jnp.sum(jnp.where(lo, 0.0, y2), axis=-1, keepdims=True)
        r = jnp.where(lo, lax.rsqrt(s_lo * (1.0 / QK_DIFF) + EPS),
                      lax.rsqrt(s_hi * (1.0 / QK_DIFF) + EPS))
    return (y * r) * gain_row


PROJ_SUB = 256


def _proj_subtiles(a_ref, w_ref, wb_ref, first_step, epilogue, w_transposed=False, side_work=None):
    @pl.when(first_step)
    def _():
        w = w_ref[...].T if w_transposed else w_ref[...]
        wb_ref[...] = w.astype(wb_ref.dtype)

    if side_work is not None:
        side_work()
    for sub in range(a_ref.shape[0] // PROJ_SUB):
        rows = slice(sub * PROJ_SUB, (sub + 1) * PROJ_SUB)
        epilogue(rows, _dot(a_ref[rows, :], wb_ref[...]))


def _heads(acc):
    return [acc[:, hh * HEAD_DIM:(hh + 1) * HEAD_DIM] for hh in range(N_HEADS)]


def _tile_cols(rows):
    blk, col = divmod(rows.start, TQ)
    return blk, slice(col, col + PROJ_SUB)


def _proj_qdiff_kernel(a_ref, w_ref, g_ref, q1_ref, q2_ref, wb_ref):
    def epilogue(rows, acc):
        lo = lax.broadcasted_iota(jnp.int32, (PROJ_SUB, HEAD_DIM), 1) < QK_DIFF
        blk, cols = _tile_cols(rows)
        for hh, y in enumerate(_heads(acc)):
            yn = _head_rmsnorm(y, g_ref[...], 2) * (QK_DIFF ** -0.5 * LOG2E)
            q1_ref[hh, blk, :, cols] = jnp.where(lo, yn, 0.0).T.astype(q1_ref.dtype)
            q2_ref[hh, blk, :, cols] = jnp.where(lo, 0.0, yn).T.astype(q2_ref.dtype)

    _proj_subtiles(a_ref, w_ref, wb_ref, pl.program_id(0) == 0, epilogue, w_transposed=True)


def _proj_qfox_kernel(a_ref, w_ref, g_ref, q_ref, wb_ref):
    def epilogue(rows, acc):
        blk, cols = _tile_cols(rows)
        for hh, y in enumerate(_heads(acc)):
            yn = _head_rmsnorm(y, g_ref[...], 1) * (HEAD_DIM ** -0.5 * LOG2E)
            q_ref[hh, blk, :, cols] = yn.T.astype(q_ref.dtype)

    _proj_subtiles(a_ref, w_ref, wb_ref, pl.program_id(0) == 0, epilogue, w_transposed=True)


def _proj_k_kernel(a_ref, w_ref, g_ref, k_ref, wb_ref, *, groups):
    def epilogue(rows, acc):
        for hh, y in enumerate(_heads(acc)):
            k_ref[hh, rows, :] = _head_rmsnorm(y, g_ref[...], groups).astype(k_ref.dtype)

    _proj_subtiles(a_ref, w_ref, wb_ref, pl.program_id(0) == 0, epilogue, w_transposed=True)


def _proj_vt_kernel(a_ref, w_ref, vt_ref, wb_ref):
    row = lax.broadcasted_iota(jnp.int32, (V_ROWS - HEAD_DIM, TKM), 0)
    ones_rows = jnp.where(row == 0, 1.0, 0.0).astype(vt_ref.dtype)
    for hh in range(N_HEADS):
        for blk in range(vt_ref.shape[1]):
            vt_ref[hh, blk, HEAD_DIM:, :] = ones_rows

    def epilogue(rows, acc):
        blk, cols = _tile_cols(rows)
        for hh, y in enumerate(_heads(acc)):
            vt_ref[hh, blk, :HEAD_DIM, cols] = y.T.astype(vt_ref.dtype)

    _proj_subtiles(a_ref, w_ref, wb_ref, pl.program_id(0) == 0, epilogue, w_transposed=True)


def _proj_gates_kernel(a_ref, w_ref, b_ref, *refs, n_cast):
    cast_in, o_ref, cast_out, wb_ref = refs[:n_cast], refs[n_cast], refs[n_cast + 1:-1], refs[-1]

    def epilogue(rows, acc):
        o_ref[rows, :] = (1.0 / (1.0 + jnp.exp(-(acc + b_ref[...])))).astype(o_ref.dtype)

    _proj_subtiles(a_ref, w_ref, wb_ref, pl.program_id(1) == 0, epilogue,
                   side_work=lambda: _cast_slabs(cast_in, cast_out))


PROJ_TM = 2 * TQ


def _proj_call(kernel, h, w, col_block, extras, out_shapes, out_specs, name, tm=PROJ_TM):
    s, d = h.shape
    in_specs = [pl.BlockSpec((tm, d), lambda i: (i, 0)),
                pl.BlockSpec((None, WIDTH, d), lambda i: (0, col_block, 0), pipeline_mode=pl.Buffered(1))]
    in_specs += [pl.BlockSpec(e.shape, lambda i: (0, 0)) for e in extras]
    return pl.pallas_call(
        kernel,
        grid=(s // tm,),
        in_specs=in_specs,
        out_specs=out_specs,
        out_shape=out_shapes,
        scratch_shapes=[pltpu.VMEM((d, WIDTH), _bf16)],
        compiler_params=_cparams(("arbitrary",)),
        name=name,
    )(h, w, *extras)


def _t_spec(tm):
    assert tm % TQ == 0
    return pl.BlockSpec((N_HEADS, tm // TQ, HEAD_DIM, TQ), lambda i: (0, i, 0, 0))


def _k_spec(tm):
    return pl.BlockSpec((N_HEADS, tm, HEAD_DIM), lambda i: (0, i, 0))


def _gates(h, w, b, cast_weights, tm=1024, tn=1024):
    s, d = h.shape
    n = w.shape[2]
    ni = s // tm
    cast_in, cast_out, cast_shapes = _cast_slab_specs(cast_weights, (n // tn) * ni, lambda j, i: j * ni + i)
    outs = pl.pallas_call(
        functools.partial(_proj_gates_kernel, n_cast=len(cast_weights)),
        grid=(n // tn, ni),
        in_specs=[pl.BlockSpec((tm, d), lambda j, i: (i, 0)),
                  pl.BlockSpec((None, d, tn), lambda j, i: (0, 0, j)),
                  pl.BlockSpec((1, tn), lambda j, i: (0, j))] + cast_in,
        out_specs=[pl.BlockSpec((tm, tn), lambda j, i: (i, j))] + cast_out,
        out_shape=[jax.ShapeDtypeStruct((s, n), _bf16)] + cast_shapes,
        scratch_shapes=[pltpu.VMEM((d, tn), _bf16)],
        compiler_params=_cparams(("arbitrary", "arbitrary")),
        name="proj_gates",
    )(h, w, b.reshape(1, n), *cast_weights)
    return outs[0], outs[1:]


F_ROWS = 16
F_CHUNK = 256


def _split3(x):
    hi = x.astype(_bf16)
    r1 = x - hi.astype(_f32)
    mid = r1.astype(_bf16)
    lo = (r1 - mid.astype(_f32)).astype(_bf16)
    return hi, mid, lo


def _select_rows_or_lanes(index, values):
    out = jnp.zeros(index.shape, _f32)
    for i in reversed(range(len(values))):
        out = jnp.where(index == i, values[i].astype(_f32), out)
    return out


def _forget_kernel(wt_ref, x_ref, g_ref, b_ref, h_ref, ft_ref, kx_ref, carry_ref):
    i = pl.program_id(0)

    @pl.when(i == 0)
    def _():
        carry_ref[...] = jnp.zeros_like(carry_ref)

    x = x_ref[...]
    r = lax.rsqrt(jnp.mean(x * x, axis=-1, keepdims=True) + EPS)
    h = ((x * r) * g_ref[...]).astype(h_ref.dtype)
    h_ref[...] = h
    wt8 = wt_ref[...]
    wt = jnp.concatenate([wt8, jnp.zeros_like(wt8)], axis=0).astype(_bf16)
    z = lax.dot_general(wt, h, (((1,), (1,)), ((), ())),
                        preferred_element_type=_f32) + b_ref[...]
    logf = (jnp.minimum(z, 0.0) - jnp.log(1.0 + jnp.exp(-jnp.abs(z)))) * LOG2E
    tm = logf.shape[1]
    r = lax.broadcasted_iota(jnp.int32, (F_CHUNK, F_CHUNK), 0)
    c = lax.broadcasted_iota(jnp.int32, (F_CHUNK, F_CHUNK), 1)
    upper = jnp.where(r <= c, 1.0, 0.0).astype(_bf16)
    carry = carry_ref[...]
    lane = lax.broadcasted_iota(jnp.int32, (HEAD_DIM, HEAD_DIM), 1)
    for ch in range(tm // F_CHUNK):
        x = logf[:, ch * F_CHUNK:(ch + 1) * F_CHUNK]
        hi, mid, lo = _split3(x)
        pre = _dot(hi, upper) + _dot(mid, upper) + _dot(lo, upper) + carry
        ft_ref[:, ch * F_CHUNK:(ch + 1) * F_CHUNK] = pre
        carry = pre[:, F_CHUNK - 1:F_CHUNK]
        for hh in range(N_HEADS):
            for sub in range(F_CHUNK // HEAD_DIM):
                row = pre[hh:hh + 1, sub * HEAD_DIM:(sub + 1) * HEAD_DIM]
                neg = -jnp.broadcast_to(row, (HEAD_DIM, HEAD_DIM)).T
                base = ch * F_CHUNK + sub * HEAD_DIM
                kx_ref[hh, base:base + HEAD_DIM, :] = _select_rows_or_lanes(
                    lane, _split3(neg)).astype(kx_ref.dtype)
    carry_ref[...] = carry


def _norm_and_forget(x, g, w_in_t, b_col, tm=1024):
    s, d = x.shape
    f_block = 6 * WIDTH // N_HEADS
    return pl.pallas_call(
        _forget_kernel,
        grid=(s // tm,),
        in_specs=[pl.BlockSpec((None, N_HEADS, d), lambda i: (0, f_block, 0)),
                  pl.BlockSpec((tm, d), lambda i: (i, 0)),
                  pl.BlockSpec((1, d), lambda i: (0, 0)),
                  pl.BlockSpec((F_ROWS, 1), lambda i: (0, 0))],
        out_specs=[pl.BlockSpec((tm, d), lambda i: (i, 0)),
                   pl.BlockSpec((F_ROWS, tm), lambda i: (0, i)),
                   pl.BlockSpec((N_HEADS, tm, HEAD_DIM), lambda i: (0, i, 0))],
        out_shape=[jax.ShapeDtypeStruct((s, d), _bf16),
                   jax.ShapeDtypeStruct((F_ROWS, s), _f32),
                   jax.ShapeDtypeStruct((N_HEADS, s, HEAD_DIM), _bf16)],
        scratch_shapes=[pltpu.VMEM((F_ROWS, 1), _f32)],
        compiler_params=_cparams(("arbitrary",)),
        name="norm_forget_gate",
    )(w_in_t, x, g.reshape(1, d), b_col)


def _block_partial(z, vt):
    mloc = jnp.max(z, axis=0, keepdims=True)
    o = _dot(vt, jnp.exp2(z - mloc).astype(_bf16))
    return mloc, o[HEAD_DIM:HEAD_DIM + 1], o[:HEAD_DIM]


def _merge_partials(m_ref, l_ref, acc_ref, idx, c, parts):
    cs = slice(c * TQS, (c + 1) * TQS)
    m_old = m_ref[idx, :, cs]
    m_new = m_old
    for mt, _, _ in parts:
        m_new = jnp.maximum(m_new, mt)
    a = jnp.exp2(m_old - m_new)
    l_new = a * l_ref[idx, :, cs]
    acc_new = a * acc_ref[idx, :, cs]
    for mt, l, o in parts:
        b = jnp.exp2(mt - m_new)
        l_new = l_new + b * l
        acc_new = acc_new + b * o
    m_ref[idx, :, cs] = m_new
    l_ref[idx, :, cs] = l_new
    acc_ref[idx, :, cs] = acc_new


def _init_state(m_ref, l_ref, acc_ref):
    m_ref[...] = jnp.full_like(m_ref, NEG_BIG)
    l_ref[...] = jnp.zeros_like(l_ref)
    acc_ref[...] = jnp.zeros_like(acc_ref)


def _head_sweep(scores_cols, main_chain, band_row, start_tile, finish_tile, sa_ref, sb_ref, sc_ref,
                unroll_pairs=False):
    n_maps = len(sa_ref)
    n_tiles = SEQ // TQ

    def n_keys(c, band):
        return (c + 1) * TK if band else TKM

    def fill_cols(refs, tile, block, c, band=False):
        for ref, val in zip(refs, scores_cols(tile, block, c, n_keys(c, band))):
            ref[:n_keys(c, band), c * TQS:(c + 1) * TQS] = val

    def overlapped(tile, cur_block, cur_refs, nxt_block, nxt_refs, band=False):
        for c in range(NSUB):
            for idx in range(n_maps):
                nxt_refs[idx][:n_keys(c, band), c * TQS:(c + 1) * TQS] = scores_cols(
                    tile, nxt_block, c, n_keys(c, band))[idx]
                main_chain(tile, cur_block, cur_refs, c, idx)

    for c in range(NSUB):
        fill_cols(sc_ref, 0, 0, c, band=True)
    start_tile()

    def tile_body(qi, carry):
        def pair(t, carry2):
            overlapped(qi, 2 * t, sa_ref, 2 * t + 1, sb_ref)
            overlapped(qi, 2 * t + 1, sb_ref, 2 * t + 2, sa_ref)
            return carry2

        n_pairs = lax.shift_right_logical(jnp.maximum(qi - 1, 0), 1)

        if unroll_pairs:
            def quad(u, carry2):
                pair(2 * u, carry2)
                pair(2 * u + 1, carry2)
                return carry2

            lax.fori_loop(0, lax.shift_right_logical(n_pairs, 1), quad, 0)

            @pl.when(lax.rem(n_pairs, 2) == 1)
            def _():
                pair(n_pairs - 1, 0)
        else:
            lax.fori_loop(0, n_pairs, pair, 0)

        last = 2 * n_pairs

        def band_and_next_tile():
            nxt = jnp.minimum(qi + 1, n_tiles - 1)
            for j in range(NSUB):
                band_row(qi, j, sc_ref)
                fill_cols(sa_ref, nxt, 0, j)
            finish_tile(qi)
            start_tile()

        @pl.when(qi - last == 2)
        def _():
            overlapped(qi, last, sa_ref, last + 1, sb_ref)
            overlapped(qi, last + 1, sb_ref, qi, sc_ref, band=True)
            band_and_next_tile()

        @pl.when(qi - last == 1)
        def _():
            overlapped(qi, last, sa_ref, qi, sc_ref, band=True)
            band_and_next_tile()

        @pl.when(qi == 0)
        def _():
            band_and_next_tile()

        return carry

    lax.fori_loop(0, n_tiles, tile_body, 0)


def _local_iotas():
    lane = lax.broadcasted_iota(jnp.int32, (TK, TQS), 1)
    sub = lax.broadcasted_iota(jnp.int32, (TK, TQS), 0)
    return lane, sub


def _diff_attn_kernel(slope_ref, q1_ref, q2_ref, k_ref, vt_ref, lq1_ref, lk1_ref, lq2_ref, lk2_ref,
                      gsub_ref, o_ref, m_ref, l_ref, acc_ref, *s_refs):
    h = pl.program_id(0)
    neg_slope = -slope_ref[h] * LOG2E
    q_refs = (q1_ref, q2_ref)

    slope2 = -neg_slope
    krow = lax.broadcasted_iota(jnp.int32, (TKM, HEAD_DIM), 0)
    klane = lax.broadcasted_iota(jnp.int32, (TKM, HEAD_DIM), 1)
    kx = jnp.where(klane < 3, (krow // HEAD_DIM).astype(_f32),
                   jnp.where(klane < 6, (krow % HEAD_DIM).astype(_f32), 0.0)).astype(_bf16)
    qrow = lax.broadcasted_iota(jnp.int32, (HEAD_DIM, TQS), 0)
    ones = jnp.ones((HEAD_DIM, TQS), _f32)
    qx = _select_rows_or_lanes(qrow, _split3(ones * (slope2 * HEAD_DIM)) + _split3(ones * slope2)).astype(_bf16)
    t_loc = lax.broadcasted_iota(jnp.int32, (1, TQS), 1).astype(_f32)
    lane, sub = _local_iotas()
    fix_diag = jnp.where((sub // 64) <= (lane // 64),
                         (sub.astype(_f32) + jnp.abs((lane - sub).astype(_f32))) * neg_slope, NEG_BIG)
    lam = (jnp.exp(jnp.sum(lq1_ref[...] * lk1_ref[...], axis=-1, keepdims=True))
           - jnp.exp(jnp.sum(lq2_ref[...] * lk2_ref[...], axis=-1, keepdims=True))
           + LAMBDA_INIT)

    def scores_cols(tile, block, c, nk):
        start = pl.multiple_of(block * TKM, TKM)
        k_aug = jnp.concatenate([k_ref[0, pl.ds(start, nk), :], kx[:nk]], axis=1)
        cs = slice(c * TQS, (c + 1) * TQS)
        return tuple(_dot(k_aug, jnp.concatenate([q_ref[0, tile, :, cs], qx], axis=0)) for q_ref in q_refs)

    def query_shift(tile, block, c):
        return (jnp.asarray((tile - block) * TQ + c * TQS, _f32) + t_loc) * neg_slope

    def main_chain(tile, block, s_refs, c, idx):
        vt = vt_ref[0, block]
        mloc, l, o = _block_partial(s_refs[idx][:, c * TQS:(c + 1) * TQS], vt)
        _merge_partials(m_ref, l_ref, acc_ref, idx, c, [(mloc + query_shift(tile, block, c), l, o)])

    def band_row(tile, j, s_refs):
        vt = vt_ref[0, tile, :, j * TK:(j + 1) * TK]
        for idx in range(2):
            for c in range(j, NSUB):
                z = s_refs[idx][j * TK:(j + 1) * TK, c * TQS:(c + 1) * TQS]
                if c == j:
                    mloc, l, o = _block_partial(z + fix_diag, vt)
                    part = (mloc + (j * TK) * neg_slope, l, o)
                else:
                    mloc, l, o = _block_partial(z, vt)
                    part = (mloc + query_shift(tile, tile, c), l, o)
                _merge_partials(m_ref, l_ref, acc_ref, idx, c, [part])

    def finish_tile(tile):
        o = acc_ref[0] / l_ref[0] - lam * (acc_ref[1] / l_ref[1])
        r = lax.rsqrt(jnp.mean(o * o, axis=0, keepdims=True) + EPS)
        y = (o * r) * gsub_ref[...] * (1.0 - LAMBDA_INIT)
        o_ref[pl.ds(pl.multiple_of(tile * TQ, TQ), TQ), :] = y.T.astype(o_ref.dtype)

    n = len(s_refs) // 3
    _head_sweep(scores_cols, main_chain, band_row, lambda: _init_state(m_ref, l_ref, acc_ref), finish_tile,
                s_refs[:n], s_refs[n:2 * n], s_refs[2 * n:], unroll_pairs=True)


def _head_specs(n_q):
    nq = SEQ // TQ
    q_spec = pl.BlockSpec((1, nq, HEAD_DIM, TQ), lambda h: (h, 0, 0, 0))
    return [q_spec] * n_q + [pl.BlockSpec((1, SEQ, HEAD_DIM), lambda h: (h, 0, 0)),
                             pl.BlockSpec((1, SEQ // TKM, V_ROWS, TKM), lambda h: (h, 0, 0, 0))]


def _head_scratch(n_maps):
    return [pltpu.VMEM((n_maps, 1, TQ), _f32),
            pltpu.VMEM((n_maps, 1, TQ), _f32),
            pltpu.VMEM((n_maps, HEAD_DIM, TQ), _f32)] + [pltpu.VMEM((TKM, TQ + HEAD_DIM), _f32)] * (3 * n_maps)


def _diff_attention(slopes, q1t, q2t, k, vt, lq1, lk1, lq2, lk2, gsub_col):
    vec = lambda: pl.BlockSpec((1, QK_DIFF), lambda h: (0, 0))
    q_specs = _head_specs(2)
    return pl.pallas_call(
        _diff_attn_kernel,
        grid=(N_HEADS,),
        in_specs=[pl.BlockSpec(memory_space=pltpu.SMEM)] + q_specs + [
            vec(), vec(), vec(), vec(), pl.BlockSpec((HEAD_DIM, 1), lambda h: (0, 0))],
        out_specs=pl.BlockSpec((SEQ, HEAD_DIM), lambda h: (0, h)),
        out_shape=jax.ShapeDtypeStruct((SEQ, WIDTH), _bf16),
        scratch_shapes=_head_scratch(2),
        compiler_params=_cparams(("parallel",)),
        name="diff_attention",
    )(slopes, q1t, q2t, k, vt, lq1, lk1, lq2, lk2, gsub_col)


def _fox_attn_kernel(q_ref, k_ref, vt_ref, ft_ref, kx_ref, o_ref, m_ref, l_ref, acc_ref, *s_refs):
    qrow = lax.broadcasted_iota(jnp.int32, (HEAD_DIM, TQS), 0)
    ones_rows = jnp.where(qrow < 3, 1.0, 0.0).astype(_bf16)
    lane, sub = _local_iotas()
    causal = sub <= lane

    def scores_cols(tile, block, c, nk):
        start = pl.multiple_of(block * TKM, TKM)
        k_aug = jnp.concatenate([k_ref[0, pl.ds(start, nk), :], kx_ref[0, pl.ds(start, nk), :]], axis=1)
        q_aug = jnp.concatenate([q_ref[0, tile, :, c * TQS:(c + 1) * TQS], ones_rows], axis=0)
        return (_dot(k_aug, q_aug),)

    def main_chain(tile, block, s_ref, c, idx):
        cs = slice(c * TQS, (c + 1) * TQS)
        mloc, l, o = _block_partial(s_ref[0][:, cs], vt_ref[0, block])
        _merge_partials(m_ref, l_ref, acc_ref, 0, c, [(mloc + ft_ref[0, tile, :, cs], l, o)])

    def band_row(tile, j, s_ref):
        vt = vt_ref[0, tile, :, j * TK:(j + 1) * TK]
        for c in range(j, NSUB):
            cs = slice(c * TQS, (c + 1) * TQS)
            z = s_ref[0][j * TK:(j + 1) * TK, cs]
            if c == j:
                z = jnp.where(causal, z, NEG_BIG)
            mloc, l, o = _block_partial(z, vt)
            _merge_partials(m_ref, l_ref, acc_ref, 0, c, [(mloc + ft_ref[0, tile, :, cs], l, o)])

    def finish_tile(tile):
        o = acc_ref[0] / l_ref[0]
        o_ref[pl.ds(pl.multiple_of(tile * TQ, TQ), TQ), :] = o.T.astype(o_ref.dtype)

    _head_sweep(scores_cols, main_chain, band_row, lambda: _init_state(m_ref, l_ref, acc_ref), finish_tile,
                s_refs[:1], s_refs[1:2], s_refs[2:], unroll_pairs=True)


def _fox_attention(qt, k, vt, ft4, kx):
    nq = SEQ // TQ
    return pl.pallas_call(
        _fox_attn_kernel,
        grid=(N_HEADS,),
        in_specs=_head_specs(1) + [pl.BlockSpec((1, nq, 1, TQ), lambda h: (h, 0, 0, 0)),
                                   pl.BlockSpec((1, SEQ, HEAD_DIM), lambda h: (h, 0, 0))],
        out_specs=pl.BlockSpec((SEQ, HEAD_DIM), lambda h: (0, h)),
        out_shape=jax.ShapeDtypeStruct((SEQ, WIDTH), _bf16),
        scratch_shapes=_head_scratch(1),
        compiler_params=_cparams(("parallel",)),
        name="fox_attention",
    )(qt, k, vt, ft4, kx)


def _merge_kernel(oa_ref, ob_ref, g_ref, x_ref, wbd_ref, wbf_ref, wout_ref, gm_ref, x1_ref, h2_ref):
    for sub in range(oa_ref.shape[0] // PROJ_SUB):
        rows = slice(sub * PROJ_SUB, (sub + 1) * PROJ_SUB)
        a = _dot(oa_ref[rows, :], wbd_ref[...])
        b = _dot(ob_ref[rows, :], wbf_ref[...])
        g = g_ref[rows, :].astype(_f32)
        merged = g[:, :D_MODEL] * a + g[:, D_MODEL:] * b
        x1 = x_ref[rows, :] + _dot(merged.astype(_bf16), wout_ref[...])
        x1_ref[rows, :] = x1
        r = lax.rsqrt(jnp.mean(x1 * x1, axis=-1, keepdims=True) + EPS)
        h2_ref[rows, :] = ((x1 * r) * gm_ref[...]).astype(h2_ref.dtype)


def _merge(oa, ob, gates, x, wbd, wbf, wout, gm, tm=512):
    s, d = x.shape
    const = lambda shape: pl.BlockSpec(shape, lambda i: (0, 0), pipeline_mode=pl.Buffered(1))
    return pl.pallas_call(
        _merge_kernel,
        grid=(s // tm,),
        in_specs=[pl.BlockSpec((tm, WIDTH), lambda i: (i, 0)),
                  pl.BlockSpec((tm, WIDTH), lambda i: (i, 0)),
                  pl.BlockSpec((tm, 2 * d), lambda i: (i, 0)),
                  pl.BlockSpec((tm, d), lambda i: (i, 0)),
                  const((WIDTH, d)), const((WIDTH, d)), const((d, d)),
                  pl.BlockSpec((1, d), lambda i: (0, 0))],
        out_specs=[pl.BlockSpec((tm, d), lambda i: (i, 0)),
                   pl.BlockSpec((tm, d), lambda i: (i, 0))],
        out_shape=[jax.ShapeDtypeStruct((s, d), _f32),
                   jax.ShapeDtypeStruct((s, d), _bf16)],
        compiler_params=_cparams(("parallel",)),
        name="merge_out_proj",
    )(oa, ob, gates, x, wbd, wbf, wout, gm.reshape(1, d))


def _mlp_kernel(h_ref, wu_ref, wd_ref, x_ref, o_ref):
    k = pl.program_id(1)

    @pl.when(k == 0)
    def _():
        o_ref[...] = x_ref[...]

    for sub in range(h_ref.shape[0] // PROJ_SUB):
        rows = slice(sub * PROJ_SUB, (sub + 1) * PROJ_SUB)
        u = jnp.maximum(_dot(h_ref[rows, :], wu_ref[...]), 0.0)
        o_ref[rows, :] += _dot((u * u).astype(_bf16), wd_ref[...])


def _mlp(h2, wu, wd, x1, tm=512, tf=2048):
    s, d = x1.shape
    f = wu.shape[1]
    return pl.pallas_call(
        _mlp_kernel,
        grid=(s // tm, f // tf),
        in_specs=[pl.BlockSpec((tm, d), lambda i, k: (i, 0)),
                  pl.BlockSpec((d, tf), lambda i, k: (0, k)),
                  pl.BlockSpec((tf, d), lambda i, k: (k, 0)),
                  pl.BlockSpec((tm, d), lambda i, k: (i, 0))],
        out_specs=pl.BlockSpec((tm, d), lambda i, k: (i, 0)),
        out_shape=jax.ShapeDtypeStruct((s, d), _f32),
        compiler_params=_cparams(("parallel", "arbitrary")),
        name="mlp_relu2",
    )(h2, wu, wd, x1)


def kernel(x, norm_mix, w_in, b_forget, qnorm_diff, knorm_diff, lambda_q1, lambda_k1, lambda_q2, lambda_k2,
           subln_diff, qnorm_fox, knorm_fox, w_branch_diff, w_branch_fox, w_gate, b_gate, w_out, norm_mlp,
           w_mlp_up, w_mlp_down):
    assert x.shape == (1, SEQ, D_MODEL)
    x2 = x[0]
    w_in_t = jnp.swapaxes(w_in, 1, 2)

    bf_col = jnp.zeros((F_ROWS, 1), _f32).at[:N_HEADS, 0].set(b_forget[0])
    h, ft, kx_fox = _norm_and_forget(x2, norm_mix[0], w_in_t, bf_col)
    ft4 = ft.reshape(F_ROWS, SEQ // TQ, 1, TQ)

    g_qd = jnp.tile(qnorm_diff[0], 2).reshape(1, HEAD_DIM)
    g_kd = jnp.tile(knorm_diff[0], 2).reshape(1, HEAD_DIM)
    tm = PROJ_TM
    t_shape = jax.ShapeDtypeStruct((N_HEADS, SEQ // TQ, HEAD_DIM, TQ), _bf16)
    k_shape = jax.ShapeDtypeStruct((N_HEADS, SEQ, HEAD_DIM), _bf16)
    vt_shape = jax.ShapeDtypeStruct((N_HEADS, SEQ // TKM, V_ROWS, TKM), _bf16)
    vt_spec = pl.BlockSpec((N_HEADS, tm // TKM, V_ROWS, TKM), lambda i: (0, i, 0, 0))

    q1t, q2t = _proj_call(_proj_qdiff_kernel, h, w_in_t, 0, [g_qd], [t_shape, t_shape],
                          [_t_spec(TQ), _t_spec(TQ)], "proj_q_diff", tm=TQ)
    ka = _proj_call(functools.partial(_proj_k_kernel, groups=2), h, w_in_t, 1, [g_kd], k_shape,
                    _k_spec(tm), "proj_k_diff")
    vat = _proj_call(_proj_vt_kernel, h, w_in_t, 2, [], vt_shape, vt_spec, "proj_v_diff")
    qbt = _proj_call(_proj_qfox_kernel, h, w_in_t, 3, [qnorm_fox[0].reshape(1, HEAD_DIM)], t_shape,
                     _t_spec(tm), "proj_q_fox")
    kb = _proj_call(functools.partial(_proj_k_kernel, groups=1), h, w_in_t, 4,
                    [knorm_fox[0].reshape(1, HEAD_DIM)], k_shape, _k_spec(tm), "proj_k_fox")
    vbt = _proj_call(_proj_vt_kernel, h, w_in_t, 5, [], vt_shape, vt_spec, "proj_v_fox")

    gates, (wbd, wbf, wout, wup, wdown) = _gates(
        h, w_gate, b_gate[0], [w_branch_diff, w_branch_fox, w_out, w_mlp_up, w_mlp_down])

    slopes = 2.0 ** (-8.0 * jnp.arange(1, N_HEADS + 1, dtype=_f32) / N_HEADS)
    row = lambda v: v[0].reshape(1, QK_DIFF)
    oa = _diff_attention(slopes, q1t, q2t, ka, vat, row(lambda_q1), row(lambda_k1), row(lambda_q2),
                         row(lambda_k2), subln_diff[0].reshape(HEAD_DIM, 1))
    ob = _fox_attention(qbt, kb, vbt, ft4, kx_fox)

    x1, h2 = _merge(oa, ob, gates, x2, wbd, wbf, wout, norm_mlp[0])
    out = _mlp(h2, wup, wdown, x1)
    return out[None]
```
